```python
import math
import jax, jax.numpy as jnp
from jax import lax
import numpy as np

D_MODEL = 1024
BATCH = 8
SEQ = 2048
DEPTH = 4
DEC_BATCH = 128
DEC_SEQ = 8
PAST_LEN = 8192
PAGE_SIZE = 128

D_MIX = D_MODEL
A_HEADS = 4
A_WIDTH = D_MIX // 4
A_HEAD_DIM = A_WIDTH // A_HEADS
CHUNK = 128
B_HEADS = 4
B_WIDTH = 3 * D_MIX // 8
B_HEAD_DIM = B_WIDTH // B_HEADS
MLSTM_CHUNK = 128
MLSTM_KSCALE = B_HEAD_DIM ** -0.5
C_WIDTH = D_MIX - A_WIDTH - B_WIDTH
C_HEADS = 6
C_V_DIM = C_WIDTH // C_HEADS
C_NOPE = 64
C_ROPE = 32
Q_LORA = 192
KV_LORA = 128
LATENT = KV_LORA + C_ROPE
MLA_SCALE = (C_NOPE + C_ROPE) ** -0.5
ROPE_THETA = 10000.0
ATTN_QBLOCK = 128
MEM_LEN = 256
X_HEADS = 4
X_HEAD_DIM = D_MODEL // X_HEADS
XA_SCALE = X_HEAD_DIM ** -0.5
D_FF = 2816
CONV_W = 3
EPS = 1e-6

IN_SIZES = (A_WIDTH, A_WIDTH,
            B_WIDTH, B_WIDTH, B_WIDTH, B_WIDTH, B_HEADS, B_HEADS,
            Q_LORA, KV_LORA, C_ROPE)
D_IN = sum(IN_SIZES)
SPLIT_IDX = [int(i) for i in np.cumsum(IN_SIZES)[:-1]]

kernel_name = 'hymba_chunkmlp_mlstm_mla_decoder_step'


def rmsnorm(x, w):
    xf = x.astype(jnp.float32)
    y = xf * lax.rsqrt(jnp.mean(xf * xf, axis=-1, keepdims=True) + EPS)
    return (y * w.astype(jnp.float32)).astype(x.dtype)


def rope(x, pos):
    half = C_ROPE // 2
    inv = ROPE_THETA ** (-jnp.arange(half, dtype=jnp.float32) / half)
    ang = pos.astype(jnp.float32)[:, None] * inv[None, :]
    cos, sin = jnp.cos(ang), jnp.sin(ang)
    if x.ndim == 4:
        cos, sin = cos[:, None, :], sin[:, None, :]
    xf = x.astype(jnp.float32)
    x1, x2 = xf[..., :half], xf[..., half:]
    return jnp.concatenate([x1 * cos - x2 * sin, x2 * cos + x1 * sin], axis=-1).astype(x.dtype)


def chunk_mlp(u, v, ws, bs):
    nb, T, _ = v.shape
    L = min(T, CHUNK)
    nc = T // L
    wsm = jnp.tril(ws[:, :L, :L])
    vh = v.reshape(nb, nc, L, A_HEADS, A_HEAD_DIM)
    z = jnp.einsum('hts,bcshd->bcthd', wsm, vh) + bs[:, :L].T[:, :, None]
    return u * z.reshape(nb, T, A_WIDTH)


def mlstm_chunk(carry, inp):
    C, n, m = carry
    q, k, v, ig, lf = inp
    L = q.shape[2]
    b = jnp.cumsum(lf, axis=-1)
    logD = ig[..., None, :] + b[..., :, None] - b[..., None, :]
    causal = jnp.tril(jnp.ones((L, L), dtype=bool))
    logD = jnp.where(causal, logD, -jnp.inf)
    m_t = jnp.maximum(b + m[..., None], jnp.max(logD, axis=-1))
    D = jnp.exp(logD - m_t[..., None])
    g = jnp.exp(b + m[..., None] - m_t)
    qk = jnp.einsum('bhtd,bhsd->bhts', q, k) * D
    num = g[..., None] * jnp.einsum('bhed,bhtd->bhte', C, q) + jnp.einsum('bhts,bhse->bhte', qk, v)
    den = g * jnp.einsum('bhd,bhtd->bht', n, q) + jnp.sum(qk, axis=-1)
    h = num / jnp.maximum(jnp.abs(den), jnp.exp(-m_t))[..., None]
    m_new = m_t[..., -1]
    w_s = D[..., -1, :]
    decay = g[..., -1]
    C_new = decay[..., None, None] * C + jnp.einsum('bhs,bhse,bhsd->bhed', w_s, v, k)
    n_new = decay[..., None] * n + jnp.einsum('bhs,bhsd->bhd', w_s, k)
    return (C_new, n_new, m_new), h


def mlstm(q, k, v, ig, lf, C0, n0, m0):
    nb, T, _ = q.shape
    L = min(T, MLSTM_CHUNK)
    nc = T // L

    def heads(a):
        return a.astype(jnp.float32).reshape(nb, nc, L, B_HEADS, B_HEAD_DIM).transpose(1, 0, 3, 2, 4)

    def gates(a):
        return a.astype(jnp.float32).reshape(nb, nc, L, B_HEADS).transpose(1, 0, 3, 2)

    carry0 = (C0.astype(jnp.float32), n0.astype(jnp.float32), m0.astype(jnp.float32))
    (C, n, m), h = lax.scan(mlstm_chunk, carry0,
                            (heads(q), heads(k) * MLSTM_KSCALE, heads(v), gates(ig), gates(lf)))
    h = h.transpose(1, 0, 3, 2, 4).reshape(nb, T, B_HEADS, B_HEAD_DIM)
    return h, (C, n, m)


def mla_attn_prompt(q_lat, q_rope, c_kv, k_r):
    nb, S, H, _ = q_lat.shape
    QB = min(ATTN_QBLOCK, S)
    n_blk = S // QB
    ql = q_lat.reshape(nb, n_blk, QB, H, KV_LORA).transpose(1, 0, 2, 3, 4)
    qr = q_rope.reshape(nb, n_blk, QB, H, C_ROPE).transpose(1, 0, 2, 3, 4)
    kpos = jnp.arange(S)

    def block(args):
        i, ql_b, qr_b = args
        s = (jnp.einsum('bqhc,bkc->bhqk', ql_b, c_kv)
             + jnp.einsum('bqhr,bkr->bhqk', qr_b, k_r)).astype(jnp.float32) * MLA_SCALE
        qpos = i * QB + jnp.arange(QB)
        s = jnp.where(kpos[None, :] <= qpos[:, None], s, -jnp.inf)
        p = jax.nn.softmax(s, axis=-1).astype(c_kv.dtype)
        return jnp.einsum('bhqk,bkc->bqhc', p, c_kv)

    o = lax.map(block, (jnp.arange(n_blk), ql, qr))
    return o.transpose(1, 0, 2, 3, 4).reshape(nb, S, H, KV_LORA)


def mla_attn_sample(q_lat, q_rope, c_kv, k_r, past):
    c_past, r_past = past[..., :KV_LORA], past[..., KV_LORA:]
    T = q_lat.shape[1]
    P = past.shape[1]
    s_past = (jnp.einsum('bqhc,bkc->bhqk', q_lat, c_past)
              + jnp.einsum('bqhr,bkr->bhqk', q_rope, r_past)).astype(jnp.float32) * MLA_SCALE
    s_new = (jnp.einsum('bqhc,bkc->bhqk', q_lat, c_kv)
             + jnp.einsum('bqhr,bkr->bhqk', q_rope, k_r)).astype(jnp.float32) * MLA_SCALE
    s_new = jnp.where(jnp.tril(jnp.ones((T, T), dtype=bool)), s_new, -jnp.inf)
    p = jax.nn.softmax(jnp.concatenate([s_past, s_new], axis=-1), axis=-1).astype(c_kv.dtype)
    return (jnp.einsum('bhqk,bkc->bqhc', p[..., :P], c_past)
            + jnp.einsum('bhqk,bkc->bqhc', p[..., P:], c_kv))


def cross_attn(h, k, v, wq, wo):
    q = jnp.einsum('btd,dhe->bthe', h, wq)
    s = jnp.einsum('bthe,bmhe->bhtm', q, k).astype(jnp.float32) * XA_SCALE
    p = jax.nn.softmax(s, axis=-1).astype(v.dtype)
    o = jnp.einsum('bhtm,bmhe->bthe', p, v)
    return jnp.einsum('bthe,hed->btd', o, wo)


def conv_ffn(h, buf, wg, wu, cw, cb, wd):
    T = h.shape[1]
    g = h @ wg
    gp = jnp.concatenate([buf.astype(g.dtype), g], axis=1)
    y = cb + sum(gp[:, j:j + T] * cw[j] for j in range(CONV_W))
    out = (jax.nn.gelu(y) * (h @ wu)) @ wd
    return out, gp[:, T:]


def head_group_mixer(h, pos, lp, mlstm_state, mla_past):
    nb, T, _ = h.shape
    au, av, bq, bk, bv, bo, bi, bf, cq, ckv, ckr = jnp.split(h @ lp['w_in'], SPLIT_IDX, axis=-1)
    av = jax.nn.gelu(av)
    ya = chunk_mlp(jax.nn.gelu(au), av, lp['chunk_ws'], lp['chunk_b'])
    ig = bi.astype(jnp.float32) + lp['mlstm_bi'].astype(jnp.float32)
    lf = jax.nn.log_sigmoid(bf.astype(jnp.float32) + lp['mlstm_bf'].astype(jnp.float32))
    hb, new_mlstm = mlstm(bq, bk, bv, ig, lf, *mlstm_state)
    hb = rmsnorm(hb.astype(h.dtype), lp['mlstm_norm_w'].reshape(B_HEADS, B_HEAD_DIM)).reshape(nb, T, B_WIDTH)
    yb = jax.nn.sigmoid(bo) * hb
    c_q = rmsnorm(cq, lp['mla_qnorm_w'])
    q = jnp.einsum('btc,chr->bthr', c_q, lp['mla_w_uq'])
    q_rope = rope(q[..., C_NOPE:], pos)
    q_lat = jnp.einsum('bthd,chd->bthc', q[..., :C_NOPE], lp['mla_w_uk'])
    c_kv = rmsnorm(ckv, lp['mla_kvnorm_w'])
    k_r = rope(ckr, pos)
    if mla_past is None:
        o_lat = mla_attn_prompt(q_lat, q_rope, c_kv, k_r)
    else:
        o_lat = mla_attn_sample(q_lat, q_rope, c_kv, k_r, mla_past)
    yc = jnp.einsum('bthc,che->bthe', o_lat, lp['mla_w_uv']).reshape(nb, T, C_WIDTH)
    y = jnp.concatenate([ya, yb, yc], axis=-1) @ lp['w_out']
    return y, new_mlstm, jnp.concatenate([c_kv, k_r], axis=-1), av


def setup_inputs(seed: int = 0) -> dict:
    key = jax.random.key(seed)
    ks = iter(jax.random.split(key, 64))

    def nrm(shape, scale=1.0):
        return jax.random.normal(next(ks), shape, jnp.float32) * scale

    def gain(shape):
        return 1.0 + nrm(shape, 0.02)

    n_pages = PAST_LEN // PAGE_SIZE
    n_phys = (DEC_BATCH * n_pages * 5) // 4
    page_table = jax.random.permutation(next(ks), n_phys)[:DEC_BATCH * n_pages]
    page_table = page_table.reshape(DEC_BATCH, n_pages).astype(jnp.int32)
    return {
        'x_prompt': nrm((BATCH, SEQ, D_MODEL)),
        'x_sample': nrm((DEC_BATCH, DEC_SEQ, D_MODEL)),
        'mem_prompt': nrm((BATCH, MEM_LEN, D_MODEL)),
        'cache_mla': nrm((DEPTH, n_phys, PAGE_SIZE, LATENT)),
        'page_table': page_table,
        'cache_mem_k': nrm((DEPTH, DEC_BATCH, MEM_LEN, X_HEADS, X_HEAD_DIM)),
        'cache_mem_v': nrm((DEPTH, DEC_BATCH, MEM_LEN, X_HEADS, X_HEAD_DIM)),
        'state_mlstm_C': nrm((DEPTH, DEC_BATCH, B_HEADS, B_HEAD_DIM, B_HEAD_DIM), 0.1),
        'state_mlstm_n': nrm((DEPTH, DEC_BATCH, B_HEADS, B_HEAD_DIM), 0.5),
        'state_mlstm_m': nrm((DEPTH, DEC_BATCH, B_HEADS)),
        'state_ffn_conv': nrm((DEPTH, DEC_BATCH, CONV_W - 1, D_FF)),
        'norm_mix_w': gain((DEPTH, D_MODEL)),
        'w_in': nrm((DEPTH, D_MODEL, D_IN), D_MODEL ** -0.5),
        'chunk_ws': nrm((DEPTH, A_HEADS, CHUNK, CHUNK), CHUNK ** -0.5),
        'chunk_b': 1.0 + nrm((DEPTH, A_HEADS, CHUNK), 0.1),
        'mlstm_bi': nrm((DEPTH, B_HEADS), 0.1),
        'mlstm_bf': jnp.linspace(3.0, 6.0, B_HEADS)[None, :] + nrm((DEPTH, B_HEADS), 0.1),
        'mlstm_norm_w': gain((DEPTH, B_WIDTH)),
        'mla_qnorm_w': gain((DEPTH, Q_LORA)),
        'mla_kvnorm_w': gain((DEPTH, KV_LORA)),
        'mla_w_uq': nrm((DEPTH, Q_LORA, C_HEADS, C_NOPE + C_ROPE), Q_LORA ** -0.5),
        'mla_w_uk': nrm((DEPTH, KV_LORA, C_HEADS, C_NOPE), KV_LORA ** -0.5),
        'mla_w_uv': nrm((DEPTH, KV_LORA, C_HEADS, C_V_DIM), KV_LORA ** -0.5),
        'w_out': nrm((DEPTH, D_MIX, D_MODEL), D_MIX ** -0.5),
        'norm_x_w': gain((DEPTH, D_MODEL)),
        'norm_mem_w': gain((DEPTH, D_MODEL)),
        'xa_wq': nrm((DEPTH, D_MODEL, X_HEADS, X_HEAD_DIM), D_MODEL ** -0.5),
        'xa_wk': nrm((DEPTH, D_MODEL, X_HEADS, X_HEAD_DIM), D_MODEL ** -0.5),
        'xa_wv': nrm((DEPTH, D_MODEL, X_HEADS, X_HEAD_DIM), D_MODEL ** -0.5),
        'xa_wo': nrm((DEPTH, X_HEADS, X_HEAD_DIM, D_MODEL), D_MODEL ** -0.5),
        'norm_ffn_w': gain((DEPTH, D_MODEL)),
        'ffn_wg': nrm((DEPTH, D_MODEL, D_FF), D_MODEL ** -0.5),
        'ffn_wu': nrm((DEPTH, D_MODEL, D_FF), D_MODEL ** -0.5),
        'ffn_conv_w': nrm((DEPTH, CONV_W, D_FF), CONV_W ** -0.5),
        'ffn_conv_b': nrm((DEPTH, D_FF), 0.02),
        'ffn_wd': nrm((DEPTH, D_FF, D_MODEL), D_FF ** -0.5),
        'norm_final_w': gain((D_MODEL,)),
    }


def reference(x_prompt, x_sample, mem_prompt, cache_mla, page_table, cache_mem_k, cache_mem_v,
              state_mlstm_C, state_mlstm_n, state_mlstm_m, state_ffn_conv,
              norm_mix_w, w_in, chunk_ws, chunk_b, mlstm_bi, mlstm_bf, mlstm_norm_w,
              mla_qnorm_w, mla_kvnorm_w, mla_w_uq, mla_w_uk, mla_w_uv, w_out,
              norm_x_w, norm_mem_w, xa_wq, xa_wk, xa_wv, xa_wo,
              norm_ffn_w, ffn_wg, ffn_wu, ffn_conv_w, ffn_conv_b, ffn_wd, norm_final_w):
    xp, xs = x_prompt, x_sample
    nbp, S, _ = xp.shape
    nbs, T, _ = xs.shape
    pos_p = jnp.arange(S)
    pos_s = PAST_LEN + jnp.arange(T)
    zero_mlstm = (jnp.zeros((nbp, B_HEADS, B_HEAD_DIM, B_HEAD_DIM), jnp.float32),
                  jnp.zeros((nbp, B_HEADS, B_HEAD_DIM), jnp.float32),
                  jnp.zeros((nbp, B_HEADS), jnp.float32))
    zero_conv = jnp.zeros((nbp, CONV_W - 1, D_FF), xp.dtype)

    lat_p, lat_s, memk_p, memv_p = [], [], [], []
    Cp, np_, mp, Cs, ns, ms = [], [], [], [], [], []
    conv_p, conv_s, chunkv_s = [], [], []
    for l in range(DEPTH):
        lp = {'w_in': w_in[l], 'chunk_ws': chunk_ws[l], 'chunk_b': chunk_b[l],
              'mlstm_bi': mlstm_bi[l], 'mlstm_bf': mlstm_bf[l], 'mlstm_norm_w': mlstm_norm_w[l],
              'mla_qnorm_w': mla_qnorm_w[l], 'mla_kvnorm_w': mla_kvnorm_w[l],
              'mla_w_uq': mla_w_uq[l], 'mla_w_uk': mla_w_uk[l], 'mla_w_uv': mla_w_uv[l],
              'w_out': w_out[l]}
        y, st, lat, _ = head_group_mixer(rmsnorm(xp, norm_mix_w[l]), pos_p, lp, zero_mlstm, None)
        xp = xp + y
        Cp.append(st[0]); np_.append(st[1]); mp.append(st[2]); lat_p.append(lat)
        past = cache_mla[l][page_table].reshape(nbs, -1, LATENT)
        y, st, lat, v_new = head_group_mixer(rmsnorm(xs, norm_mix_w[l]), pos_s, lp,
                                             (state_mlstm_C[l], state_mlstm_n[l], state_mlstm_m[l]), past)
        xs = xs + y
        Cs.append(st[0]); ns.append(st[1]); ms.append(st[2]); lat_s.append(lat); chunkv_s.append(v_new)
        mem_n = rmsnorm(mem_prompt, norm_mem_w[l])
        mk = jnp.einsum('bmd,dhe->bmhe', mem_n, xa_wk[l])
        mv = jnp.einsum('bmd,dhe->bmhe', mem_n, xa_wv[l])
        memk_p.append(mk); memv_p.append(mv)
        xp = xp + cross_attn(rmsnorm(xp, norm_x_w[l]), mk, mv, xa_wq[l], xa_wo[l])
        xs = xs + cross_attn(rmsnorm(xs, norm_x_w[l]), cache_mem_k[l], cache_mem_v[l], xa_wq[l], xa_wo[l])
        y, buf = conv_ffn(rmsnorm(xp, norm_ffn_w[l]), zero_conv, ffn_wg[l], ffn_wu[l],
                          ffn_conv_w[l], ffn_conv_b[l], ffn_wd[l])
        xp = xp + y
        conv_p.append(buf)
        y, buf = conv_ffn(rmsnorm(xs, norm_ffn_w[l]), state_ffn_conv[l], ffn_wg[l], ffn_wu[l],
                          ffn_conv_w[l], ffn_conv_b[l], ffn_wd[l])
        xs = xs + y
        conv_s.append(buf)

    y_prompt = rmsnorm(xp, norm_final_w)
    y_sample = rmsnorm(xs, norm_final_w)
    return (y_prompt, y_sample,
            jnp.stack(lat_p), jnp.stack(lat_s),
            jnp.stack(memk_p), jnp.stack(memv_p),
            jnp.stack(Cp), jnp.stack(np_), jnp.stack(mp),
            jnp.stack(Cs), jnp.stack(ns), jnp.stack(ms),
            jnp.stack(conv_p), jnp.stack(conv_s),
            jnp.stack(chunkv_s))
```

```python
import functools
import math

import numpy as np
import jax
import jax.numpy as jnp
from jax import lax
from jax.experimental import pallas as pl
from jax.experimental.pallas import tpu as pltpu

F32 = jnp.float32
BF16 = jnp.bfloat16

D_MODEL = 1024
PAGE_SIZE = 128
A_HEADS = 4
A_WIDTH = 256
A_HEAD_DIM = 64
CHUNK = 128
B_HEADS = 4
B_WIDTH = 384
B_HEAD_DIM = 96
MLSTM_KSCALE = B_HEAD_DIM ** -0.5
C_WIDTH = 384
C_HEADS = 6
C_V_DIM = 64
C_NOPE = 64
C_ROPE = 32
Q_LORA = 192
KV_LORA = 128
LATENT = KV_LORA + C_ROPE
MLA_SCALE = (C_NOPE + C_ROPE) ** -0.5
ROPE_THETA = 10000.0
MEM_LEN = 256
X_HEADS = 4
X_HEAD_DIM = 256
XA_SCALE = X_HEAD_DIM ** -0.5
D_FF = 2816
CONV_W = 3
EPS = 1e-6

LANE = 128
HP = 128
BW_P = B_HEADS * HP
FF_CHUNK = 256
N_FF_CHUNKS = D_FF // FF_CHUNK
VMEM_LIMIT = 48 * 1024 * 1024

S_AU, S_AV, S_BQ, S_BK, S_BV, S_BO = 0, 256, 512, 1024, 1536, 2048
S_CQ, S_CKV, S_R1, S_R2, S_G = 2560, 2816, 2944, 3072, 3200
N_IN = 3328
CQ_P = 256
UQ_NOPE, UQ_RA, UQ_RB, N_UQ = 0, 384, 1152, 1920
QC_SLOT = 256
N_MIX_P = A_WIDTH + BW_P + C_WIDTH


def _rms(x, w, n):
    ms = jnp.sum(x * x, axis=-1, keepdims=True) * (1.0 / n)
    return x * lax.rsqrt(ms + EPS) * w


def _gelu(x):
    c = math.sqrt(2.0 / math.pi)
    return x * (0.5 * (1.0 + jnp.tanh(c * (x + 0.044715 * (x * x * x)))))


def _sigmoid(x):
    return 1.0 / (1.0 + jnp.exp(-x))


def _log_sigmoid(x):
    return jnp.minimum(x, 0.0) - jnp.log1p(jnp.exp(-jnp.abs(x)))


def _dot(a, b):
    return jnp.dot(a, b, preferred_element_type=F32)


def _dot_nt(a, b):
    return lax.dot_general(a, b, (((1,), (1,)), ((), ())), preferred_element_type=F32)


def _dot_tn(a, b):
    return lax.dot_general(a, b, (((0,), (0,)), ((), ())), preferred_element_type=F32)


def _split3(a):
    a1 = a.astype(BF16)
    r1 = a - a1.astype(F32)
    a2 = r1.astype(BF16)
    a3 = (r1 - a2.astype(F32)).astype(BF16)
    return a1, a2, a3


def _wspec(shape):
    nd = len(shape)
    return pl.BlockSpec(shape, lambda *_: (0,) * nd, pipeline_mode=pl.Buffered(1))


def _params(n_axes):
    return pltpu.CompilerParams(dimension_semantics=("arbitrary",) * n_axes,
                                vmem_limit_bytes=VMEM_LIMIT)


def _inproj_kernel(x_ref, nw_ref, w1_ref, qnw_ref, kvnw_ref, wuq_ref, wuk_ref, gb_ref, cos_ref, sin_ref,
                   gu_ref, gv_ref, q_ref, k_ref, v_ref, so_ref, qc_ref, lat_ref, gate_ref):
    xn = _rms(x_ref[...], nw_ref[...], D_MODEL).astype(BF16)

    def seg(a, b):
        return _dot(xn, w1_ref[:, a:b])

    gu_ref[...] = _gelu(seg(S_AU, S_AV))
    gv_ref[...] = _gelu(seg(S_AV, S_BQ))
    q_ref[...] = seg(S_BQ, S_BK).astype(q_ref.dtype)
    k_ref[...] = (seg(S_BK, S_BV) * MLSTM_KSCALE).astype(k_ref.dtype)
    v_ref[...] = seg(S_BV, S_BO).astype(v_ref.dtype)
    so_ref[...] = _sigmoid(seg(S_BO, S_CQ))

    c_q = _rms(seg(S_CQ, S_CKV), qnw_ref[...], Q_LORA).astype(BF16)
    q2 = _dot(c_q, wuq_ref[...])
    qlat = _dot(q2[:, UQ_NOPE:UQ_RA].astype(BF16), wuk_ref[...])
    cos = cos_ref[...]
    sin = sin_ref[...]
    for h in range(C_HEADS):
        ra = q2[:, UQ_RA + LANE * h:UQ_RA + LANE * (h + 1)]
        rb = q2[:, UQ_RB + LANE * h:UQ_RB + LANE * (h + 1)]
        qr = ra * cos + rb * sin
        qc_ref[:, QC_SLOT * h:QC_SLOT * h + LANE] = (qlat[:, LANE * h:LANE * (h + 1)] * MLA_SCALE).astype(qc_ref.dtype)
        qc_ref[:, QC_SLOT * h + LANE:QC_SLOT * (h + 1)] = (qr * MLA_SCALE).astype(qc_ref.dtype)

    lat_ref[:, 0:KV_LORA] = _rms(seg(S_CKV, S_R1), kvnw_ref[...], KV_LORA)
    kr = seg(S_R1, S_R2) * cos + seg(S_R2, S_G) * sin
    lat_ref[:, KV_LORA:LATENT] = kr[:, :C_ROPE]

    gt = seg(S_G, N_IN) + gb_ref[...]
    lane = lax.broadcasted_iota(jnp.int32, gt.shape, 1)
    gate_ref[...] = jnp.where(lane < B_HEADS, gt, jnp.where(lane < 2 * B_HEADS, _log_sigmoid(gt), 0.0))


def _inproj(x, nw, w1, qnw, kvnw, wuq, wuk, gb, cos_t, sin_t, *, tm, act_dtype):
    m = x.shape[0]
    n_pos_blocks = cos_t.shape[0] // tm
    row = lambda w: pl.BlockSpec((tm, w), lambda i: (i, 0))
    pos = pl.BlockSpec((tm, LANE), lambda i: (i % n_pos_blocks, 0))
    out_shapes = (
        jax.ShapeDtypeStruct((m, A_WIDTH), F32), jax.ShapeDtypeStruct((m, A_WIDTH), F32),
        jax.ShapeDtypeStruct((m, BW_P), act_dtype), jax.ShapeDtypeStruct((m, BW_P), act_dtype),
        jax.ShapeDtypeStruct((m, BW_P), act_dtype), jax.ShapeDtypeStruct((m, BW_P), F32),
        jax.ShapeDtypeStruct((m, C_HEADS * QC_SLOT), act_dtype), jax.ShapeDtypeStruct((m, LATENT), F32),
        jax.ShapeDtypeStruct((m, LANE), F32))
    return pl.pallas_call(
        _inproj_kernel,
        grid=(m // tm,),
        in_specs=[row(D_MODEL), _wspec((1, D_MODEL)), _wspec((D_MODEL, N_IN)), _wspec((1, CQ_P)),
                  _wspec((1, KV_LORA)), _wspec((CQ_P, N_UQ)), _wspec((C_HEADS * C_NOPE, C_HEADS * LANE)),
                  _wspec((1, LANE)), pos, pos],
        out_specs=(row(A_WIDTH), row(A_WIDTH), row(BW_P), row(BW_P), row(BW_P), row(BW_P),
                   row(C_HEADS * QC_SLOT), row(LATENT), row(LANE)),
        out_shape=out_shapes,
        compiler_params=_params(1),
        name="inproj",
    )(x, nw, w1, qnw, kvnw, wuq, wuk, gb, cos_t, sin_t)


def _chunkmlp_kernel(gu_ref, gv_ref, ws_ref, bias_ref, ya_ref, *, n_chunks):
    r = lax.broadcasted_iota(jnp.int32, (CHUNK, CHUNK), 0)
    c = lax.broadcasted_iota(jnp.int32, (CHUNK, CHUNK), 1)
    wsm = [jnp.where(r >= c, ws_ref[h], 0.0).astype(BF16) for h in range(A_HEADS)]
    head_of_lane = lax.broadcasted_iota(jnp.int32, (CHUNK, A_WIDTH), 1) >> (A_HEAD_DIM.bit_length() - 1)
    for ci in range(n_chunks):
        rows = slice(ci * CHUNK, (ci + 1) * CHUNK)
        gv = gv_ref[rows, :]
        z = bias_ref[...]
        for h in range(A_HEADS):
            z = z + _dot(wsm[h], jnp.where(head_of_lane == h, gv, 0.0).astype(BF16))
        ya_ref[rows, :] = gu_ref[rows, :] * z


def _chunkmlp(gu, gv, ws, bias, *, tm):
    m = gu.shape[0]
    row = pl.BlockSpec((tm, A_WIDTH), lambda i: (i, 0))
    return pl.pallas_call(
        functools.partial(_chunkmlp_kernel, n_chunks=tm // CHUNK),
        grid=(m // tm,),
        in_specs=[row, row, _wspec((A_HEADS, CHUNK, CHUNK)), _wspec((CHUNK, A_WIDTH))],
        out_specs=row,
        out_shape=jax.ShapeDtypeStruct((m, A_WIDTH), F32),
        compiler_params=_params(1),
        name="chunkmlp",
    )(gu, gv, ws, bias)


_COL_M, _COL_G, _COL_RS = 0, 8, 16


def _mlstm_kernel(q_ref, k_ref, v_ref, so_ref, gcol_ref, grow_ref, m0_ref, c0_ref, n0_ref, nw_ref,
                  yb_ref, cout_ref, nout_ref, mout_ref,
                  cs_ref, ns_ref, mp_ref, col_ref, sv_ref, qc_ref, nr_ref, *, ns, ls):
    L = CHUNK
    ci = pl.program_id(1)

    ls_shift = ls.bit_length() - 1

    @pl.when(ci == 0)
    def _init():
        cs_ref[...] = jnp.zeros(cs_ref.shape, F32)
        ns_ref[...] = jnp.zeros(ns_ref.shape, F32)
        for g in range(ns):
            for h in range(B_HEADS):
                cs_ref[g * B_HEADS + h, :B_HEAD_DIM, :B_HEAD_DIM] = c0_ref[g, h]
                ns_ref[g * B_HEADS + h, 0:1, :B_HEAD_DIM] = n0_ref[g, h:h + 1, :]
        mp_ref[...] = m0_ref[...]

    col_ref[...] = jnp.zeros(col_ref.shape, F32)
    row = lax.broadcasted_iota(jnp.int32, (L, L), 0)
    col = lax.broadcasted_iota(jnp.int32, (L, L), 1)
    causal = (col <= row) & ((col >> ls_shift) == (row >> ls_shift))
    cum = jnp.where(causal, 1.0, 0.0).astype(BF16)
    rid = lax.broadcasted_iota(jnp.int32, (L, 1), 0)

    gcol = gcol_ref[...]
    grow = grow_ref[...]
    bcol = sum(_dot(cum, t) for t in _split3(gcol))
    brow = sum(_dot_nt(t, cum) for t in _split3(grow))
    mp = mp_ref[...]

    qs, ks, vs = [], [], []
    for h in range(B_HEADS):
        hs = slice(h * HP, (h + 1) * HP)
        qh = q_ref[:, hs].astype(BF16)
        kh = k_ref[:, hs].astype(BF16)
        vh = v_ref[:, hs].astype(BF16)
        qs.append(qh), ks.append(kh), vs.append(vh)
        ig_r = grow[h:h + 1, :]
        b_r = brow[B_HEADS + h:B_HEADS + h + 1, :]
        b_c = bcol[:, B_HEADS + h:B_HEADS + h + 1]
        bm = b_c + mp[:, h:h + 1]
        logd = jnp.where(causal, ig_r + b_c - b_r, -jnp.inf)
        m_t = jnp.maximum(bm, jnp.max(logd, axis=-1, keepdims=True))
        d = jnp.exp(logd - m_t)
        s = _dot_nt(qh, kh) * d
        sv_ref[h] = _dot(s.astype(BF16), vh)
        col_ref[:, _COL_M + h:_COL_M + h + 1] = m_t
        col_ref[:, _COL_G + h:_COL_G + h + 1] = jnp.exp(bm - m_t)
        col_ref[:, _COL_RS + h:_COL_RS + h + 1] = jnp.sum(s, axis=-1, keepdims=True)

    qc_ref[...] = jnp.zeros(qc_ref.shape, F32)
    nr_ref[...] = jnp.zeros(nr_ref.shape, F32)

    def seq_update(g):
        last = g * ls + (ls - 1)
        in_seq = (rid >> ls_shift) == g
        is_last = rid == last
        for h in range(B_HEADS):
            c_old = cs_ref[g * B_HEADS + h]
            n_old = ns_ref[g * B_HEADS + h, 0:1, :]
            qm = jnp.where(in_seq, qs[h], jnp.zeros_like(qs[h])) if ns > 1 else qs[h]
            qc_ref[h] += _dot_nt(qm, c_old.astype(BF16))
            nr_ref[h] += jnp.where(in_seq, n_old, 0.0)
            m_t = col_ref[:, _COL_M + h:_COL_M + h + 1]
            gdec = col_ref[:, _COL_G + h:_COL_G + h + 1]
            b_c = bcol[:, B_HEADS + h:B_HEADS + h + 1]
            ig_c = gcol[:, h:h + 1]
            pick = lambda a: jnp.sum(jnp.where(is_last, a, 0.0), axis=0, keepdims=True)
            m_new, b_last, decay = pick(m_t), pick(b_c), pick(gdec)
            w_c = jnp.where(in_seq, jnp.exp(ig_c + b_last - b_c - m_new), 0.0)
            vw = (vs[h].astype(F32) * w_c).astype(BF16)
            cs_ref[g * B_HEADS + h] = decay * c_old + _dot_tn(vw, ks[h])
            ns_ref[g * B_HEADS + h, 0:1, :] = (decay * n_old
                                               + jnp.sum(ks[h].astype(F32) * w_c, axis=0, keepdims=True))
            if ns == 1:
                mp_ref[:, h:h + 1] = jnp.broadcast_to(m_new, (L, 1))

    if ns == 1:
        seq_update(0)
    else:
        lax.fori_loop(0, ns, lambda g, c: (seq_update(g), c)[1], 0)

    for h in range(B_HEADS):
        hs = slice(h * HP, (h + 1) * HP)
        m_t = col_ref[:, _COL_M + h:_COL_M + h + 1]
        gdec = col_ref[:, _COL_G + h:_COL_G + h + 1]
        rs = col_ref[:, _COL_RS + h:_COL_RS + h + 1]
        num = gdec * qc_ref[h] + sv_ref[h]
        qn = jnp.sum(qs[h].astype(F32) * nr_ref[h], axis=-1, keepdims=True)
        den = gdec * qn + rs
        hh = num / jnp.maximum(jnp.abs(den), jnp.exp(-m_t))
        yb_ref[:, hs] = so_ref[:, hs] * _rms(hh, nw_ref[:, hs], B_HEAD_DIM)

    mout_ref[...] = col_ref[...]
    for g in range(ns):
        for h in range(B_HEADS):
            cout_ref[g, h] = cs_ref[g * B_HEADS + h, :B_HEAD_DIM, :B_HEAD_DIM]
            nout_ref[g, h:h + 1, :] = ns_ref[g * B_HEADS + h, 0:1, :B_HEAD_DIM]


def _mlstm(q, k, v, so, gcol, grow, m0, c0, n0, nw, *, ns, n_tiles, n_chunks):
    m = q.shape[0]
    nb = c0.shape[0]
    ls = CHUNK // ns
    rows = lambda w: pl.BlockSpec((CHUNK, w), lambda t, c: (t * n_chunks + c, 0))
    nspec = pl.BlockSpec((ns, B_HEADS, B_HEAD_DIM), lambda t, c: (t, 0, 0))
    return pl.pallas_call(
        functools.partial(_mlstm_kernel, ns=ns, ls=ls),
        grid=(n_tiles, n_chunks),
        in_specs=[rows(BW_P), rows(BW_P), rows(BW_P), rows(BW_P), rows(LANE),
                  pl.BlockSpec((None, 16, CHUNK), lambda t, c: (t * n_chunks + c, 0, 0)),
                  pl.BlockSpec((None, CHUNK, LANE), lambda t, c: (t, 0, 0)),
                  pl.BlockSpec((ns, B_HEADS, B_HEAD_DIM, B_HEAD_DIM), lambda t, c: (t, 0, 0, 0)),
                  nspec,
                  _wspec((1, BW_P))],
        out_specs=(rows(BW_P),
                   pl.BlockSpec((ns, B_HEADS, B_HEAD_DIM, B_HEAD_DIM), lambda t, c: (t, 0, 0, 0)),
                   nspec,
                   pl.BlockSpec((None, CHUNK, LANE), lambda t, c: (t, 0, 0))),
        out_shape=(jax.ShapeDtypeStruct((m, BW_P), F32),
                   jax.ShapeDtypeStruct((nb, B_HEADS, B_HEAD_DIM, B_HEAD_DIM), F32),
                   jax.ShapeDtypeStruct((nb, B_HEADS, B_HEAD_DIM), F32),
                   jax.ShapeDtypeStruct((n_tiles, CHUNK, LANE), F32)),
        scratch_shapes=[pltpu.VMEM((ns * B_HEADS, HP, HP), F32), pltpu.VMEM((ns * B_HEADS, 8, HP), F32),
                        pltpu.VMEM((CHUNK, LANE), F32), pltpu.VMEM((CHUNK, LANE), F32),
                        pltpu.VMEM((B_HEADS, CHUNK, HP), F32), pltpu.VMEM((B_HEADS, CHUNK, HP), F32),
                        pltpu.VMEM((B_HEADS, CHUNK, HP), F32)],
        compiler_params=_params(2),
        name="mlstm",
    )(q, k, v, so, gcol, grow, m0, c0, n0, nw)


def _stack_heads(qc_ref, dtype):
    return jnp.concatenate([qc_ref[:, QC_SLOT * h:QC_SLOT * h + LATENT] for h in range(C_HEADS)],
                           axis=0).astype(dtype)


def _unstack_project(o, t, wuv_ref):
    wide = jnp.concatenate([o[h * t:(h + 1) * t] for h in range(C_HEADS)], axis=1)
    return _dot(wide.astype(BF16), wuv_ref[...])


def _softmax_step(s, v, m_ref, l_ref, acc_ref):
    m_prev = m_ref[...]
    m_new = jnp.maximum(m_prev, jnp.max(s, axis=-1, keepdims=True))
    p = jnp.exp(s - m_new)
    corr = jnp.exp(m_prev - m_new)
    l_ref[...] = corr * l_ref[...] + jnp.sum(p, axis=-1, keepdims=True)
    acc_ref[...] = corr * acc_ref[...] + _dot(p.astype(BF16), v)
    m_ref[...] = m_new


def _mla_prompt_kernel(qc_ref, lat_ref, wuv_ref, yc_ref, kb_ref, m_ref, l_ref, acc_ref, *, tq, tk):
    qi = pl.program_id(1)

    @pl.when(qi == 0)
    def _cast_keys():
        kb_ref[...] = lat_ref[...].astype(BF16)

    qst = _stack_heads(qc_ref, BF16)
    m_ref[...] = jnp.full(m_ref.shape, -jnp.inf, F32)
    l_ref[...] = jnp.zeros(l_ref.shape, F32)
    acc_ref[...] = jnp.zeros(acc_ref.shape, F32)
    rows = C_HEADS * tq
    qpos = qi * tq + (lax.broadcasted_iota(jnp.int32, (rows, tk), 0) & (tq - 1))
    kidx = lax.broadcasted_iota(jnp.int32, (rows, tk), 1)

    def body(j, carry):
        kj = kb_ref[pl.ds(pl.multiple_of(j * tk, tk), tk), :]
        s = jnp.where(j * tk + kidx <= qpos, _dot_nt(qst, kj), -jnp.inf)
        _softmax_step(s, kj[:, :KV_LORA], m_ref, l_ref, acc_ref)
        return carry

    lax.fori_loop(0, (qi * tq + tq + tk - 1) // tk, body, 0)
    yc_ref[...] = _unstack_project(acc_ref[...] / l_ref[...], tq, wuv_ref)


def _mla_prompt(qc, lat, wuv, *, nb, seq, tq, tk):
    m = qc.shape[0]
    n_q = seq // tq
    rows = C_HEADS * tq
    return pl.pallas_call(
        functools.partial(_mla_prompt_kernel, tq=tq, tk=tk),
        grid=(nb, n_q),
        in_specs=[pl.BlockSpec((tq, C_HEADS * QC_SLOT), lambda b, i: (b * n_q + i, 0)),
                  pl.BlockSpec((seq, LATENT), lambda b, i: (b, 0)),
                  _wspec((C_HEADS * KV_LORA, C_WIDTH))],
        out_specs=pl.BlockSpec((tq, C_WIDTH), lambda b, i: (b * n_q + i, 0)),
        out_shape=jax.ShapeDtypeStruct((m, C_WIDTH), F32),
        scratch_shapes=[pltpu.VMEM((seq, LATENT), BF16), pltpu.VMEM((rows, 1), F32), pltpu.VMEM((rows, 1), F32),
                        pltpu.VMEM((rows, KV_LORA), F32)],
        compiler_params=_params(2),
        name="mla_prompt",
    )(qc, lat, wuv)


def _mla_sample_kernel(pt_ref, qc_ref, latn_ref, wuv_ref, *rest, pages_per_step, t_new):
    page_refs = rest[:pages_per_step]
    yc_ref, kb_ref, kn_ref, m_ref, l_ref, acc_ref = rest[pages_per_step:]
    j = pl.program_id(1)
    qst = _stack_heads(qc_ref, BF16)

    @pl.when(j == 0)
    def _init():
        m_ref[...] = jnp.full(m_ref.shape, -jnp.inf, F32)
        l_ref[...] = jnp.zeros(l_ref.shape, F32)
        acc_ref[...] = jnp.zeros(acc_ref.shape, F32)

    for i in range(pages_per_step):
        kb_ref[i * PAGE_SIZE:(i + 1) * PAGE_SIZE, :] = page_refs[i][...].astype(BF16)
    kb = kb_ref[...]
    _softmax_step(_dot_nt(qst, kb), kb[:, :KV_LORA], m_ref, l_ref, acc_ref)

    @pl.when(j == pl.num_programs(1) - 1)
    def _new_rows_and_finish():
        kn_ref[...] = jnp.zeros(kn_ref.shape, F32)
        kn_ref[0:t_new, :] = latn_ref[...]
        kn = kn_ref[...].astype(BF16)
        rows = C_HEADS * t_new
        tpos = lax.broadcasted_iota(jnp.int32, (rows, PAGE_SIZE), 0) & (t_new - 1)
        kidx = lax.broadcasted_iota(jnp.int32, (rows, PAGE_SIZE), 1)
        s = jnp.where(kidx <= tpos, _dot_nt(qst, kn), -jnp.inf)
        _softmax_step(s, kn[:, :KV_LORA], m_ref, l_ref, acc_ref)
        yc_ref[...] = _unstack_project(acc_ref[...] / l_ref[...], t_new, wuv_ref)


def _mla_sample(page_table, qc, lat_new, wuv, cache, *, layer, t_new, pages_per_step):
    m = qc.shape[0]
    nb, n_pages = page_table.shape
    n_steps = n_pages // pages_per_step
    rows = C_HEADS * t_new

    def page_spec(i):
        return pl.BlockSpec((None, None, PAGE_SIZE, LATENT),
                            lambda b, j, pt: (layer, pt[b, j * pages_per_step + i], 0, 0))

    grid_spec = pltpu.PrefetchScalarGridSpec(
        num_scalar_prefetch=1,
        grid=(nb, n_steps),
        in_specs=[pl.BlockSpec((t_new, C_HEADS * QC_SLOT), lambda b, j, pt: (b, 0)),
                  pl.BlockSpec((t_new, LATENT), lambda b, j, pt: (b, 0)),
                  pl.BlockSpec((C_HEADS * KV_LORA, C_WIDTH), lambda b, j, pt: (0, 0))]
                 + [page_spec(i) for i in range(pages_per_step)],
        out_specs=pl.BlockSpec((t_new, C_WIDTH), lambda b, j, pt: (b, 0)),
        scratch_shapes=[pltpu.VMEM((pages_per_step * PAGE_SIZE, LATENT), BF16), pltpu.VMEM((PAGE_SIZE, LATENT), F32),
                        pltpu.VMEM((rows, 1), F32), pltpu.VMEM((rows, 1), F32), pltpu.VMEM((rows, KV_LORA), F32)])
    return pl.pallas_call(
        functools.partial(_mla_sample_kernel, pages_per_step=pages_per_step, t_new=t_new),
        grid_spec=grid_spec,
        out_shape=jax.ShapeDtypeStruct((m, C_WIDTH), F32),
        compiler_params=_params(2),
        name="mla_sample",
    )(page_table, qc, lat_new, wuv, *([cache] * pages_per_step))


def _outproj_kernel(x_ref, ya_ref, yb_ref, yc_ref, wo_ref, nxw_ref, wq_ref, xo_ref, q_ref):
    y = (_dot(ya_ref[...].astype(BF16), wo_ref[0:A_WIDTH, :])
         + _dot(yb_ref[...].astype(BF16), wo_ref[A_WIDTH:A_WIDTH + BW_P, :])
         + _dot(yc_ref[...].astype(BF16), wo_ref[A_WIDTH + BW_P:N_MIX_P, :]))
    xn = x_ref[...] + y
    xo_ref[...] = xn
    hq = _rms(xn, nxw_ref[...], D_MODEL).astype(BF16)
    q_ref[...] = (_dot(hq, wq_ref[...]) * XA_SCALE).astype(q_ref.dtype)


def _outproj(x, ya, yb, yc, wo, nxw, wq, *, tm, q_dtype):
    m = x.shape[0]
    row = lambda w: pl.BlockSpec((tm, w), lambda i: (i, 0))
    return pl.pallas_call(
        _outproj_kernel,
        grid=(m // tm,),
        in_specs=[row(D_MODEL), row(A_WIDTH), row(BW_P), row(C_WIDTH), _wspec((N_MIX_P, D_MODEL)),
                  _wspec((1, D_MODEL)), _wspec((D_MODEL, D_MODEL))],
        out_specs=(row(D_MODEL), row(D_MODEL)),
        out_shape=(jax.ShapeDtypeStruct((m, D_MODEL), F32), jax.ShapeDtypeStruct((m, D_MODEL), q_dtype)),
        compiler_params=_params(1),
        name="outproj",
    )(x, ya, yb, yc, wo, nxw, wq)


def _memkv_kernel(mem_ref, nw_ref, wk_ref, wv_ref, k_ref, v_ref):
    mn = _rms(mem_ref[...], nw_ref[...], D_MODEL).astype(BF16)
    k_ref[...] = _dot(mn, wk_ref[...])
    v_ref[...] = _dot(mn, wv_ref[...])


def _memkv(mem, nw, wk, wv, *, tm):
    m = mem.shape[0]
    row = pl.BlockSpec((tm, D_MODEL), lambda i: (i, 0))
    return pl.pallas_call(
        _memkv_kernel,
        grid=(m // tm,),
        in_specs=[row, _wspec((1, D_MODEL)), _wspec((D_MODEL, D_MODEL)), _wspec((D_MODEL, D_MODEL))],
        out_specs=(row, row),
        out_shape=(jax.ShapeDtypeStruct((m, D_MODEL), F32),) * 2,
        compiler_params=_params(1),
        name="memkv",
    )(mem, nw, wk, wv)


def _xattn_kernel(x_ref, q_ref, k_ref, v_ref, wo_ref, xo_ref, o_ref, *, n_seq, t_rows):
    pad = 16 - t_rows if t_rows < 16 else 0
    for g in range(n_seq):
        rs = slice(g * t_rows, (g + 1) * t_rows)
        for h in range(X_HEADS):
            hs = slice(h * X_HEAD_DIM, (h + 1) * X_HEAD_DIM)
            qh = q_ref[rs, hs]
            if pad:
                qh = jnp.concatenate([qh.astype(F32), jnp.zeros((pad, X_HEAD_DIM), F32)], axis=0)
            s = _dot_nt(qh.astype(BF16), k_ref[g, :, hs].astype(BF16))
            p = jnp.exp(s - jnp.max(s, axis=-1, keepdims=True))
            p = p / jnp.sum(p, axis=-1, keepdims=True)
            o = _dot(p.astype(BF16), v_ref[g, :, hs].astype(BF16))
            o_ref[rs, hs] = o[:t_rows] if pad else o
    xo_ref[...] = x_ref[...] + _dot(o_ref[...].astype(BF16), wo_ref[...])


def _xattn(x, q, k, v, wo, *, layer, n_seq, t_rows, tiles_per_seq):
    m = x.shape[0]
    r = n_seq * t_rows
    row = pl.BlockSpec((r, D_MODEL), lambda i: (i, 0))
    kv = pl.BlockSpec((None, n_seq, MEM_LEN, D_MODEL), lambda i: (layer, i // tiles_per_seq, 0, 0))
    return pl.pallas_call(
        functools.partial(_xattn_kernel, n_seq=n_seq, t_rows=t_rows),
        grid=(m // r,),
        in_specs=[row, row, kv, kv, _wspec((D_MODEL, D_MODEL))],
        out_specs=row,
        out_shape=jax.ShapeDtypeStruct((m, D_MODEL), F32),
        scratch_shapes=[pltpu.VMEM((r, D_MODEL), F32)],
        compiler_params=_params(1),
        name="xattn",
    )(x, q, k, v, wo)


def _ffn_kernel(x_ref, nw_ref, wg_ref, wu_ref, wd_ref, cw_ref, cb_ref, p0_ref, p1_ref, fnw_ref, *rest,
                seq_rows, tiles_per_seq, final_norm):
    if final_norm:
        xo_ref, gtail_ref, y_ref, xn_ref, acc_ref, carry_ref = rest
    else:
        xo_ref, gtail_ref, xn_ref, acc_ref, carry_ref = rest
    i = pl.program_id(0)
    j = pl.program_id(1)
    tm = x_ref.shape[0]

    @pl.when(j == 0)
    def _norm():
        xn_ref[...] = _rms(x_ref[...], nw_ref[...], D_MODEL).astype(BF16)
        acc_ref[...] = jnp.zeros(acc_ref.shape, F32)

    xn = xn_ref[...]
    g = _dot(xn, wg_ref[...])
    u = _dot(xn, wu_ref[...])
    rid = lax.broadcasted_iota(jnp.int32, (tm, 1), 0)
    if seq_rows:
        pos = rid & (seq_rows - 1)
        prev1 = p1_ref[...]
        prev0 = p0_ref[...]
        gtail_ref[...] = g
    else:
        pos = rid

        @pl.when(i % tiles_per_seq == 0)
        def _seq_start():
            carry_ref[j, 0:1, :] = p0_ref[...]
            carry_ref[j, 1:2, :] = p1_ref[...]

        prev0 = carry_ref[j, 0:1, :]
        prev1 = carry_ref[j, 1:2, :]
        gtail_ref[...] = g[tm - 2:tm, :]
    g1 = jnp.where(pos == 0, prev1, pltpu.roll(g, 1, 0))
    g2 = jnp.where(pos == 0, prev0, jnp.where(pos == 1, prev1, pltpu.roll(g, 2, 0)))
    if not seq_rows:
        carry_ref[j, 0:2, :] = g[tm - 2:tm, :]
    y = cb_ref[...] + g2 * cw_ref[0:1, :] + g1 * cw_ref[1:2, :] + g * cw_ref[2:3, :]
    acc_ref[...] += _dot((_gelu(y) * u).astype(BF16), wd_ref[...])

    @pl.when(j == pl.num_programs(1) - 1)
    def _finish():
        xo = x_ref[...] + acc_ref[...]
        xo_ref[...] = xo
        if final_norm:
            y_ref[...] = _rms(xo, fnw_ref[...], D_MODEL)


def _ffn(x, nw, wg, wu, wd, cw, cb, p0, p1, fnw, *, tm, seq_rows, tiles_per_seq, final_norm):
    m = x.shape[0]
    n_tiles = m // tm
    row = pl.BlockSpec((tm, D_MODEL), lambda i, j: (i, 0))
    if seq_rows:
        pspec = pl.BlockSpec((tm, FF_CHUNK), lambda i, j: (i, j))
        gspec = pspec
        gshape = jax.ShapeDtypeStruct((m, D_FF), F32)
    else:
        pspec = pl.BlockSpec((None, 1, FF_CHUNK), lambda i, j: (i // tiles_per_seq, 0, j))
        gspec = pl.BlockSpec((None, 2, FF_CHUNK), lambda i, j: (i // tiles_per_seq, 0, j))
        gshape = jax.ShapeDtypeStruct((n_tiles // tiles_per_seq, 2, D_FF), F32)
    return pl.pallas_call(
        functools.partial(_ffn_kernel, seq_rows=seq_rows, tiles_per_seq=tiles_per_seq, final_norm=final_norm),
        grid=(n_tiles, N_FF_CHUNKS),
        in_specs=[row, pl.BlockSpec((1, D_MODEL), lambda i, j: (0, 0)),
                  pl.BlockSpec((None, D_MODEL, FF_CHUNK), lambda i, j: (j, 0, 0)),
                  pl.BlockSpec((None, D_MODEL, FF_CHUNK), lambda i, j: (j, 0, 0)),
                  pl.BlockSpec((None, FF_CHUNK, D_MODEL), lambda i, j: (j, 0, 0)),
                  pl.BlockSpec((CONV_W, FF_CHUNK), lambda i, j: (0, j)),
                  pl.BlockSpec((1, FF_CHUNK), lambda i, j: (0, j)),
                  pspec, pspec, pl.BlockSpec((1, D_MODEL), lambda i, j: (0, 0))],
        out_specs=(row, gspec) + ((row,) if final_norm else ()),
        out_shape=(jax.ShapeDtypeStruct((m, D_MODEL), F32), gshape)
                  + ((jax.ShapeDtypeStruct((m, D_MODEL), F32),) if final_norm else ()),
        scratch_shapes=[pltpu.VMEM((tm, D_MODEL), BF16), pltpu.VMEM((tm, D_MODEL), F32),
                        pltpu.VMEM((N_FF_CHUNKS, 8, FF_CHUNK), F32)],
        compiler_params=_params(2),
        name="ffn",
    )(x, nw, wg, wu, wd, cw, cb, p0, p1, fnw)


def _pad_heads_idx(start):
    idx = np.full((BW_P,), -1, np.int64)
    for h in range(B_HEADS):
        idx[h * HP:h * HP + B_HEAD_DIM] = start + h * B_HEAD_DIM + np.arange(B_HEAD_DIM)
    return idx


def _take_cols(w, idx):
    wz = jnp.concatenate([w, jnp.zeros(w.shape[:-1] + (1,), w.dtype)], axis=-1)
    return jnp.take(wz, jnp.asarray(np.where(idx < 0, w.shape[-1], idx)), axis=-1)


def _win_layout():
    half = C_ROPE // 2
    base = {'au': 0, 'av': 256, 'bq': 512, 'bk': 896, 'bv': 1280, 'bo': 1664, 'bi': 2048, 'bf': 2052,
            'cq': 2056, 'ckv': 2248, 'ckr': 2376}
    idx = np.full((N_IN,), -1, np.int64)
    idx[S_AU:S_AU + 256] = base['au'] + np.arange(256)
    idx[S_AV:S_AV + 256] = base['av'] + np.arange(256)
    for s, name in ((S_BQ, 'bq'), (S_BK, 'bk'), (S_BV, 'bv'), (S_BO, 'bo')):
        idx[s:s + BW_P] = _pad_heads_idx(base[name])
    idx[S_CQ:S_CQ + Q_LORA] = base['cq'] + np.arange(Q_LORA)
    idx[S_CKV:S_CKV + KV_LORA] = base['ckv'] + np.arange(KV_LORA)
    idx[S_R1:S_R1 + C_ROPE] = base['ckr'] + np.arange(C_ROPE)
    idx[S_R2:S_R2 + half] = base['ckr'] + half + np.arange(half)
    idx[S_R2 + half:S_R2 + C_ROPE] = base['ckr'] + np.arange(half)
    idx[S_G:S_G + B_HEADS] = base['bi'] + np.arange(B_HEADS)
    idx[S_G + B_HEADS:S_G + 2 * B_HEADS] = base['bf'] + np.arange(B_HEADS)
    return idx


def _wuq_layout():
    half = C_ROPE // 2
    per = C_NOPE + C_ROPE
    idx = np.full((N_UQ,), -1, np.int64)
    for h in range(C_HEADS):
        idx[UQ_NOPE + h * C_NOPE:UQ_NOPE + (h + 1) * C_NOPE] = h * per + np.arange(C_NOPE)
        idx[UQ_RA + h * LANE:UQ_RA + h * LANE + C_ROPE] = h * per + C_NOPE + np.arange(C_ROPE)
        idx[UQ_RB + h * LANE:UQ_RB + h * LANE + half] = h * per + C_NOPE + half + np.arange(half)
        idx[UQ_RB + h * LANE + half:UQ_RB + h * LANE + C_ROPE] = h * per + C_NOPE + np.arange(half)
    return idx


def _rope_tables(pos, reps):
    half = C_ROPE // 2
    inv = ROPE_THETA ** (-jnp.arange(half, dtype=F32) / half)
    ang = pos.astype(F32)[:, None] * inv[None, :]
    cos, sin = jnp.cos(ang), jnp.sin(ang)
    z = jnp.zeros((pos.shape[0], LANE - C_ROPE), F32)
    ct = jnp.concatenate([cos, cos, z], axis=1)
    st = jnp.concatenate([-sin, sin, z], axis=1)
    return jnp.tile(ct, (reps, 1)), jnp.tile(st, (reps, 1))


def _pad_lanes(v, idx):
    return _take_cols(v[None, :], idx)


def _layer_weights(l, w):
    hp_idx = _pad_heads_idx(0)
    out = {}
    out['nw_mix'] = w['norm_mix_w'][l][None, :]
    out['w1'] = _take_cols(w['w_in'][l], _win_layout()).astype(BF16)
    out['qnw'] = _pad_lanes(w['mla_qnorm_w'][l], np.concatenate([np.arange(Q_LORA), np.full(CQ_P - Q_LORA, -1)]))
    out['kvnw'] = w['mla_kvnorm_w'][l][None, :]
    wuq = w['mla_w_uq'][l].reshape(Q_LORA, C_HEADS * (C_NOPE + C_ROPE))
    wuq = _take_cols(wuq, _wuq_layout())
    out['wuq'] = jnp.concatenate([wuq, jnp.zeros((CQ_P - Q_LORA, N_UQ), F32)], axis=0).astype(BF16)
    wuk = jnp.transpose(w['mla_w_uk'][l], (1, 2, 0))
    eye = jnp.eye(C_HEADS, dtype=F32)
    out['wuk'] = jnp.einsum('hdc,hg->hdgc', wuk, eye).reshape(C_HEADS * C_NOPE, C_HEADS * KV_LORA).astype(BF16)
    wuv = jnp.transpose(w['mla_w_uv'][l], (1, 0, 2))
    out['wuv'] = jnp.einsum('hce,hg->hcge', wuv, eye).reshape(C_HEADS * KV_LORA, C_WIDTH).astype(BF16)
    gb = jnp.concatenate([w['mlstm_bi'][l], w['mlstm_bf'][l]])
    out['gb'] = _pad_lanes(gb, np.concatenate([np.arange(2 * B_HEADS), np.full(LANE - 2 * B_HEADS, -1)]))
    out['mnw'] = _pad_lanes(w['mlstm_norm_w'][l], hp_idx)
    wo = w['w_out'][l]
    rows = np.concatenate([np.arange(A_WIDTH), np.where(hp_idx < 0, -1, hp_idx + A_WIDTH),
                           A_WIDTH + B_WIDTH + np.arange(C_WIDTH)])
    out['wo'] = jnp.transpose(_take_cols(jnp.transpose(wo), rows)).astype(BF16)
    out['nw_x'] = w['norm_x_w'][l][None, :]
    out['nw_mem'] = w['norm_mem_w'][l][None, :]
    out['xwq'] = w['xa_wq'][l].reshape(D_MODEL, D_MODEL).astype(BF16)
    out['xwk'] = w['xa_wk'][l].reshape(D_MODEL, D_MODEL).astype(BF16)
    out['xwv'] = w['xa_wv'][l].reshape(D_MODEL, D_MODEL).astype(BF16)
    out['xwo'] = w['xa_wo'][l].reshape(D_MODEL, D_MODEL).astype(BF16)
    out['nw_ffn'] = w['norm_ffn_w'][l][None, :]
    out['wg'] = jnp.transpose(w['ffn_wg'][l].reshape(D_MODEL, N_FF_CHUNKS, FF_CHUNK), (1, 0, 2)).astype(BF16)
    out['wu'] = jnp.transpose(w['ffn_wu'][l].reshape(D_MODEL, N_FF_CHUNKS, FF_CHUNK), (1, 0, 2)).astype(BF16)
    out['wd'] = w['ffn_wd'][l].reshape(N_FF_CHUNKS, FF_CHUNK, D_MODEL).astype(BF16)
    out['cw'] = w['ffn_conv_w'][l]
    out['cb'] = w['ffn_conv_b'][l][None, :]
    return out


def _chunk_mixers(ws, cb, t_rows):
    reps = CHUNK // t_rows
    wt = ws[:, :t_rows, :t_rows]
    if reps > 1:
        wt = jnp.einsum('ab,hts->hatbs', jnp.eye(reps, dtype=F32), wt).reshape(A_HEADS, CHUNK, CHUNK)
    bias = jnp.tile(jnp.repeat(jnp.transpose(cb[:, :t_rows]), A_HEAD_DIM, axis=1), (reps, 1))
    return wt, bias


def _mixer_block(x, lw, *, mlstm_state, pos_tables, chunk_w, tm, act_dtype, ns, n_tiles, n_chunks, mla_fn):
    m = x.shape[0]
    cos_t, sin_t = pos_tables
    gu, gv, q, k, v, so, qc, lat, gate = _inproj(x, lw['nw_mix'], lw['w1'], lw['qnw'], lw['kvnw'], lw['wuq'],
                                                 lw['wuk'], lw['gb'], cos_t, sin_t, tm=tm, act_dtype=act_dtype)
    ya = _chunkmlp(gu, gv, chunk_w[0], chunk_w[1], tm=tm)
    grow = jnp.transpose(gate[:, :16].reshape(m // CHUNK, CHUNK, 16), (0, 2, 1))
    c0, n0, m0 = mlstm_state
    yb, c_new, n_new, m_rows = _mlstm(q, k, v, so, gate, grow, m0, c0, n0, lw['mnw'],
                                      ns=ns, n_tiles=n_tiles, n_chunks=n_chunks)
    yc = mla_fn(qc, lat)
    return (ya, yb, yc), (c_new, n_new, m_rows), lat, gv


def kernel(x_prompt, x_sample, mem_prompt, cache_mla, page_table, cache_mem_k, cache_mem_v, state_mlstm_C, state_mlstm_n, state_mlstm_m, state_ffn_conv, norm_mix_w, w_in, chunk_ws, chunk_b, mlstm_bi, mlstm_bf, mlstm_norm_w, mla_qnorm_w, mla_kvnorm_w, mla_w_uq, mla_w_uk, mla_w_uv, w_out, norm_x_w, norm_mem_w, xa_wq, xa_wk, xa_wv, xa_wo, norm_ffn_w, ffn_wg, ffn_wu, ffn_conv_w, ffn_conv_b, ffn_wd, norm_final_w):
    w = dict(norm_mix_w=norm_mix_w, w_in=w_in, mlstm_bi=mlstm_bi, mlstm_bf=mlstm_bf, mlstm_norm_w=mlstm_norm_w,
             mla_qnorm_w=mla_qnorm_w, mla_kvnorm_w=mla_kvnorm_w, mla_w_uq=mla_w_uq, mla_w_uk=mla_w_uk,
             mla_w_uv=mla_w_uv, w_out=w_out, norm_x_w=norm_x_w, norm_mem_w=norm_mem_w, xa_wq=xa_wq, xa_wk=xa_wk,
             xa_wv=xa_wv, xa_wo=xa_wo, norm_ffn_w=norm_ffn_w, ffn_wg=ffn_wg, ffn_wu=ffn_wu,
             ffn_conv_w=ffn_conv_w, ffn_conv_b=ffn_conv_b, ffn_wd=ffn_wd)
    depth = w_in.shape[0]
    nbp, S, _ = x_prompt.shape
    nbs, T, _ = x_sample.shape
    n_pages = page_table.shape[1]
    past_len = n_pages * PAGE_SIZE
    mp, ms = nbp * S, nbs * T
    assert S % CHUNK == 0 and CHUNK % T == 0 and T % 8 == 0 and ms % CHUNK == 0

    tm_p = min(512, S)
    tm_s = min(512, ms)
    tq = min(128, S)
    tk = min(256, S)
    ns_s = CHUNK // T
    pages_per_step = min(16, n_pages)
    xg = 4 if nbs % 4 == 0 else 1
    tm_ffn_p = min(1024, S)
    tm_ffn_s = min(256, ms)

    pos_p = _rope_tables(jnp.arange(S), 1)
    pos_s = _rope_tables(past_len + jnp.arange(T), tm_s // T)
    fnw = norm_final_w[None, :]

    xp = x_prompt.reshape(mp, D_MODEL)
    xs = x_sample.reshape(ms, D_MODEL)
    mem = mem_prompt.reshape(nbp * MEM_LEN, D_MODEL)

    zero_c = jnp.zeros((nbp, B_HEADS, B_HEAD_DIM, B_HEAD_DIM), F32)
    zero_n = jnp.zeros((nbp, B_HEADS, B_HEAD_DIM), F32)
    zero_m = jnp.zeros((nbp, CHUNK, LANE), F32)
    zero_conv = jnp.zeros((nbp, 1, D_FF), F32)
    cache_k = cache_mem_k.reshape(depth, nbs, MEM_LEN, D_MODEL)
    cache_v = cache_mem_v.reshape(depth, nbs, MEM_LEN, D_MODEL)

    outs = {k: [] for k in ('lat_p', 'lat_s', 'mk', 'mv', 'cp', 'np', 'mp', 'cs', 'ns', 'ms', 'conv_p', 'conv_s',
                            'chunkv')}
    yp = ys = None
    for l in range(depth):
        lw = _layer_weights(l, w)
        last = l == depth - 1

        mla_p = functools.partial(_mla_prompt, wuv=lw['wuv'], nb=nbp, seq=S, tq=tq, tk=tk)
        ymix, st, lat, _ = _mixer_block(
            xp, lw, mlstm_state=(zero_c, zero_n, zero_m), pos_tables=pos_p,
            chunk_w=_chunk_mixers(chunk_ws[l], chunk_b[l], CHUNK), tm=tm_p, act_dtype=BF16, ns=1, n_tiles=nbp,
            n_chunks=S // CHUNK, mla_fn=lambda qc, lt: mla_p(qc, lt))
        xp, qx_p = _outproj(xp, *ymix, lw['wo'], lw['nw_x'], lw['xwq'], tm=tm_p, q_dtype=BF16)
        outs['lat_p'].append(lat.reshape(nbp, S, LATENT))
        outs['cp'].append(st[0])
        outs['np'].append(st[1])
        outs['mp'].append(st[2][:, CHUNK - 1, :B_HEADS])

        m0_rows = jnp.concatenate([jnp.repeat(state_mlstm_m[l], T, axis=0),
                                   jnp.zeros((ms, LANE - B_HEADS), F32)], axis=1).reshape(ms // CHUNK, CHUNK, LANE)
        mla_s = functools.partial(_mla_sample, page_table, wuv=lw['wuv'], cache=cache_mla, layer=l, t_new=T,
                                  pages_per_step=pages_per_step)
        ymix, st, lat, gv = _mixer_block(
            xs, lw, mlstm_state=(state_mlstm_C[l], state_mlstm_n[l], m0_rows), pos_tables=pos_s,
            chunk_w=_chunk_mixers(chunk_ws[l], chunk_b[l], T), tm=tm_s, act_dtype=F32, ns=ns_s,
            n_tiles=ms // CHUNK, n_chunks=1, mla_fn=lambda qc, lt: mla_s(qc, lt))
        xs, qx_s = _outproj(xs, *ymix, lw['wo'], lw['nw_x'], lw['xwq'], tm=tm_s, q_dtype=F32)
        outs['lat_s'].append(lat.reshape(nbs, T, LATENT))
        outs['cs'].append(st[0])
        outs['ns'].append(st[1])
        outs['ms'].append(st[2].reshape(nbs, T, LANE)[:, T - 1, :B_HEADS])
        outs['chunkv'].append(gv.reshape(nbs, T, A_WIDTH))

        mk, mv = _memkv(mem, lw['nw_mem'], lw['xwk'], lw['xwv'], tm=min(512, nbp * MEM_LEN))
        outs['mk'].append(mk.reshape(nbp, MEM_LEN, X_HEADS, X_HEAD_DIM))
        outs['mv'].append(mv.reshape(nbp, MEM_LEN, X_HEADS, X_HEAD_DIM))
        xp = _xattn(xp, qx_p, mk.reshape(1, nbp, MEM_LEN, D_MODEL), mv.reshape(1, nbp, MEM_LEN, D_MODEL),
                    lw['xwo'], layer=0, n_seq=1, t_rows=tm_p, tiles_per_seq=S // tm_p)
        xs = _xattn(xs, qx_s, cache_k, cache_v, lw['xwo'], layer=l, n_seq=xg, t_rows=T, tiles_per_seq=1)

        res_p = _ffn(xp, lw['nw_ffn'], lw['wg'], lw['wu'], lw['wd'], lw['cw'], lw['cb'], zero_conv, zero_conv, fnw,
                     tm=tm_ffn_p, seq_rows=0, tiles_per_seq=S // tm_ffn_p, final_norm=last)
        buf = state_ffn_conv[l]
        p0 = jnp.repeat(buf[:, 0, :], T, axis=0)
        p1 = jnp.repeat(buf[:, 1, :], T, axis=0)
        res_s = _ffn(xs, lw['nw_ffn'], lw['wg'], lw['wu'], lw['wd'], lw['cw'], lw['cb'], p0, p1, fnw,
                     tm=tm_ffn_s, seq_rows=T, tiles_per_seq=1, final_norm=last)
        xp, xs = res_p[0], res_s[0]
        outs['conv_p'].append(res_p[1])
        outs['conv_s'].append(res_s[1].reshape(nbs, T, D_FF)[:, T - (CONV_W - 1):, :])
        if last:
            yp, ys = res_p[2], res_s[2]

    st = lambda k: jnp.stack(outs[k])
    return (yp.reshape(nbp, S, D_MODEL), ys.reshape(nbs, T, D_MODEL),
            st('lat_p'), st('lat_s'), st('mk'), st('mv'),
            st('cp'), st('np'), st('mp'), st('cs'), st('ns'), st('ms'),
            st('conv_p'), st('conv_s'), st('chunkv'))
```

```python
import functools
import math

import numpy as np
import jax
import jax.numpy as jnp
from jax import lax
from jax.experimental import pallas as pl
from jax.experimental.pallas import tpu as pltpu

F32 = jnp.float32
BF16 = jnp.bfloat16

D_MODEL = 1024
PAGE_SIZE = 128
A_HEADS = 4
A_WIDTH = 256
A_HEAD_DIM = 64
CHUNK = 128
B_HEADS = 4
B_WIDTH = 384
B_HEAD_DIM = 96
MLSTM_KSCALE = B_HEAD_DIM ** -0.5
C_WIDTH = 384
C_HEADS = 6
C_V_DIM = 64
C_NOPE = 64
C_ROPE = 32
Q_LORA = 192
KV_LORA = 128
LATENT = KV_LORA + C_ROPE
MLA_SCALE = (C_NOPE + C_ROPE) ** -0.5
ROPE_THETA = 10000.0
MEM_LEN = 256
X_HEADS = 4
X_HEAD_DIM = 256
XA_SCALE = X_HEAD_DIM ** -0.5
D_FF = 2816
CONV_W = 3
EPS = 1e-6

LANE = 128
HP = 128
BW_P = B_HEADS * HP
FF_CHUNK = 256
N_FF_CHUNKS = D_FF // FF_CHUNK
VMEM_LIMIT = 48 * 1024 * 1024

S_AU, S_AV, S_BQ, S_BK, S_BV, S_BO = 0, 256, 512, 1024, 1536, 2048
S_CQ, S_CKV, S_R1, S_R2, S_G = 2560, 2816, 2944, 3072, 3200
N_IN = 3328
CQ_P = 256
UQ_NOPE, UQ_RA, UQ_RB, N_UQ = 0, 384, 1152, 1920
QC_SLOT = 256
N_MIX_P = A_WIDTH + BW_P + C_WIDTH


def _rms(x, w, n):
    ms = jnp.sum(x * x, axis=-1, keepdims=True) * (1.0 / n)
    return x * lax.rsqrt(ms + EPS) * w


def _gelu(x):
    c = math.sqrt(2.0 / math.pi)
    return x * (0.5 * (1.0 + jnp.tanh(c * (x + 0.044715 * (x * x * x)))))


def _sigmoid(x):
    return 1.0 / (1.0 + jnp.exp(-x))


def _log_sigmoid(x):
    return jnp.minimum(x, 0.0) - jnp.log1p(jnp.exp(-jnp.abs(x)))


def _dot(a, b):
    return jnp.dot(a, b, preferred_element_type=F32)


def _dot_nt(a, b):
    return lax.dot_general(a, b, (((1,), (1,)), ((), ())), preferred_element_type=F32)


def _dot_tn(a, b):
    return lax.dot_general(a, b, (((0,), (0,)), ((), ())), preferred_element_type=F32)


def _split3(a):
    a1 = a.astype(BF16)
    r1 = a - a1.astype(F32)
    a2 = r1.astype(BF16)
    a3 = (r1 - a2.astype(F32)).astype(BF16)
    return a1, a2, a3


def _wspec(shape):
    nd = len(shape)
    return pl.BlockSpec(shape, lambda *_: (0,) * nd, pipeline_mode=pl.Buffered(1))


def _params(n_axes):
    return pltpu.CompilerParams(dimension_semantics=("arbitrary",) * n_axes,
                                vmem_limit_bytes=VMEM_LIMIT)


def _inproj_kernel(x_ref, nw_ref, w1_ref, qnw_ref, kvnw_ref, wuq_ref, wuk_ref, gb_ref, cos_ref, sin_ref,
                   gu_ref, gv_ref, q_ref, k_ref, v_ref, so_ref, qc_ref, lat_ref, gate_ref):
    xn = _rms(x_ref[...], nw_ref[...], D_MODEL).astype(BF16)

    def seg(a, b):
        return _dot(xn, w1_ref[:, a:b])

    gu_ref[...] = _gelu(seg(S_AU, S_AV))
    gv_ref[...] = _gelu(seg(S_AV, S_BQ))
    q_ref[...] = seg(S_BQ, S_BK).astype(q_ref.dtype)
    k_ref[...] = (seg(S_BK, S_BV) * MLSTM_KSCALE).astype(k_ref.dtype)
    v_ref[...] = seg(S_BV, S_BO).astype(v_ref.dtype)
    so_ref[...] = _sigmoid(seg(S_BO, S_CQ))

    c_q = _rms(seg(S_CQ, S_CKV), qnw_ref[...], Q_LORA).astype(BF16)
    q2 = _dot(c_q, wuq_ref[...])
    qlat = _dot(q2[:, UQ_NOPE:UQ_RA].astype(BF16), wuk_ref[...])
    cos = cos_ref[...]
    sin = sin_ref[...]
    for h in range(C_HEADS):
        ra = q2[:, UQ_RA + LANE * h:UQ_RA + LANE * (h + 1)]
        rb = q2[:, UQ_RB + LANE * h:UQ_RB + LANE * (h + 1)]
        qr = ra * cos + rb * sin
        qc_ref[:, QC_SLOT * h:QC_SLOT * h + LANE] = (qlat[:, LANE * h:LANE * (h + 1)] * MLA_SCALE).astype(qc_ref.dtype)
        qc_ref[:, QC_SLOT * h + LANE:QC_SLOT * (h + 1)] = (qr * MLA_SCALE).astype(qc_ref.dtype)

    lat_ref[:, 0:KV_LORA] = _rms(seg(S_CKV, S_R1), kvnw_ref[...], KV_LORA)
    kr = seg(S_R1, S_R2) * cos + seg(S_R2, S_G) * sin
    lat_ref[:, KV_LORA:LATENT] = kr[:, :C_ROPE]

    gt = seg(S_G, N_IN) + gb_ref[...]
    lane = lax.broadcasted_iota(jnp.int32, gt.shape, 1)
    gate_ref[...] = jnp.where(lane < B_HEADS, gt, jnp.where(lane < 2 * B_HEADS, _log_sigmoid(gt), 0.0))


def _inproj(x, nw, w1, qnw, kvnw, wuq, wuk, gb, cos_t, sin_t, *, tm, act_dtype):
    m = x.shape[0]
    n_pos_blocks = cos_t.shape[0] // tm
    row = lambda w: pl.BlockSpec((tm, w), lambda i: (i, 0))
    pos = pl.BlockSpec((tm, LANE), lambda i: (i % n_pos_blocks, 0))
    out_shapes = (
        jax.ShapeDtypeStruct((m, A_WIDTH), F32), jax.ShapeDtypeStruct((m, A_WIDTH), F32),
        jax.ShapeDtypeStruct((m, BW_P), act_dtype), jax.ShapeDtypeStruct((m, BW_P), act_dtype),
        jax.ShapeDtypeStruct((m, BW_P), act_dtype), jax.ShapeDtypeStruct((m, BW_P), F32),
        jax.ShapeDtypeStruct((m, C_HEADS * QC_SLOT), act_dtype), jax.ShapeDtypeStruct((m, LATENT), F32),
        jax.ShapeDtypeStruct((m, LANE), F32))
    return pl.pallas_call(
        _inproj_kernel,
        grid=(m // tm,),
        in_specs=[row(D_MODEL), _wspec((1, D_MODEL)), _wspec((D_MODEL, N_IN)), _wspec((1, CQ_P)),
                  _wspec((1, KV_LORA)), _wspec((CQ_P, N_UQ)), _wspec((C_HEADS * C_NOPE, C_HEADS * LANE)),
                  _wspec((1, LANE)), pos, pos],
        out_specs=(row(A_WIDTH), row(A_WIDTH), row(BW_P), row(BW_P), row(BW_P), row(BW_P),
                   row(C_HEADS * QC_SLOT), row(LATENT), row(LANE)),
        out_shape=out_shapes,
        compiler_params=_params(1),
        name="inproj",
    )(x, nw, w1, qnw, kvnw, wuq, wuk, gb, cos_t, sin_t)


def _chunkmlp_kernel(gu_ref, gv_ref, ws_ref, bias_ref, ya_ref, *, n_chunks):
    r = lax.broadcasted_iota(jnp.int32, (CHUNK, CHUNK), 0)
    c = lax.broadcasted_iota(jnp.int32, (CHUNK, CHUNK), 1)
    wsm = [jnp.where(r >= c, ws_ref[h], 0.0).astype(BF16) for h in range(A_HEADS)]
    head_of_lane = lax.broadcasted_iota(jnp.int32, (CHUNK, A_WIDTH), 1) >> (A_HEAD_DIM.bit_length() - 1)
    for ci in range(n_chunks):
        rows = slice(ci * CHUNK, (ci + 1) * CHUNK)
        gv = gv_ref[rows, :]
        z = bias_ref[...]
        for h in range(A_HEADS):
            z = z + _dot(wsm[h], jnp.where(head_of_lane == h, gv, 0.0).astype(BF16))
        ya_ref[rows, :] = gu_ref[rows, :] * z


def _chunkmlp(gu, gv, ws, bias, *, tm):
    m = gu.shape[0]
    row = pl.BlockSpec((tm, A_WIDTH), lambda i: (i, 0))
    return pl.pallas_call(
        functools.partial(_chunkmlp_kernel, n_chunks=tm // CHUNK),
        grid=(m // tm,),
        in_specs=[row, row, _wspec((A_HEADS, CHUNK, CHUNK)), _wspec((CHUNK, A_WIDTH))],
        out_specs=row,
        out_shape=jax.ShapeDtypeStruct((m, A_WIDTH), F32),
        compiler_params=_params(1),
        name="chunkmlp",
    )(gu, gv, ws, bias)


_COL_M, _COL_G, _COL_RS = 0, 8, 16


def _mlstm_kernel(q_ref, k_ref, v_ref, so_ref, gcol_ref, grow_ref, m0_ref, c0_ref, n0_ref, nw_ref,
                  yb_ref, cout_ref, nout_ref, mout_ref,
                  cs_ref, ns_ref, mp_ref, col_ref, sv_ref, qc_ref, nr_ref, *, ns, ls):
    L = CHUNK
    ci = pl.program_id(1)

    ls_shift = ls.bit_length() - 1

    @pl.when(ci == 0)
    def _init():
        cs_ref[...] = jnp.zeros(cs_ref.shape, F32)
        ns_ref[...] = jnp.zeros(ns_ref.shape, F32)
        for g in range(ns):
            for h in range(B_HEADS):
                cs_ref[g * B_HEADS + h, :B_HEAD_DIM, :B_HEAD_DIM] = c0_ref[g, h]
                ns_ref[g * B_HEADS + h, 0:1, :B_HEAD_DIM] = n0_ref[g, h:h + 1, :]
        mp_ref[...] = m0_ref[...]

    col_ref[...] = jnp.zeros(col_ref.shape, F32)
    row = lax.broadcasted_iota(jnp.int32, (L, L), 0)
    col = lax.broadcasted_iota(jnp.int32, (L, L), 1)
    causal = (col <= row) & ((col >> ls_shift) == (row >> ls_shift))
    cum = jnp.where(causal, 1.0, 0.0).astype(BF16)
    rid = lax.broadcasted_iota(jnp.int32, (L, 1), 0)

    gcol = gcol_ref[...]
    grow = grow_ref[...]
    bcol = sum(_dot(cum, t) for t in _split3(gcol))
    brow = sum(_dot_nt(t, cum) for t in _split3(grow))
    mp = mp_ref[...]

    qs, ks, vs = [], [], []
    for h in range(B_HEADS):
        hs = slice(h * HP, (h + 1) * HP)
        qh = q_ref[:, hs].astype(BF16)
        kh = k_ref[:, hs].astype(BF16)
        vh = v_ref[:, hs].astype(BF16)
        qs.append(qh), ks.append(kh), vs.append(vh)
        ig_r = grow[h:h + 1, :]
        b_r = brow[B_HEADS + h:B_HEADS + h + 1, :]
        b_c = bcol[:, B_HEADS + h:B_HEADS + h + 1]
        bm = b_c + mp[:, h:h + 1]
        logd = jnp.where(causal, ig_r + b_c - b_r, -jnp.inf)
        m_t = jnp.maximum(bm, jnp.max(logd, axis=-1, keepdims=True))
        d = jnp.exp(logd - m_t)
        s = _dot_nt(qh, kh) * d
        sv_ref[h] = _dot(s.astype(BF16), vh)
        col_ref[:, _COL_M + h:_COL_M + h + 1] = m_t
        col_ref[:, _COL_G + h:_COL_G + h + 1] = jnp.exp(bm - m_t)
        col_ref[:, _COL_RS + h:_COL_RS + h + 1] = jnp.sum(s, axis=-1, keepdims=True)

    qc_ref[...] = jnp.zeros(qc_ref.shape, F32)
    nr_ref[...] = jnp.zeros(nr_ref.shape, F32)

    def seq_update(g):
        last = g * ls + (ls - 1)
        in_seq = (rid >> ls_shift) == g
        is_last = rid == last
        for h in range(B_HEADS):
            c_old = cs_ref[g * B_HEADS + h]
            n_old = ns_ref[g * B_HEADS + h, 0:1, :]
            qm = jnp.where(in_seq, qs[h], jnp.zeros_like(qs[h])) if ns > 1 else qs[h]
            qc_ref[h] += _dot_nt(qm, c_old.astype(BF16))
            nr_ref[h] += jnp.where(in_seq, n_old, 0.0)
            m_t = col_ref[:, _COL_M + h:_COL_M + h + 1]
            gdec = col_ref[:, _COL_G + h:_COL_G + h + 1]
            b_c = bcol[:, B_HEADS + h:B_HEADS + h + 1]
            ig_c = gcol[:, h:h + 1]
            pick = lambda a: jnp.sum(jnp.where(is_last, a, 0.0), axis=0, keepdims=True)
            m_new, b_last, decay = pick(m_t), pick(b_c), pick(gdec)
            w_c = jnp.where(in_seq, jnp.exp(ig_c + b_last - b_c - m_new), 0.0)
            vw = (vs[h].astype(F32) * w_c).astype(BF16)
            cs_ref[g * B_HEADS + h] = decay * c_old + _dot_tn(vw, ks[h])
            ns_ref[g * B_HEADS + h, 0:1, :] = (decay * n_old
                                               + jnp.sum(ks[h].astype(F32) * w_c, axis=0, keepdims=True))
            if ns == 1:
                mp_ref[:, h:h + 1] = jnp.broadcast_to(m_new, (L, 1))

    if ns == 1:
        seq_update(0)
    else:
        lax.fori_loop(0, ns, lambda g, c: (seq_update(g), c)[1], 0)

    for h in range(B_HEADS):
        hs = slice(h * HP, (h + 1) * HP)
        m_t = col_ref[:, _COL_M + h:_COL_M + h + 1]
        gdec = col_ref[:, _COL_G + h:_COL_G + h + 1]
        rs = col_ref[:, _COL_RS + h:_COL_RS + h + 1]
        num = gdec * qc_ref[h] + sv_ref[h]
        qn = jnp.sum(qs[h].astype(F32) * nr_ref[h], axis=-1, keepdims=True)
        den = gdec * qn + rs
        hh = num / jnp.maximum(jnp.abs(den), jnp.exp(-m_t))
        yb_ref[:, hs] = so_ref[:, hs] * _rms(hh, nw_ref[:, hs], B_HEAD_DIM)

    mout_ref[...] = col_ref[...]
    for g in range(ns):
        for h in range(B_HEADS):
            cout_ref[g, h] = cs_ref[g * B_HEADS + h, :B_HEAD_DIM, :B_HEAD_DIM]
            nout_ref[g, h:h + 1, :] = ns_ref[g * B_HEADS + h, 0:1, :B_HEAD_DIM]


def _mlstm(q, k, v, so, gcol, grow, m0, c0, n0, nw, *, ns, n_tiles, n_chunks):
    m = q.shape[0]
    nb = c0.shape[0]
    ls = CHUNK // ns
    rows = lambda w: pl.BlockSpec((CHUNK, w), lambda t, c: (t * n_chunks + c, 0))
    nspec = pl.BlockSpec((ns, B_HEADS, B_HEAD_DIM), lambda t, c: (t, 0, 0))
    return pl.pallas_call(
        functools.partial(_mlstm_kernel, ns=ns, ls=ls),
        grid=(n_tiles, n_chunks),
        in_specs=[rows(BW_P), rows(BW_P), rows(BW_P), rows(BW_P), rows(LANE),
                  pl.BlockSpec((None, 16, CHUNK), lambda t, c: (t * n_chunks + c, 0, 0)),
                  pl.BlockSpec((None, CHUNK, LANE), lambda t, c: (t, 0, 0)),
                  pl.BlockSpec((ns, B_HEADS, B_HEAD_DIM, B_HEAD_DIM), lambda t, c: (t, 0, 0, 0)),
                  nspec,
                  _wspec((1, BW_P))],
        out_specs=(rows(BW_P),
                   pl.BlockSpec((ns, B_HEADS, B_HEAD_DIM, B_HEAD_DIM), lambda t, c: (t, 0, 0, 0)),
                   nspec,
                   pl.BlockSpec((None, CHUNK, LANE), lambda t, c: (t, 0, 0))),
        out_shape=(jax.ShapeDtypeStruct((m, BW_P), F32),
                   jax.ShapeDtypeStruct((nb, B_HEADS, B_HEAD_DIM, B_HEAD_DIM), F32),
                   jax.ShapeDtypeStruct((nb, B_HEADS, B_HEAD_DIM), F32),
                   jax.ShapeDtypeStruct((n_tiles, CHUNK, LANE), F32)),
        scratch_shapes=[pltpu.VMEM((ns * B_HEADS, HP, HP), F32), pltpu.VMEM((ns * B_HEADS, 8, HP), F32),
                        pltpu.VMEM((CHUNK, LANE), F32), pltpu.VMEM((CHUNK, LANE), F32),
                        pltpu.VMEM((B_HEADS, CHUNK, HP), F32), pltpu.VMEM((B_HEADS, CHUNK, HP), F32),
                        pltpu.VMEM((B_HEADS, CHUNK, HP), F32)],
        compiler_params=_params(2),
        name="mlstm",
    )(q, k, v, so, gcol, grow, m0, c0, n0, nw)


def _stack_heads(qc_ref, dtype):
    return jnp.concatenate([qc_ref[:, QC_SLOT * h:QC_SLOT * h + LATENT] for h in range(C_HEADS)],
                           axis=0).astype(dtype)


def _unstack_project(o, t, wuv_ref):
    wide = jnp.concatenate([o[h * t:(h + 1) * t] for h in range(C_HEADS)], axis=1)
    return _dot(wide.astype(BF16), wuv_ref[...])


def _softmax_step(s, pv, m_ref, acc_ref):
    reps = s.shape[1] // LANE
    m_prev = m_ref[...]
    m_new = jnp.maximum(m_prev, jnp.max(s, axis=-1, keepdims=True))
    p = jnp.exp(s - jnp.concatenate([m_new] * reps, axis=1))
    corr = jnp.exp(m_prev - m_new)
    acc_ref[...] = jnp.concatenate([corr, corr], axis=1) * acc_ref[...] + pv(p.astype(BF16))
    m_ref[...] = m_new


def _softmax_result(acc_ref):
    acc = acc_ref[...]
    return acc[:, :KV_LORA] / acc[:, KV_LORA:]


def _mla_prompt_kernel(qc_ref, lat_ref, wuv_ref, yc_ref, kb_ref, vb_ref, m_ref, acc_ref, *, tq, tk):
    qi = pl.program_id(1)

    @pl.when(qi == 0)
    def _cast_keys():
        lat = lat_ref[...]
        kb_ref[...] = lat.astype(BF16)
        vb_ref[:, 0:KV_LORA] = lat[:, 0:KV_LORA].astype(BF16)
        vb_ref[:, KV_LORA:2 * KV_LORA] = jnp.ones((lat.shape[0], KV_LORA), BF16)

    qst = _stack_heads(qc_ref, BF16)
    m_ref[...] = jnp.full(m_ref.shape, -jnp.inf, F32)
    acc_ref[...] = jnp.zeros(acc_ref.shape, F32)
    rows = C_HEADS * tq

    def block(j, masked):
        start = pl.multiple_of(j * tk, tk)
        s = _dot_nt(qst, kb_ref[pl.ds(start, tk), :])
        if masked:
            qpos = qi * tq + (lax.broadcasted_iota(jnp.int32, (rows, tk), 0) & (tq - 1))
            kpos = start + lax.broadcasted_iota(jnp.int32, (rows, tk), 1)
            s = jnp.where(kpos <= qpos, s, -jnp.inf)
        _softmax_step(s, lambda p: _dot(p, vb_ref[pl.ds(start, tk), :]), m_ref, acc_ref)

    n_full = (qi * tq) // tk
    lax.fori_loop(0, n_full, lambda j, c: (block(j, False), c)[1], 0)
    block(n_full, True)
    yc_ref[...] = _unstack_project(_softmax_result(acc_ref), tq, wuv_ref)


def _mla_prompt(qc, lat, wuv, *, nb, seq, tq, tk):
    m = qc.shape[0]
    n_q = seq // tq
    rows = C_HEADS * tq
    return pl.pallas_call(
        functools.partial(_mla_prompt_kernel, tq=tq, tk=tk),
        grid=(nb, n_q),
        in_specs=[pl.BlockSpec((tq, C_HEADS * QC_SLOT), lambda b, i: (b * n_q + i, 0)),
                  pl.BlockSpec((seq, LATENT), lambda b, i: (b, 0)),
                  _wspec((C_HEADS * KV_LORA, C_WIDTH))],
        out_specs=pl.BlockSpec((tq, C_WIDTH), lambda b, i: (b * n_q + i, 0)),
        out_shape=jax.ShapeDtypeStruct((m, C_WIDTH), F32),
        scratch_shapes=[pltpu.VMEM((seq, LATENT), BF16), pltpu.VMEM((seq, 2 * KV_LORA), BF16),
                        pltpu.VMEM((rows, LANE), F32), pltpu.VMEM((rows, 2 * KV_LORA), F32)],
        compiler_params=_params(2),
        name="mla_prompt",
    )(qc, lat, wuv)


def _mla_sample_kernel(pt_ref, qc_ref, latn_ref, wuv_ref, *rest, pages_per_step, t_new):
    page_refs = rest[:pages_per_step]
    yc_ref, kt_ref, kn_ref, m_ref, acc_ref = rest[pages_per_step:]
    j = pl.program_id(1)
    qst = _stack_heads(qc_ref, BF16)
    rows = C_HEADS * t_new

    @pl.when(j == 0)
    def _init():
        m_ref[...] = jnp.full(m_ref.shape, -jnp.inf, F32)
        acc_ref[...] = jnp.zeros(acc_ref.shape, F32)

    for i in range(pages_per_step):
        kt_ref[:, i * PAGE_SIZE:(i + 1) * PAGE_SIZE] = page_refs[i][...].astype(BF16)

    def pv_past(p):
        o = _dot_nt(p, kt_ref[0:KV_LORA, :])
        return jnp.concatenate([o, jnp.broadcast_to(jnp.sum(p.astype(F32), axis=-1, keepdims=True),
                                                    (rows, KV_LORA))], axis=1)

    _softmax_step(_dot(qst, kt_ref[...]), pv_past, m_ref, acc_ref)

    @pl.when(j == pl.num_programs(1) - 1)
    def _new_rows_and_finish():
        kn_ref[...] = jnp.zeros(kn_ref.shape, F32)
        kn_ref[0:t_new, 0:LATENT] = latn_ref[...]
        kn_ref[:, KV_LORA + LANE:] = jnp.ones((PAGE_SIZE, LANE), F32)
        kn = kn_ref[...].astype(BF16)
        tpos = lax.broadcasted_iota(jnp.int32, (rows, PAGE_SIZE), 0) & (t_new - 1)
        kidx = lax.broadcasted_iota(jnp.int32, (rows, PAGE_SIZE), 1)
        s = jnp.where(kidx <= tpos, _dot_nt(qst, kn[:, 0:LATENT]), -jnp.inf)
        vn = jnp.concatenate([kn[:, 0:KV_LORA], kn[:, KV_LORA + LANE:]], axis=1)
        _softmax_step(s, lambda p: _dot(p, vn), m_ref, acc_ref)
        yc_ref[...] = _unstack_project(_softmax_result(acc_ref), t_new, wuv_ref)


def _mla_sample(page_table, qc, lat_new, wuv, cache_t, *, layer, t_new, pages_per_step):
    m = qc.shape[0]
    nb, n_pages = page_table.shape
    n_steps = n_pages // pages_per_step
    rows = C_HEADS * t_new

    def page_spec(i):
        return pl.BlockSpec((None, None, LATENT, PAGE_SIZE),
                            lambda b, j, pt: (layer, pt[b, j * pages_per_step + i], 0, 0))

    grid_spec = pltpu.PrefetchScalarGridSpec(
        num_scalar_prefetch=1,
        grid=(nb, n_steps),
        in_specs=[pl.BlockSpec((t_new, C_HEADS * QC_SLOT), lambda b, j, pt: (b, 0)),
                  pl.BlockSpec((t_new, LATENT), lambda b, j, pt: (b, 0)),
                  pl.BlockSpec((C_HEADS * KV_LORA, C_WIDTH), lambda b, j, pt: (0, 0))]
                 + [page_spec(i) for i in range(pages_per_step)],
        out_specs=pl.BlockSpec((t_new, C_WIDTH), lambda b, j, pt: (b, 0)),
        scratch_shapes=[pltpu.VMEM((LATENT, pages_per_step * PAGE_SIZE), BF16),
                        pltpu.VMEM((PAGE_SIZE, KV_LORA + 2 * LANE), F32),
                        pltpu.VMEM((rows, LANE), F32), pltpu.VMEM((rows, 2 * KV_LORA), F32)])
    return pl.pallas_call(
        functools.partial(_mla_sample_kernel, pages_per_step=pages_per_step, t_new=t_new),
        grid_spec=grid_spec,
        out_shape=jax.ShapeDtypeStruct((m, C_WIDTH), F32),
        compiler_params=_params(2),
        name="mla_sample",
    )(page_table, qc, lat_new, wuv, *([cache_t] * pages_per_step))


def _outproj_kernel(x_ref, ya_ref, yb_ref, yc_ref, wo_ref, nxw_ref, wq_ref, xo_ref, q_ref):
    y = (_dot(ya_ref[...].astype(BF16), wo_ref[0:A_WIDTH, :])
         + _dot(yb_ref[...].astype(BF16), wo_ref[A_WIDTH:A_WIDTH + BW_P, :])
         + _dot(yc_ref[...].astype(BF16), wo_ref[A_WIDTH + BW_P:N_MIX_P, :]))
    xn = x_ref[...] + y
    xo_ref[...] = xn
    hq = _rms(xn, nxw_ref[...], D_MODEL).astype(BF16)
    q_ref[...] = (_dot(hq, wq_ref[...]) * XA_SCALE).astype(q_ref.dtype)


def _outproj(x, ya, yb, yc, wo, nxw, wq, *, tm, q_dtype):
    m = x.shape[0]
    row = lambda w: pl.BlockSpec((tm, w), lambda i: (i, 0))
    return pl.pallas_call(
        _outproj_kernel,
        grid=(m // tm,),
        in_specs=[row(D_MODEL), row(A_WIDTH), row(BW_P), row(C_WIDTH), _wspec((N_MIX_P, D_MODEL)),
                  _wspec((1, D_MODEL)), _wspec((D_MODEL, D_MODEL))],
        out_specs=(row(D_MODEL), row(D_MODEL)),
        out_shape=(jax.ShapeDtypeStruct((m, D_MODEL), F32), jax.ShapeDtypeStruct((m, D_MODEL), q_dtype)),
        compiler_params=_params(1),
        name="outproj",
    )(x, ya, yb, yc, wo, nxw, wq)


def _memkv_kernel(mem_ref, nw_ref, wk_ref, wv_ref, k_ref, v_ref):
    mn = _rms(mem_ref[...], nw_ref[...], D_MODEL).astype(BF16)
    k = _dot(mn, wk_ref[...])
    v = _dot(mn, wv_ref[...])
    for h in range(X_HEADS):
        hs = slice(h * X_HEAD_DIM, (h + 1) * X_HEAD_DIM)
        k_ref[:, h, :] = k[:, hs]
        v_ref[:, h, :] = v[:, hs]


def _memkv(mem, nw, wk, wv, *, tm):
    m = mem.shape[0]
    row = pl.BlockSpec((tm, D_MODEL), lambda i: (i, 0))
    out = pl.BlockSpec((tm, X_HEADS, X_HEAD_DIM), lambda i: (i, 0, 0))
    return pl.pallas_call(
        _memkv_kernel,
        grid=(m // tm,),
        in_specs=[row, _wspec((1, D_MODEL)), _wspec((D_MODEL, D_MODEL)), _wspec((D_MODEL, D_MODEL))],
        out_specs=(out, out),
        out_shape=(jax.ShapeDtypeStruct((m, X_HEADS, X_HEAD_DIM), F32),) * 2,
        compiler_params=_params(1),
        name="memkv",
    )(mem, nw, wk, wv)


def _xattn_kernel(x_ref, q_ref, k_ref, v_ref, wo_ref, xo_ref, o_ref, *, n_seq, t_rows):
    pad = 16 - t_rows if t_rows < 16 else 0
    for g in range(n_seq):
        rs = slice(g * t_rows, (g + 1) * t_rows)
        for h in range(X_HEADS):
            hs = slice(h * X_HEAD_DIM, (h + 1) * X_HEAD_DIM)
            qh = q_ref[rs, hs]
            if pad:
                qh = jnp.concatenate([qh.astype(F32), jnp.zeros((pad, X_HEAD_DIM), F32)], axis=0)
            s = _dot_nt(qh.astype(BF16), k_ref[g, :, h, :].astype(BF16))
            p = jnp.exp(s - jnp.max(s, axis=-1, keepdims=True))
            p = p / jnp.sum(p, axis=-1, keepdims=True)
            o = _dot(p.astype(BF16), v_ref[g, :, h, :].astype(BF16))
            o_ref[rs, hs] = o[:t_rows] if pad else o
    xo_ref[...] = x_ref[...] + _dot(o_ref[...].astype(BF16), wo_ref[...])


def _xattn(x, q, k, v, wo, *, layer, n_seq, t_rows, tiles_per_seq):
    m = x.shape[0]
    r = n_seq * t_rows
    row = pl.BlockSpec((r, D_MODEL), lambda i: (i, 0))
    kv = pl.BlockSpec((None, n_seq, MEM_LEN, X_HEADS, X_HEAD_DIM),
                      lambda i: (layer, i // tiles_per_seq, 0, 0, 0))
    return pl.pallas_call(
        functools.partial(_xattn_kernel, n_seq=n_seq, t_rows=t_rows),
        grid=(m // r,),
        in_specs=[row, row, kv, kv, _wspec((D_MODEL, D_MODEL))],
        out_specs=row,
        out_shape=jax.ShapeDtypeStruct((m, D_MODEL), F32),
        scratch_shapes=[pltpu.VMEM((r, D_MODEL), F32)],
        compiler_params=_params(1),
        name="xattn",
    )(x, q, k, v, wo)


def _ffn_kernel(x_ref, nw_ref, wg_ref, wu_ref, wd_ref, cw_ref, cb_ref, p0_ref, p1_ref, fnw_ref, *rest,
                seq_rows, tiles_per_seq, final_norm):
    if final_norm:
        xo_ref, gtail_ref, y_ref, xn_ref, acc_ref, carry_ref = rest
    else:
        xo_ref, gtail_ref, xn_ref, acc_ref, carry_ref = rest
    i = pl.program_id(0)
    j = pl.program_id(1)
    tm = x_ref.shape[0]

    @pl.when(j == 0)
    def _norm():
        xn_ref[...] = _rms(x_ref[...], nw_ref[...], D_MODEL).astype(BF16)
        acc_ref[...] = jnp.zeros(acc_ref.shape, F32)

    xn = xn_ref[...]
    g = _dot(xn, wg_ref[...])
    u = _dot(xn, wu_ref[...])
    rid = lax.broadcasted_iota(jnp.int32, (tm, 1), 0)
    if seq_rows:
        pos = rid & (seq_rows - 1)
        prev1 = p1_ref[...]
        prev0 = p0_ref[...]
        gtail_ref[...] = g
    else:
        pos = rid

        @pl.when(i % tiles_per_seq == 0)
        def _seq_start():
            carry_ref[j, 0:1, :] = p0_ref[...]
            carry_ref[j, 1:2, :] = p1_ref[...]

        prev0 = carry_ref[j, 0:1, :]
        prev1 = carry_ref[j, 1:2, :]
        gtail_ref[...] = g[tm - 2:tm, :]
    g1 = jnp.where(pos == 0, prev1, pltpu.roll(g, 1, 0))
    g2 = jnp.where(pos == 0, prev0, jnp.where(pos == 1, prev1, pltpu.roll(g, 2, 0)))
    if not seq_rows:
        carry_ref[j, 0:2, :] = g[tm - 2:tm, :]
    y = cb_ref[...] + g2 * cw_ref[0:1, :] + g1 * cw_ref[1:2, :] + g * cw_ref[2:3, :]
    acc_ref[...] += _dot((_gelu(y) * u).astype(BF16), wd_ref[...])

    @pl.when(j == pl.num_programs(1) - 1)
    def _finish():
        xo = x_ref[...] + acc_ref[...]
        xo_ref[...] = xo
        if final_norm:
            y_ref[...] = _rms(xo, fnw_ref[...], D_MODEL)


def _ffn(x, nw, wg, wu, wd, cw, cb, p0, p1, fnw, *, tm, seq_rows, tiles_per_seq, final_norm):
    m = x.shape[0]
    n_tiles = m // tm
    row = pl.BlockSpec((tm, D_MODEL), lambda i, j: (i, 0))
    if seq_rows:
        pspec = pl.BlockSpec((tm, FF_CHUNK), lambda i, j: (i, j))
        gspec = pspec
        gshape = jax.ShapeDtypeStruct((m, D_FF), F32)
    else:
        pspec = pl.BlockSpec((None, 1, FF_CHUNK), lambda i, j: (i // tiles_per_seq, 0, j))
        gspec = pl.BlockSpec((None, 2, FF_CHUNK), lambda i, j: (i, 0, j))
        gshape = jax.ShapeDtypeStruct((n_tiles, 2, D_FF), F32)
    return pl.pallas_call(
        functools.partial(_ffn_kernel, seq_rows=seq_rows, tiles_per_seq=tiles_per_seq, final_norm=final_norm),
        grid=(n_tiles, N_FF_CHUNKS),
        in_specs=[row, pl.BlockSpec((1, D_MODEL), lambda i, j: (0, 0)),
                  pl.BlockSpec((None, D_MODEL, FF_CHUNK), lambda i, j: (j, 0, 0)),
                  pl.BlockSpec((None, D_MODEL, FF_CHUNK), lambda i, j: (j, 0, 0)),
                  pl.BlockSpec((None, FF_CHUNK, D_MODEL), lambda i, j: (j, 0, 0)),
                  pl.BlockSpec((CONV_W, FF_CHUNK), lambda i, j: (0, j)),
                  pl.BlockSpec((1, FF_CHUNK), lambda i, j: (0, j)),
                  pspec, pspec, pl.BlockSpec((1, D_MODEL), lambda i, j: (0, 0))],
        out_specs=(row, gspec) + ((row,) if final_norm else ()),
        out_shape=(jax.ShapeDtypeStruct((m, D_MODEL), F32), gshape)
                  + ((jax.ShapeDtypeStruct((m, D_MODEL), F32),) if final_norm else ()),
        scratch_shapes=[pltpu.VMEM((tm, D_MODEL), BF16), pltpu.VMEM((tm, D_MODEL), F32),
                        pltpu.VMEM((N_FF_CHUNKS, 8, FF_CHUNK), F32)],
        compiler_params=_params(2),
        name="ffn",
    )(x, nw, wg, wu, wd, cw, cb, p0, p1, fnw)


def _pad_heads_idx(start):
    idx = np.full((BW_P,), -1, np.int64)
    for h in range(B_HEADS):
        idx[h * HP:h * HP + B_HEAD_DIM] = start + h * B_HEAD_DIM + np.arange(B_HEAD_DIM)
    return idx


def _take_cols(w, idx):
    wz = jnp.concatenate([w, jnp.zeros(w.shape[:-1] + (1,), w.dtype)], axis=-1)
    return jnp.take(wz, jnp.asarray(np.where(idx < 0, w.shape[-1], idx)), axis=-1)


def _win_layout():
    half = C_ROPE // 2
    base = {'au': 0, 'av': 256, 'bq': 512, 'bk': 896, 'bv': 1280, 'bo': 1664, 'bi': 2048, 'bf': 2052,
            'cq': 2056, 'ckv': 2248, 'ckr': 2376}
    idx = np.full((N_IN,), -1, np.int64)
    idx[S_AU:S_AU + 256] = base['au'] + np.arange(256)
    idx[S_AV:S_AV + 256] = base['av'] + np.arange(256)
    for s, name in ((S_BQ, 'bq'), (S_BK, 'bk'), (S_BV, 'bv'), (S_BO, 'bo')):
        idx[s:s + BW_P] = _pad_heads_idx(base[name])
    idx[S_CQ:S_CQ + Q_LORA] = base['cq'] + np.arange(Q_LORA)
    idx[S_CKV:S_CKV + KV_LORA] = base['ckv'] + np.arange(KV_LORA)
    idx[S_R1:S_R1 + C_ROPE] = base['ckr'] + np.arange(C_ROPE)
    idx[S_R2:S_R2 + half] = base['ckr'] + half + np.arange(half)
    idx[S_R2 + half:S_R2 + C_ROPE] = base['ckr'] + np.arange(half)
    idx[S_G:S_G + B_HEADS] = base['bi'] + np.arange(B_HEADS)
    idx[S_G + B_HEADS:S_G + 2 * B_HEADS] = base['bf'] + np.arange(B_HEADS)
    return idx


def _wuq_layout():
    half = C_ROPE // 2
    per = C_NOPE + C_ROPE
    idx = np.full((N_UQ,), -1, np.int64)
    for h in range(C_HEADS):
        idx[UQ_NOPE + h * C_NOPE:UQ_NOPE + (h + 1) * C_NOPE] = h * per + np.arange(C_NOPE)
        idx[UQ_RA + h * LANE:UQ_RA + h * LANE + C_ROPE] = h * per + C_NOPE + np.arange(C_ROPE)
        idx[UQ_RB + h * LANE:UQ_RB + h * LANE + half] = h * per + C_NOPE + half + np.arange(half)
        idx[UQ_RB + h * LANE + half:UQ_RB + h * LANE + C_ROPE] = h * per + C_NOPE + np.arange(half)
    return idx


def _rope_tables(pos, reps):
    half = C_ROPE // 2
    inv = ROPE_THETA ** (-jnp.arange(half, dtype=F32) / half)
    ang = pos.astype(F32)[:, None] * inv[None, :]
    cos, sin = jnp.cos(ang), jnp.sin(ang)
    z = jnp.zeros((pos.shape[0], LANE - C_ROPE), F32)
    ct = jnp.concatenate([cos, cos, z], axis=1)
    st = jnp.concatenate([-sin, sin, z], axis=1)
    return jnp.tile(ct, (reps, 1)), jnp.tile(st, (reps, 1))


def _pad_lanes(v, idx):
    return _take_cols(v[None, :], idx)


def _layer_weights(l, w):
    hp_idx = _pad_heads_idx(0)
    out = {}
    out['nw_mix'] = w['norm_mix_w'][l][None, :]
    out['w1'] = _take_cols(w['w_in'][l], _win_layout()).astype(BF16)
    out['qnw'] = _pad_lanes(w['mla_qnorm_w'][l], np.concatenate([np.arange(Q_LORA), np.full(CQ_P - Q_LORA, -1)]))
    out['kvnw'] = w['mla_kvnorm_w'][l][None, :]
    wuq = w['mla_w_uq'][l].reshape(Q_LORA, C_HEADS * (C_NOPE + C_ROPE))
    wuq = _take_cols(wuq, _wuq_layout())
    out['wuq'] = jnp.concatenate([wuq, jnp.zeros((CQ_P - Q_LORA, N_UQ), F32)], axis=0).astype(BF16)
    wuk = jnp.transpose(w['mla_w_uk'][l], (1, 2, 0))
    eye = jnp.eye(C_HEADS, dtype=F32)
    out['wuk'] = jnp.einsum('hdc,hg->hdgc', wuk, eye).reshape(C_HEADS * C_NOPE, C_HEADS * KV_LORA).astype(BF16)
    wuv = jnp.transpose(w['mla_w_uv'][l], (1, 0, 2))
    out['wuv'] = jnp.einsum('hce,hg->hcge', wuv, eye).reshape(C_HEADS * KV_LORA, C_WIDTH).astype(BF16)
    gb = jnp.concatenate([w['mlstm_bi'][l], w['mlstm_bf'][l]])
    out['gb'] = _pad_lanes(gb, np.concatenate([np.arange(2 * B_HEADS), np.full(LANE - 2 * B_HEADS, -1)]))
    out['mnw'] = _pad_lanes(w['mlstm_norm_w'][l], hp_idx)
    wo = w['w_out'][l]
    rows = np.concatenate([np.arange(A_WIDTH), np.where(hp_idx < 0, -1, hp_idx + A_WIDTH),
                           A_WIDTH + B_WIDTH + np.arange(C_WIDTH)])
    out['wo'] = jnp.transpose(_take_cols(jnp.transpose(wo), rows)).astype(BF16)
    out['nw_x'] = w['norm_x_w'][l][None, :]
    out['nw_mem'] = w['norm_mem_w'][l][None, :]
    out['xwq'] = w['xa_wq'][l].reshape(D_MODEL, D_MODEL).astype(BF16)
    out['xwk'] = w['xa_wk'][l].reshape(D_MODEL, D_MODEL).astype(BF16)
    out['xwv'] = w['xa_wv'][l].reshape(D_MODEL, D_MODEL).astype(BF16)
    out['xwo'] = w['xa_wo'][l].reshape(D_MODEL, D_MODEL).astype(BF16)
    out['nw_ffn'] = w['norm_ffn_w'][l][None, :]
    out['wg'] = jnp.transpose(w['ffn_wg'][l].reshape(D_MODEL, N_FF_CHUNKS, FF_CHUNK), (1, 0, 2)).astype(BF16)
    out['wu'] = jnp.transpose(w['ffn_wu'][l].reshape(D_MODEL, N_FF_CHUNKS, FF_CHUNK), (1, 0, 2)).astype(BF16)
    out['wd'] = w['ffn_wd'][l].reshape(N_FF_CHUNKS, FF_CHUNK, D_MODEL).astype(BF16)
    out['cw'] = w['ffn_conv_w'][l]
    out['cb'] = w['ffn_conv_b'][l][None, :]
    return out


def _chunk_mixers(ws, cb, t_rows):
    reps = CHUNK // t_rows
    wt = ws[:, :t_rows, :t_rows]
    if reps > 1:
        wt = jnp.einsum('ab,hts->hatbs', jnp.eye(reps, dtype=F32), wt).reshape(A_HEADS, CHUNK, CHUNK)
    bias = jnp.tile(jnp.repeat(jnp.transpose(cb[:, :t_rows]), A_HEAD_DIM, axis=1), (reps, 1))
    return wt, bias


def _mixer_block(x, lw, *, mlstm_state, pos_tables, chunk_w, tm, act_dtype, ns, n_tiles, n_chunks, mla_fn):
    m = x.shape[0]
    cos_t, sin_t = pos_tables
    gu, gv, q, k, v, so, qc, lat, gate = _inproj(x, lw['nw_mix'], lw['w1'], lw['qnw'], lw['kvnw'], lw['wuq'],
                                                 lw['wuk'], lw['gb'], cos_t, sin_t, tm=tm, act_dtype=act_dtype)
    ya = _chunkmlp(gu, gv, chunk_w[0], chunk_w[1], tm=tm)
    grow = jnp.transpose(gate[:, :16].reshape(m // CHUNK, CHUNK, 16), (0, 2, 1))
    c0, n0, m0 = mlstm_state
    yb, c_new, n_new, m_rows = _mlstm(q, k, v, so, gate, grow, m0, c0, n0, lw['mnw'],
                                      ns=ns, n_tiles=n_tiles, n_chunks=n_chunks)
    yc = mla_fn(qc, lat)
    return (ya, yb, yc), (c_new, n_new, m_rows), lat, gv


def kernel(x_prompt, x_sample, mem_prompt, cache_mla, page_table, cache_mem_k, cache_mem_v, state_mlstm_C, state_mlstm_n, state_mlstm_m, state_ffn_conv, norm_mix_w, w_in, chunk_ws, chunk_b, mlstm_bi, mlstm_bf, mlstm_norm_w, mla_qnorm_w, mla_kvnorm_w, mla_w_uq, mla_w_uk, mla_w_uv, w_out, norm_x_w, norm_mem_w, xa_wq, xa_wk, xa_wv, xa_wo, norm_ffn_w, ffn_wg, ffn_wu, ffn_conv_w, ffn_conv_b, ffn_wd, norm_final_w):
    w = dict(norm_mix_w=norm_mix_w, w_in=w_in, mlstm_bi=mlstm_bi, mlstm_bf=mlstm_bf, mlstm_norm_w=mlstm_norm_w,
             mla_qnorm_w=mla_qnorm_w, mla_kvnorm_w=mla_kvnorm_w, mla_w_uq=mla_w_uq, mla_w_uk=mla_w_uk,
             mla_w_uv=mla_w_uv, w_out=w_out, norm_x_w=norm_x_w, norm_mem_w=norm_mem_w, xa_wq=xa_wq, xa_wk=xa_wk,
             xa_wv=xa_wv, xa_wo=xa_wo, norm_ffn_w=norm_ffn_w, ffn_wg=ffn_wg, ffn_wu=ffn_wu,
             ffn_conv_w=ffn_conv_w, ffn_conv_b=ffn_conv_b, ffn_wd=ffn_wd)
    depth = w_in.shape[0]
    nbp, S, _ = x_prompt.shape
    nbs, T, _ = x_sample.shape
    n_pages = page_table.shape[1]
    past_len = n_pages * PAGE_SIZE
    mp, ms = nbp * S, nbs * T
    assert S % CHUNK == 0 and CHUNK % T == 0 and T % 8 == 0 and ms % CHUNK == 0

    tm_p = min(512, S)
    tm_s = min(512, ms)
    tq = min(256, S)
    tk = min(512, S)
    ns_s = CHUNK // T
    pages_per_step = min(32, n_pages)
    xg = 4 if nbs % 4 == 0 else 1
    tm_ffn_p = min(1024, S)
    tm_ffn_s = min(256, ms)

    pos_p = _rope_tables(jnp.arange(S), 1)
    pos_s = _rope_tables(past_len + jnp.arange(T), tm_s // T)
    fnw = norm_final_w[None, :]

    xp = x_prompt.reshape(mp, D_MODEL)
    xs = x_sample.reshape(ms, D_MODEL)
    mem = mem_prompt.reshape(nbp * MEM_LEN, D_MODEL)

    zero_c = jnp.zeros((nbp, B_HEADS, B_HEAD_DIM, B_HEAD_DIM), F32)
    zero_n = jnp.zeros((nbp, B_HEADS, B_HEAD_DIM), F32)
    zero_m = jnp.zeros((nbp, CHUNK, LANE), F32)
    zero_conv = jnp.zeros((nbp, 1, D_FF), F32)
    cache_t = jnp.swapaxes(cache_mla, 2, 3)

    outs = {k: [] for k in ('lat_p', 'lat_s', 'mk', 'mv', 'cp', 'np', 'mp', 'cs', 'ns', 'ms', 'conv_p', 'conv_s',
                            'chunkv')}
    yp = ys = None
    for l in range(depth):
        lw = _layer_weights(l, w)
        last = l == depth - 1

        mla_p = functools.partial(_mla_prompt, wuv=lw['wuv'], nb=nbp, seq=S, tq=tq, tk=tk)
        ymix, st, lat, _ = _mixer_block(
            xp, lw, mlstm_state=(zero_c, zero_n, zero_m), pos_tables=pos_p,
            chunk_w=_chunk_mixers(chunk_ws[l], chunk_b[l], CHUNK), tm=tm_p, act_dtype=BF16, ns=1, n_tiles=nbp,
            n_chunks=S // CHUNK, mla_fn=lambda qc, lt: mla_p(qc, lt))
        xp, qx_p = _outproj(xp, *ymix, lw['wo'], lw['nw_x'], lw['xwq'], tm=tm_p, q_dtype=BF16)
        outs['lat_p'].append(lat.reshape(nbp, S, LATENT))
        outs['cp'].append(st[0])
        outs['np'].append(st[1])
        outs['mp'].append(st[2][:, CHUNK - 1, :B_HEADS])

        m0_rows = jnp.concatenate([jnp.repeat(state_mlstm_m[l], T, axis=0),
                                   jnp.zeros((ms, LANE - B_HEADS), F32)], axis=1).reshape(ms // CHUNK, CHUNK, LANE)
        mla_s = functools.partial(_mla_sample, page_table, wuv=lw['wuv'], cache_t=cache_t, layer=l, t_new=T,
                                  pages_per_step=pages_per_step)
        ymix, st, lat, gv = _mixer_block(
            xs, lw, mlstm_state=(state_mlstm_C[l], state_mlstm_n[l], m0_rows), pos_tables=pos_s,
            chunk_w=_chunk_mixers(chunk_ws[l], chunk_b[l], T), tm=tm_s, act_dtype=F32, ns=ns_s,
            n_tiles=ms // CHUNK, n_chunks=1, mla_fn=lambda qc, lt: mla_s(qc, lt))
        xs, qx_s = _outproj(xs, *ymix, lw['wo'], lw['nw_x'], lw['xwq'], tm=tm_s, q_dtype=F32)
        outs['lat_s'].append(lat.reshape(nbs, T, LATENT))
        outs['cs'].append(st[0])
        outs['ns'].append(st[1])
        outs['ms'].append(st[2].reshape(nbs, T, LANE)[:, T - 1, :B_HEADS])
        outs['chunkv'].append(gv.reshape(nbs, T, A_WIDTH))

        mk, mv = _memkv(mem, lw['nw_mem'], lw['xwk'], lw['xwv'], tm=min(512, nbp * MEM_LEN))
        mk = mk.reshape(1, nbp, MEM_LEN, X_HEADS, X_HEAD_DIM)
        mv = mv.reshape(1, nbp, MEM_LEN, X_HEADS, X_HEAD_DIM)
        outs['mk'].append(mk[0])
        outs['mv'].append(mv[0])
        xp = _xattn(xp, qx_p, mk, mv, lw['xwo'], layer=0, n_seq=1, t_rows=tm_p, tiles_per_seq=S // tm_p)
        xs = _xattn(xs, qx_s, cache_mem_k, cache_mem_v, lw['xwo'], layer=l, n_seq=xg, t_rows=T, tiles_per_seq=1)

        res_p = _ffn(xp, lw['nw_ffn'], lw['wg'], lw['wu'], lw['wd'], lw['cw'], lw['cb'], zero_conv, zero_conv, fnw,
                     tm=tm_ffn_p, seq_rows=0, tiles_per_seq=S // tm_ffn_p, final_norm=last)
        buf = state_ffn_conv[l]
        p0 = jnp.repeat(buf[:, 0, :], T, axis=0)
        p1 = jnp.repeat(buf[:, 1, :], T, axis=0)
        res_s = _ffn(xs, lw['nw_ffn'], lw['wg'], lw['wu'], lw['wd'], lw['cw'], lw['cb'], p0, p1, fnw,
                     tm=tm_ffn_s, seq_rows=T, tiles_per_seq=1, final_norm=last)
        xp, xs = res_p[0], res_s[0]
        outs['conv_p'].append(res_p[1][S // tm_ffn_p - 1::S // tm_ffn_p])
        outs['conv_s'].append(res_s[1].reshape(nbs, T, D_FF)[:, T - (CONV_W - 1):, :])
        if last:
            yp, ys = res_p[2], res_s[2]

    st = lambda k: jnp.stack(outs[k])
    return (yp.reshape(nbp, S, D_MODEL), ys.reshape(nbs, T, D_MODEL),
            st('lat_p'), st('lat_s'), st('mk'), st('mv'),
            st('cp'), st('np'), st('mp'), st('cs'), st('ns'), st('ms'),
            st('conv_p'), st('conv_s'), st('chunkv'))
```

```python
import functools
import math

import numpy as np
import jax
import jax.numpy as jnp
from jax import lax
from jax.experimental import pallas as pl
from jax.experimental.pallas import tpu as pltpu

F32 = jnp.float32
BF16 = jnp.bfloat16

D_MODEL = 1024
PAGE_SIZE = 128
A_HEADS = 4
A_WIDTH = 256
A_HEAD_DIM = 64
CHUNK = 128
B_HEADS = 4
B_WIDTH = 384
B_HEAD_DIM = 96
MLSTM_KSCALE = B_HEAD_DIM ** -0.5
C_WIDTH = 384
C_HEADS = 6
C_V_DIM = 64
C_NOPE = 64
C_ROPE = 32
Q_LORA = 192
KV_LORA = 128
LATENT = KV_LORA + C_ROPE
MLA_SCALE = (C_NOPE + C_ROPE) ** -0.5
ROPE_THETA = 10000.0
MEM_LEN = 256
X_HEADS = 4
X_HEAD_DIM = 256
XA_SCALE = X_HEAD_DIM ** -0.5
D_FF = 2816
CONV_W = 3
EPS = 1e-6

LANE = 128
HP = 128
BW_P = B_HEADS * HP
FF_CHUNK = 256
N_FF_CHUNKS = D_FF // FF_CHUNK
VMEM_LIMIT = 48 * 1024 * 1024

S_AU, S_AV, S_BQ, S_BK, S_BV, S_BO = 0, 256, 512, 1024, 1536, 2048
S_CQ, S_CKV, S_R1, S_R2, S_G = 2560, 2816, 2944, 3072, 3200
N_IN = 3328
CQ_P = 256
UQ_NOPE, UQ_RA, UQ_RB, N_UQ = 0, 384, 1152, 1920
QC_SLOT = 256
N_MIX_P = A_WIDTH + BW_P + C_WIDTH


def _rms(x, w, n):
    ms = jnp.sum(x * x, axis=-1, keepdims=True) * (1.0 / n)
    return x * lax.rsqrt(ms + EPS) * w


def _gelu(x):
    c = math.sqrt(2.0 / math.pi)
    return x * (0.5 * (1.0 + jnp.tanh(c * (x + 0.044715 * (x * x * x)))))


def _sigmoid(x):
    return 1.0 / (1.0 + jnp.exp(-x))


def _log_sigmoid(x):
    return jnp.minimum(x, 0.0) - jnp.log1p(jnp.exp(-jnp.abs(x)))


def _dot(a, b):
    return jnp.dot(a, b, preferred_element_type=F32)


def _dot_nt(a, b):
    return lax.dot_general(a, b, (((1,), (1,)), ((), ())), preferred_element_type=F32)


def _dot_tn(a, b):
    return lax.dot_general(a, b, (((0,), (0,)), ((), ())), preferred_element_type=F32)


def _split3(a):
    a1 = a.astype(BF16)
    r1 = a - a1.astype(F32)
    a2 = r1.astype(BF16)
    a3 = (r1 - a2.astype(F32)).astype(BF16)
    return a1, a2, a3


def _wspec(shape):
    nd = len(shape)
    return pl.BlockSpec(shape, lambda *_: (0,) * nd, pipeline_mode=pl.Buffered(1))


def _params(n_axes):
    return pltpu.CompilerParams(dimension_semantics=("arbitrary",) * n_axes,
                                vmem_limit_bytes=VMEM_LIMIT)


def _inproj_kernel(x_ref, nw_ref, w1_ref, qnw_ref, kvnw_ref, wuq_ref, wuk_ref, gb_ref, cos_ref, sin_ref,
                   gu_ref, gv_ref, q_ref, k_ref, v_ref, so_ref, qc_ref, lat_ref, gate_ref):
    xn = _rms(x_ref[...], nw_ref[...], D_MODEL).astype(BF16)

    def seg(a, b):
        return _dot(xn, w1_ref[:, a:b])

    gu_ref[...] = _gelu(seg(S_AU, S_AV))
    gv_ref[...] = _gelu(seg(S_AV, S_BQ))
    q_ref[...] = seg(S_BQ, S_BK).astype(q_ref.dtype)
    k_ref[...] = (seg(S_BK, S_BV) * MLSTM_KSCALE).astype(k_ref.dtype)
    v_ref[...] = seg(S_BV, S_BO).astype(v_ref.dtype)
    so_ref[...] = _sigmoid(seg(S_BO, S_CQ))

    c_q = _rms(seg(S_CQ, S_CKV), qnw_ref[...], Q_LORA).astype(BF16)
    q2 = _dot(c_q, wuq_ref[...])
    qlat = _dot(q2[:, UQ_NOPE:UQ_RA].astype(BF16), wuk_ref[...])
    cos = cos_ref[...]
    sin = sin_ref[...]
    for h in range(C_HEADS):
        ra = q2[:, UQ_RA + LANE * h:UQ_RA + LANE * (h + 1)]
        rb = q2[:, UQ_RB + LANE * h:UQ_RB + LANE * (h + 1)]
        qr = ra * cos + rb * sin
        qc_ref[:, QC_SLOT * h:QC_SLOT * h + LANE] = (qlat[:, LANE * h:LANE * (h + 1)] * MLA_SCALE).astype(qc_ref.dtype)
        qc_ref[:, QC_SLOT * h + LANE:QC_SLOT * (h + 1)] = (qr * MLA_SCALE).astype(qc_ref.dtype)

    lat_ref[:, 0:KV_LORA] = _rms(seg(S_CKV, S_R1), kvnw_ref[...], KV_LORA)
    kr = seg(S_R1, S_R2) * cos + seg(S_R2, S_G) * sin
    lat_ref[:, KV_LORA:LATENT] = kr[:, :C_ROPE]

    gt = seg(S_G, N_IN) + gb_ref[...]
    lane = lax.broadcasted_iota(jnp.int32, gt.shape, 1)
    gate_ref[...] = jnp.where(lane < B_HEADS, gt, jnp.where(lane < 2 * B_HEADS, _log_sigmoid(gt), 0.0))


def _inproj(x, nw, w1, qnw, kvnw, wuq, wuk, gb, cos_t, sin_t, *, tm, act_dtype):
    m = x.shape[0]
    n_pos_blocks = cos_t.shape[0] // tm
    row = lambda w: pl.BlockSpec((tm, w), lambda i: (i, 0))
    pos = pl.BlockSpec((tm, LANE), lambda i: (i % n_pos_blocks, 0))
    out_shapes = (
        jax.ShapeDtypeStruct((m, A_WIDTH), F32), jax.ShapeDtypeStruct((m, A_WIDTH), F32),
        jax.ShapeDtypeStruct((m, BW_P), act_dtype), jax.ShapeDtypeStruct((m, BW_P), act_dtype),
        jax.ShapeDtypeStruct((m, BW_P), act_dtype), jax.ShapeDtypeStruct((m, BW_P), F32),
        jax.ShapeDtypeStruct((m, C_HEADS * QC_SLOT), act_dtype), jax.ShapeDtypeStruct((m, LATENT), F32),
        jax.ShapeDtypeStruct((m, LANE), F32))
    return pl.pallas_call(
        _inproj_kernel,
        grid=(m // tm,),
        in_specs=[row(D_MODEL), _wspec((1, D_MODEL)), _wspec((D_MODEL, N_IN)), _wspec((1, CQ_P)),
                  _wspec((1, KV_LORA)), _wspec((CQ_P, N_UQ)), _wspec((C_HEADS * C_NOPE, C_HEADS * LANE)),
                  _wspec((1, LANE)), pos, pos],
        out_specs=(row(A_WIDTH), row(A_WIDTH), row(BW_P), row(BW_P), row(BW_P), row(BW_P),
                   row(C_HEADS * QC_SLOT), row(LATENT), row(LANE)),
        out_shape=out_shapes,
        compiler_params=_params(1),
        name="inproj",
    )(x, nw, w1, qnw, kvnw, wuq, wuk, gb, cos_t, sin_t)


def _chunkmlp_kernel(gu_ref, gv_ref, ws_ref, bias_ref, ya_ref, *, n_chunks):
    r = lax.broadcasted_iota(jnp.int32, (CHUNK, CHUNK), 0)
    c = lax.broadcasted_iota(jnp.int32, (CHUNK, CHUNK), 1)
    wsm = [jnp.where(r >= c, ws_ref[h], 0.0).astype(BF16) for h in range(A_HEADS)]
    head_of_lane = lax.broadcasted_iota(jnp.int32, (CHUNK, A_WIDTH), 1) >> (A_HEAD_DIM.bit_length() - 1)
    for ci in range(n_chunks):
        rows = slice(ci * CHUNK, (ci + 1) * CHUNK)
        gv = gv_ref[rows, :]
        z = bias_ref[...]
        for h in range(A_HEADS):
            z = z + _dot(wsm[h], jnp.where(head_of_lane == h, gv, 0.0).astype(BF16))
        ya_ref[rows, :] = gu_ref[rows, :] * z


def _chunkmlp(gu, gv, ws, bias, *, tm):
    m = gu.shape[0]
    row = pl.BlockSpec((tm, A_WIDTH), lambda i: (i, 0))
    return pl.pallas_call(
        functools.partial(_chunkmlp_kernel, n_chunks=tm // CHUNK),
        grid=(m // tm,),
        in_specs=[row, row, _wspec((A_HEADS, CHUNK, CHUNK)), _wspec((CHUNK, A_WIDTH))],
        out_specs=row,
        out_shape=jax.ShapeDtypeStruct((m, A_WIDTH), F32),
        compiler_params=_params(1),
        name="chunkmlp",
    )(gu, gv, ws, bias)


_COL_M, _COL_G, _COL_RS = 0, 8, 16


def _mlstm_kernel(q_ref, k_ref, v_ref, so_ref, gcol_ref, grow_ref, m0_ref, c0_ref, n0_ref, nw_ref,
                  yb_ref, cout_ref, nout_ref, mout_ref,
                  cs_ref, ns_ref, mp_ref, col_ref, sv_ref, qc_ref, nr_ref, *, ns, ls):
    L = CHUNK
    ci = pl.program_id(1)

    ls_shift = ls.bit_length() - 1

    @pl.when(ci == 0)
    def _init():
        cs_ref[...] = jnp.zeros(cs_ref.shape, F32)
        ns_ref[...] = jnp.zeros(ns_ref.shape, F32)
        for g in range(ns):
            for h in range(B_HEADS):
                cs_ref[g * B_HEADS + h, :B_HEAD_DIM, :B_HEAD_DIM] = c0_ref[g, h]
                ns_ref[g * B_HEADS + h, 0:1, :B_HEAD_DIM] = n0_ref[g, h:h + 1, :]
        mp_ref[...] = m0_ref[...]

    col_ref[...] = jnp.zeros(col_ref.shape, F32)
    row = lax.broadcasted_iota(jnp.int32, (L, L), 0)
    col = lax.broadcasted_iota(jnp.int32, (L, L), 1)
    causal = (col <= row) & ((col >> ls_shift) == (row >> ls_shift))
    cum = jnp.where(causal, 1.0, 0.0).astype(BF16)
    rid = lax.broadcasted_iota(jnp.int32, (L, 1), 0)

    gcol = gcol_ref[...]
    grow = grow_ref[...]
    bcol = sum(_dot(cum, t) for t in _split3(gcol))
    brow = sum(_dot_nt(t, cum) for t in _split3(grow))
    mp = mp_ref[...]

    qs, ks, vs = [], [], []
    for h in range(B_HEADS):
        hs = slice(h * HP, (h + 1) * HP)
        qh = q_ref[:, hs].astype(BF16)
        kh = k_ref[:, hs].astype(BF16)
        vh = v_ref[:, hs].astype(BF16)
        qs.append(qh), ks.append(kh), vs.append(vh)
        ig_r = grow[h:h + 1, :]
        b_r = brow[B_HEADS + h:B_HEADS + h + 1, :]
        b_c = bcol[:, B_HEADS + h:B_HEADS + h + 1]
        bm = b_c + mp[:, h:h + 1]
        logd = jnp.where(causal, ig_r + b_c - b_r, -jnp.inf)
        m_t = jnp.maximum(bm, jnp.max(logd, axis=-1, keepdims=True))
        d = jnp.exp(logd - m_t)
        s = _dot_nt(qh, kh) * d
        sv_ref[h] = _dot(s.astype(BF16), vh)
        col_ref[:, _COL_M + h:_COL_M + h + 1] = m_t
        col_ref[:, _COL_G + h:_COL_G + h + 1] = jnp.exp(bm - m_t)
        col_ref[:, _COL_RS + h:_COL_RS + h + 1] = jnp.sum(s, axis=-1, keepdims=True)

    qc_ref[...] = jnp.zeros(qc_ref.shape, F32)
    nr_ref[...] = jnp.zeros(nr_ref.shape, F32)

    def seq_update(g):
        last = g * ls + (ls - 1)
        in_seq = (rid >> ls_shift) == g
        is_last = rid == last
        for h in range(B_HEADS):
            c_old = cs_ref[g * B_HEADS + h]
            n_old = ns_ref[g * B_HEADS + h, 0:1, :]
            qm = jnp.where(in_seq, qs[h], jnp.zeros_like(qs[h])) if ns > 1 else qs[h]
            qc_ref[h] += _dot_nt(qm, c_old.astype(BF16))
            nr_ref[h] += jnp.where(in_seq, n_old, 0.0)
            m_t = col_ref[:, _COL_M + h:_COL_M + h + 1]
            gdec = col_ref[:, _COL_G + h:_COL_G + h + 1]
            b_c = bcol[:, B_HEADS + h:B_HEADS + h + 1]
            ig_c = gcol[:, h:h + 1]
            pick = lambda a: jnp.sum(jnp.where(is_last, a, 0.0), axis=0, keepdims=True)
            m_new, b_last, decay = pick(m_t), pick(b_c), pick(gdec)
            w_c = jnp.where(in_seq, jnp.exp(ig_c + b_last - b_c - m_new), 0.0)
            vw = (vs[h].astype(F32) * w_c).astype(BF16)
            cs_ref[g * B_HEADS + h] = decay * c_old + _dot_tn(vw, ks[h])
            ns_ref[g * B_HEADS + h, 0:1, :] = (decay * n_old
                                               + jnp.sum(ks[h].astype(F32) * w_c, axis=0, keepdims=True))
            if ns == 1:
                mp_ref[:, h:h + 1] = jnp.broadcast_to(m_new, (L, 1))

    if ns == 1:
        seq_update(0)
    else:
        lax.fori_loop(0, ns, lambda g, c: (seq_update(g), c)[1], 0)

    for h in range(B_HEADS):
        hs = slice(h * HP, (h + 1) * HP)
        m_t = col_ref[:, _COL_M + h:_COL_M + h + 1]
        gdec = col_ref[:, _COL_G + h:_COL_G + h + 1]
        rs = col_ref[:, _COL_RS + h:_COL_RS + h + 1]
        num = gdec * qc_ref[h] + sv_ref[h]
        qn = jnp.sum(qs[h].astype(F32) * nr_ref[h], axis=-1, keepdims=True)
        den = gdec * qn + rs
        hh = num / jnp.maximum(jnp.abs(den), jnp.exp(-m_t))
        yb_ref[:, hs] = so_ref[:, hs] * _rms(hh, nw_ref[:, hs], B_HEAD_DIM)

    mout_ref[...] = col_ref[...]
    for g in range(ns):
        for h in range(B_HEADS):
            cout_ref[g, h] = cs_ref[g * B_HEADS + h, :B_HEAD_DIM, :B_HEAD_DIM]
            nout_ref[g, h:h + 1, :] = ns_ref[g * B_HEADS + h, 0:1, :B_HEAD_DIM]


def _mlstm(q, k, v, so, gcol, grow, m0, c0, n0, nw, *, ns, n_tiles, n_chunks):
    m = q.shape[0]
    nb = c0.shape[0]
    ls = CHUNK // ns
    rows = lambda w: pl.BlockSpec((CHUNK, w), lambda t, c: (t * n_chunks + c, 0))
    nspec = pl.BlockSpec((ns, B_HEADS, B_HEAD_DIM), lambda t, c: (t, 0, 0))
    return pl.pallas_call(
        functools.partial(_mlstm_kernel, ns=ns, ls=ls),
        grid=(n_tiles, n_chunks),
        in_specs=[rows(BW_P), rows(BW_P), rows(BW_P), rows(BW_P), rows(LANE),
                  pl.BlockSpec((None, 16, CHUNK), lambda t, c: (t * n_chunks + c, 0, 0)),
                  pl.BlockSpec((None, CHUNK, LANE), lambda t, c: (t, 0, 0)),
                  pl.BlockSpec((ns, B_HEADS, B_HEAD_DIM, B_HEAD_DIM), lambda t, c: (t, 0, 0, 0)),
                  nspec,
                  _wspec((1, BW_P))],
        out_specs=(rows(BW_P),
                   pl.BlockSpec((ns, B_HEADS, B_HEAD_DIM, B_HEAD_DIM), lambda t, c: (t, 0, 0, 0)),
                   nspec,
                   pl.BlockSpec((None, CHUNK, LANE), lambda t, c: (t, 0, 0))),
        out_shape=(jax.ShapeDtypeStruct((m, BW_P), F32),
                   jax.ShapeDtypeStruct((nb, B_HEADS, B_HEAD_DIM, B_HEAD_DIM), F32),
                   jax.ShapeDtypeStruct((nb, B_HEADS, B_HEAD_DIM), F32),
                   jax.ShapeDtypeStruct((n_tiles, CHUNK, LANE), F32)),
        scratch_shapes=[pltpu.VMEM((ns * B_HEADS, HP, HP), F32), pltpu.VMEM((ns * B_HEADS, 8, HP), F32),
                        pltpu.VMEM((CHUNK, LANE), F32), pltpu.VMEM((CHUNK, LANE), F32),
                        pltpu.VMEM((B_HEADS, CHUNK, HP), F32), pltpu.VMEM((B_HEADS, CHUNK, HP), F32),
                        pltpu.VMEM((B_HEADS, CHUNK, HP), F32)],
        compiler_params=_params(2),
        name="mlstm",
    )(q, k, v, so, gcol, grow, m0, c0, n0, nw)


def _stack_heads(qc_ref, dtype):
    return jnp.concatenate([qc_ref[:, QC_SLOT * h:QC_SLOT * h + LATENT] for h in range(C_HEADS)],
                           axis=0).astype(dtype)


def _unstack_project(o, t, wuv_ref):
    wide = jnp.concatenate([o[h * t:(h + 1) * t] for h in range(C_HEADS)], axis=1)
    return _dot(wide.astype(BF16), wuv_ref[...])


def _softmax_step(s, pv, m_ref, acc_ref):
    reps = s.shape[1] // LANE
    m_prev = m_ref[...]
    m_new = jnp.maximum(m_prev, jnp.max(s, axis=-1, keepdims=True))
    p = jnp.exp(s - jnp.concatenate([m_new] * reps, axis=1))
    corr = jnp.exp(m_prev - m_new)
    acc_ref[...] = jnp.concatenate([corr, corr], axis=1) * acc_ref[...] + pv(p.astype(BF16))
    m_ref[...] = m_new


def _softmax_result(acc_ref):
    acc = acc_ref[...]
    return acc[:, :KV_LORA] / acc[:, KV_LORA:]


def _mla_prompt_kernel(qc_ref, lat_ref, wuv_ref, yc_ref, kb_ref, vb_ref, m_ref, acc_ref, *, tq, tk):
    qi = pl.program_id(1)

    @pl.when(qi == 0)
    def _cast_keys():
        lat = lat_ref[...]
        kb_ref[...] = lat.astype(BF16)
        vb_ref[:, 0:KV_LORA] = lat[:, 0:KV_LORA].astype(BF16)
        vb_ref[:, KV_LORA:2 * KV_LORA] = jnp.ones((lat.shape[0], KV_LORA), BF16)

    qst = _stack_heads(qc_ref, BF16)
    m_ref[...] = jnp.full(m_ref.shape, -jnp.inf, F32)
    acc_ref[...] = jnp.zeros(acc_ref.shape, F32)
    rows = C_HEADS * tq

    def block(j, masked):
        start = pl.multiple_of(j * tk, tk)
        s = _dot_nt(qst, kb_ref[pl.ds(start, tk), :])
        if masked:
            qpos = qi * tq + (lax.broadcasted_iota(jnp.int32, (rows, tk), 0) & (tq - 1))
            kpos = start + lax.broadcasted_iota(jnp.int32, (rows, tk), 1)
            s = jnp.where(kpos <= qpos, s, -jnp.inf)
        _softmax_step(s, lambda p: _dot(p, vb_ref[pl.ds(start, tk), :]), m_ref, acc_ref)

    n_full = (qi * tq) // tk
    lax.fori_loop(0, n_full, lambda j, c: (block(j, False), c)[1], 0)
    block(n_full, True)
    yc_ref[...] = _unstack_project(_softmax_result(acc_ref), tq, wuv_ref)


def _mla_prompt(qc, lat, wuv, *, nb, seq, tq, tk):
    m = qc.shape[0]
    n_q = seq // tq
    rows = C_HEADS * tq
    return pl.pallas_call(
        functools.partial(_mla_prompt_kernel, tq=tq, tk=tk),
        grid=(nb, n_q),
        in_specs=[pl.BlockSpec((tq, C_HEADS * QC_SLOT), lambda b, i: (b * n_q + i, 0)),
                  pl.BlockSpec((seq, LATENT), lambda b, i: (b, 0)),
                  _wspec((C_HEADS * KV_LORA, C_WIDTH))],
        out_specs=pl.BlockSpec((tq, C_WIDTH), lambda b, i: (b * n_q + i, 0)),
        out_shape=jax.ShapeDtypeStruct((m, C_WIDTH), F32),
        scratch_shapes=[pltpu.VMEM((seq, LATENT), BF16), pltpu.VMEM((seq, 2 * KV_LORA), BF16),
                        pltpu.VMEM((rows, LANE), F32), pltpu.VMEM((rows, 2 * KV_LORA), F32)],
        compiler_params=_params(2),
        name="mla_prompt",
    )(qc, lat, wuv)


def _mla_sample_kernel(pt_ref, qc_ref, latn_ref, wuv_ref, *rest, pages_per_step, t_new, seqs_per_step):
    n_pg = seqs_per_step * pages_per_step
    page_refs = rest[:n_pg]
    yc_ref, kt_ref, kn_ref, m_ref, acc_ref = rest[n_pg:]
    j = pl.program_id(1)
    rows = C_HEADS * t_new

    @pl.when(j == 0)
    def _init():
        m_ref[...] = jnp.full(m_ref.shape, -jnp.inf, F32)
        acc_ref[...] = jnp.zeros(acc_ref.shape, F32)

    qsts = []
    for r in range(seqs_per_step):
        rs = slice(r * t_new, (r + 1) * t_new)
        qst = jnp.concatenate([qc_ref[rs, QC_SLOT * h:QC_SLOT * h + LATENT] for h in range(C_HEADS)],
                              axis=0).astype(BF16)
        qsts.append(qst)
        for i in range(pages_per_step):
            kt_ref[r, :, i * PAGE_SIZE:(i + 1) * PAGE_SIZE] = page_refs[r * pages_per_step + i][...].astype(BF16)

        def pv_past(p, r=r):
            o = _dot_nt(p, kt_ref[r, 0:KV_LORA, :])
            return jnp.concatenate([o, jnp.broadcast_to(jnp.sum(p.astype(F32), axis=-1, keepdims=True),
                                                        (rows, KV_LORA))], axis=1)

        _softmax_step(_dot(qst, kt_ref[r]), pv_past, m_ref.at[r], acc_ref.at[r])

    @pl.when(j == pl.num_programs(1) - 1)
    def _new_rows_and_finish():
        tpos = lax.broadcasted_iota(jnp.int32, (rows, PAGE_SIZE), 0) & (t_new - 1)
        kidx = lax.broadcasted_iota(jnp.int32, (rows, PAGE_SIZE), 1)
        for r in range(seqs_per_step):
            kn_ref[r] = jnp.zeros(kn_ref.shape[1:], F32)
            kn_ref[r, 0:t_new, 0:LATENT] = latn_ref[r * t_new:(r + 1) * t_new, :]
            kn_ref[r, :, KV_LORA + LANE:] = jnp.ones((PAGE_SIZE, LANE), F32)
            kn = kn_ref[r].astype(BF16)
            s = jnp.where(kidx <= tpos, _dot_nt(qsts[r], kn[:, 0:LATENT]), -jnp.inf)
            vn = jnp.concatenate([kn[:, 0:KV_LORA], kn[:, KV_LORA + LANE:]], axis=1)
            _softmax_step(s, lambda p, vn=vn: _dot(p, vn), m_ref.at[r], acc_ref.at[r])
            yc_ref[r * t_new:(r + 1) * t_new, :] = _unstack_project(_softmax_result(acc_ref.at[r]), t_new, wuv_ref)


def _mla_sample(page_table, qc, lat_new, wuv, cache_t, *, layer, t_new, pages_per_step, seqs_per_step):
    m = qc.shape[0]
    nb, n_pages = page_table.shape
    n_steps = n_pages // pages_per_step
    rows = C_HEADS * t_new
    tr = seqs_per_step * t_new

    def page_spec(r, i):
        return pl.BlockSpec((None, None, LATENT, PAGE_SIZE),
                            lambda b, j, pt: (layer, pt[b * seqs_per_step + r, j * pages_per_step + i], 0, 0))

    grid_spec = pltpu.PrefetchScalarGridSpec(
        num_scalar_prefetch=1,
        grid=(nb // seqs_per_step, n_steps),
        in_specs=[pl.BlockSpec((tr, C_HEADS * QC_SLOT), lambda b, j, pt: (b, 0)),
                  pl.BlockSpec((tr, LATENT), lambda b, j, pt: (b, 0)),
                  pl.BlockSpec((C_HEADS * KV_LORA, C_WIDTH), lambda b, j, pt: (0, 0))]
                 + [page_spec(r, i) for r in range(seqs_per_step) for i in range(pages_per_step)],
        out_specs=pl.BlockSpec((tr, C_WIDTH), lambda b, j, pt: (b, 0)),
        scratch_shapes=[pltpu.VMEM((seqs_per_step, LATENT, pages_per_step * PAGE_SIZE), BF16),
                        pltpu.VMEM((seqs_per_step, PAGE_SIZE, KV_LORA + 2 * LANE), F32),
                        pltpu.VMEM((seqs_per_step, rows, LANE), F32),
                        pltpu.VMEM((seqs_per_step, rows, 2 * KV_LORA), F32)])
    return pl.pallas_call(
        functools.partial(_mla_sample_kernel, pages_per_step=pages_per_step, t_new=t_new,
                          seqs_per_step=seqs_per_step),
        grid_spec=grid_spec,
        out_shape=jax.ShapeDtypeStruct((m, C_WIDTH), F32),
        compiler_params=_params(2),
        name="mla_sample",
    )(page_table, qc, lat_new, wuv, *([cache_t] * (seqs_per_step * pages_per_step)))


def _outproj_kernel(x_ref, ya_ref, yb_ref, yc_ref, wo_ref, nxw_ref, wq_ref, xo_ref, q_ref):
    y = (_dot(ya_ref[...].astype(BF16), wo_ref[0:A_WIDTH, :])
         + _dot(yb_ref[...].astype(BF16), wo_ref[A_WIDTH:A_WIDTH + BW_P, :])
         + _dot(yc_ref[...].astype(BF16), wo_ref[A_WIDTH + BW_P:N_MIX_P, :]))
    xn = x_ref[...] + y
    xo_ref[...] = xn
    hq = _rms(xn, nxw_ref[...], D_MODEL).astype(BF16)
    q_ref[...] = (_dot(hq, wq_ref[...]) * XA_SCALE).astype(q_ref.dtype)


def _outproj(x, ya, yb, yc, wo, nxw, wq, *, tm, q_dtype):
    m = x.shape[0]
    row = lambda w: pl.BlockSpec((tm, w), lambda i: (i, 0))
    return pl.pallas_call(
        _outproj_kernel,
        grid=(m // tm,),
        in_specs=[row(D_MODEL), row(A_WIDTH), row(BW_P), row(C_WIDTH), _wspec((N_MIX_P, D_MODEL)),
                  _wspec((1, D_MODEL)), _wspec((D_MODEL, D_MODEL))],
        out_specs=(row(D_MODEL), row(D_MODEL)),
        out_shape=(jax.ShapeDtypeStruct((m, D_MODEL), F32), jax.ShapeDtypeStruct((m, D_MODEL), q_dtype)),
        compiler_params=_params(1),
        name="outproj",
    )(x, ya, yb, yc, wo, nxw, wq)


def _memkv_kernel(mem_ref, nw_ref, wk_ref, wv_ref, k_ref, v_ref, kb_ref, vb_ref):
    mn = _rms(mem_ref[...], nw_ref[...], D_MODEL).astype(BF16)
    k = _dot(mn, wk_ref[...])
    v = _dot(mn, wv_ref[...])
    kb_ref[...] = k.astype(BF16)
    vb_ref[...] = v.astype(BF16)
    for h in range(X_HEADS):
        hs = slice(h * X_HEAD_DIM, (h + 1) * X_HEAD_DIM)
        k_ref[:, h, :] = k[:, hs]
        v_ref[:, h, :] = v[:, hs]


def _memkv(mem, nw, wk, wv, *, tm):
    m = mem.shape[0]
    row = pl.BlockSpec((tm, D_MODEL), lambda i: (i, 0))
    out = pl.BlockSpec((tm, X_HEADS, X_HEAD_DIM), lambda i: (i, 0, 0))
    return pl.pallas_call(
        _memkv_kernel,
        grid=(m // tm,),
        in_specs=[row, _wspec((1, D_MODEL)), _wspec((D_MODEL, D_MODEL)), _wspec((D_MODEL, D_MODEL))],
        out_specs=(out, out, row, row),
        out_shape=(jax.ShapeDtypeStruct((m, X_HEADS, X_HEAD_DIM), F32),) * 2
                  + (jax.ShapeDtypeStruct((m, D_MODEL), BF16),) * 2,
        compiler_params=_params(1),
        name="memkv",
    )(mem, nw, wk, wv)


def _softmax_rows(s):
    p = jnp.exp(s - jnp.max(s, axis=-1, keepdims=True))
    return p / jnp.sum(p, axis=-1, keepdims=True)


def _xattn_prompt_kernel(x_ref, q_ref, k_ref, v_ref, wo_ref, xo_ref, o_ref):
    for h in range(X_HEADS):
        hs = slice(h * X_HEAD_DIM, (h + 1) * X_HEAD_DIM)
        p = _softmax_rows(_dot_nt(q_ref[:, hs], k_ref[:, hs]))
        o_ref[:, hs] = _dot(p.astype(BF16), v_ref[:, hs])
    xo_ref[...] = x_ref[...] + _dot(o_ref[...].astype(BF16), wo_ref[...])


def _xattn_prompt(x, q, k, v, wo, *, tm, tiles_per_seq):
    m = x.shape[0]
    row = pl.BlockSpec((tm, D_MODEL), lambda i: (i, 0))
    kv = pl.BlockSpec((MEM_LEN, D_MODEL), lambda i: (i // tiles_per_seq, 0))
    return pl.pallas_call(
        _xattn_prompt_kernel,
        grid=(m // tm,),
        in_specs=[row, row, kv, kv, _wspec((D_MODEL, D_MODEL))],
        out_specs=row,
        out_shape=jax.ShapeDtypeStruct((m, D_MODEL), F32),
        scratch_shapes=[pltpu.VMEM((tm, D_MODEL), F32)],
        compiler_params=_params(1),
        name="xattn_prompt",
    )(x, q, k, v, wo)


def _xattn_sample_kernel(x_ref, q_ref, k_ref, v_ref, wo_ref, xo_ref, o_ref, *, n_seq, t_rows):
    rows = X_HEADS * t_rows
    nk = MEM_LEN * X_HEADS
    r = lax.broadcasted_iota(jnp.int32, (rows, nk), 0)
    c = lax.broadcasted_iota(jnp.int32, (rows, nk), 1)
    keep = (c & (X_HEADS - 1)) == (r >> (t_rows.bit_length() - 1))
    for g in range(n_seq):
        rs = slice(g * t_rows, (g + 1) * t_rows)
        qst = jnp.concatenate([q_ref[rs, h * X_HEAD_DIM:(h + 1) * X_HEAD_DIM] for h in range(X_HEADS)], axis=0)
        k2 = k_ref[g].reshape(nk, X_HEAD_DIM).astype(BF16)
        v2 = v_ref[g].reshape(nk, X_HEAD_DIM).astype(BF16)
        p = _softmax_rows(jnp.where(keep, _dot_nt(qst.astype(BF16), k2), -jnp.inf))
        o = _dot(p.astype(BF16), v2)
        for h in range(X_HEADS):
            o_ref[rs, h * X_HEAD_DIM:(h + 1) * X_HEAD_DIM] = o[h * t_rows:(h + 1) * t_rows]
    xo_ref[...] = x_ref[...] + _dot(o_ref[...].astype(BF16), wo_ref[...])


def _xattn_sample(x, q, k, v, wo, *, layer, n_seq, t_rows):
    m = x.shape[0]
    r = n_seq * t_rows
    row = pl.BlockSpec((r, D_MODEL), lambda i: (i, 0))
    kv = pl.BlockSpec((None, n_seq, MEM_LEN, X_HEADS, X_HEAD_DIM), lambda i: (layer, i, 0, 0, 0))
    return pl.pallas_call(
        functools.partial(_xattn_sample_kernel, n_seq=n_seq, t_rows=t_rows),
        grid=(m // r,),
        in_specs=[row, row, kv, kv, _wspec((D_MODEL, D_MODEL))],
        out_specs=row,
        out_shape=jax.ShapeDtypeStruct((m, D_MODEL), F32),
        scratch_shapes=[pltpu.VMEM((r, D_MODEL), F32)],
        compiler_params=_params(1),
        name="xattn_sample",
    )(x, q, k, v, wo)


def _ffn_kernel(x_ref, nw_ref, wg_ref, wu_ref, wd_ref, cw_ref, cb_ref, p0_ref, p1_ref, fnw_ref, *rest,
                seq_rows, tiles_per_seq, final_norm):
    if final_norm:
        xo_ref, gtail_ref, y_ref, carry_ref = rest
    else:
        xo_ref, gtail_ref, carry_ref = rest
    tm = x_ref.shape[0]
    x = x_ref[...]
    xn = _rms(x, nw_ref[...], D_MODEL).astype(BF16)
    rid = lax.broadcasted_iota(jnp.int32, (tm, 1), 0)
    if seq_rows:
        pos = rid & (seq_rows - 1)
    else:
        pos = rid

        @pl.when(pl.program_id(0) % tiles_per_seq == 0)
        def _seq_start():
            carry_ref[0:1, :] = p0_ref[...]
            carry_ref[1:2, :] = p1_ref[...]

    acc = None
    for c in range(N_FF_CHUNKS):
        cs = slice(c * FF_CHUNK, (c + 1) * FF_CHUNK)
        g = _dot(xn, wg_ref[:, cs])
        u = _dot(xn, wu_ref[:, cs])
        if seq_rows:
            prev0, prev1 = p0_ref[:, cs], p1_ref[:, cs]
            gtail_ref[:, cs] = g
        else:
            prev0, prev1 = carry_ref[0:1, cs], carry_ref[1:2, cs]
            gtail_ref[:, cs] = g[tm - 2:tm, :]
            carry_ref[0:2, cs] = g[tm - 2:tm, :]
        g1 = jnp.where(pos == 0, prev1, pltpu.roll(g, 1, 0))
        g2 = jnp.where(pos == 0, prev0, jnp.where(pos == 1, prev1, pltpu.roll(g, 2, 0)))
        y = cb_ref[:, cs] + g2 * cw_ref[0:1, cs] + g1 * cw_ref[1:2, cs] + g * cw_ref[2:3, cs]
        d = _dot((_gelu(y) * u).astype(BF16), wd_ref[cs, :])
        acc = d if acc is None else acc + d
    xo = x + acc
    xo_ref[...] = xo
    if final_norm:
        y_ref[...] = _rms(xo, fnw_ref[...], D_MODEL)


def _ffn(x, nw, wg, wu, wd, cw, cb, p0, p1, fnw, *, tm, seq_rows, tiles_per_seq, final_norm):
    m = x.shape[0]
    n_tiles = m // tm
    row = pl.BlockSpec((tm, D_MODEL), lambda i: (i, 0))
    if seq_rows:
        pspec = pl.BlockSpec((tm, D_FF), lambda i: (i, 0))
        gspec = pspec
        gshape = jax.ShapeDtypeStruct((m, D_FF), F32)
    else:
        pspec = pl.BlockSpec((None, 1, D_FF), lambda i: (i // tiles_per_seq, 0, 0))
        gspec = pl.BlockSpec((None, 2, D_FF), lambda i: (i, 0, 0))
        gshape = jax.ShapeDtypeStruct((n_tiles, 2, D_FF), F32)
    return pl.pallas_call(
        functools.partial(_ffn_kernel, seq_rows=seq_rows, tiles_per_seq=tiles_per_seq, final_norm=final_norm),
        grid=(n_tiles,),
        in_specs=[row, _wspec((1, D_MODEL)), _wspec((D_MODEL, D_FF)), _wspec((D_MODEL, D_FF)),
                  _wspec((D_FF, D_MODEL)), _wspec((CONV_W, D_FF)), _wspec((1, D_FF)),
                  pspec, pspec, _wspec((1, D_MODEL))],
        out_specs=(row, gspec) + ((row,) if final_norm else ()),
        out_shape=(jax.ShapeDtypeStruct((m, D_MODEL), F32), gshape)
                  + ((jax.ShapeDtypeStruct((m, D_MODEL), F32),) if final_norm else ()),
        scratch_shapes=[pltpu.VMEM((8, D_FF), F32)],
        compiler_params=_params(1),
        name="ffn",
    )(x, nw, wg, wu, wd, cw, cb, p0, p1, fnw)


def _pad_heads_idx(start):
    idx = np.full((BW_P,), -1, np.int64)
    for h in range(B_HEADS):
        idx[h * HP:h * HP + B_HEAD_DIM] = start + h * B_HEAD_DIM + np.arange(B_HEAD_DIM)
    return idx


def _take_cols(w, idx):
    wz = jnp.concatenate([w, jnp.zeros(w.shape[:-1] + (1,), w.dtype)], axis=-1)
    return jnp.take(wz, jnp.asarray(np.where(idx < 0, w.shape[-1], idx)), axis=-1)


def _win_layout():
    half = C_ROPE // 2
    base = {'au': 0, 'av': 256, 'bq': 512, 'bk': 896, 'bv': 1280, 'bo': 1664, 'bi': 2048, 'bf': 2052,
            'cq': 2056, 'ckv': 2248, 'ckr': 2376}
    idx = np.full((N_IN,), -1, np.int64)
    idx[S_AU:S_AU + 256] = base['au'] + np.arange(256)
    idx[S_AV:S_AV + 256] = base['av'] + np.arange(256)
    for s, name in ((S_BQ, 'bq'), (S_BK, 'bk'), (S_BV, 'bv'), (S_BO, 'bo')):
        idx[s:s + BW_P] = _pad_heads_idx(base[name])
    idx[S_CQ:S_CQ + Q_LORA] = base['cq'] + np.arange(Q_LORA)
    idx[S_CKV:S_CKV + KV_LORA] = base['ckv'] + np.arange(KV_LORA)
    idx[S_R1:S_R1 + C_ROPE] = base['ckr'] + np.arange(C_ROPE)
    idx[S_R2:S_R2 + half] = base['ckr'] + half + np.arange(half)
    idx[S_R2 + half:S_R2 + C_ROPE] = base['ckr'] + np.arange(half)
    idx[S_G:S_G + B_HEADS] = base['bi'] + np.arange(B_HEADS)
    idx[S_G + B_HEADS:S_G + 2 * B_HEADS] = base['bf'] + np.arange(B_HEADS)
    return idx


def _wuq_layout():
    half = C_ROPE // 2
    per = C_NOPE + C_ROPE
    idx = np.full((N_UQ,), -1, np.int64)
    for h in range(C_HEADS):
        idx[UQ_NOPE + h * C_NOPE:UQ_NOPE + (h + 1) * C_NOPE] = h * per + np.arange(C_NOPE)
        idx[UQ_RA + h * LANE:UQ_RA + h * LANE + C_ROPE] = h * per + C_NOPE + np.arange(C_ROPE)
        idx[UQ_RB + h * LANE:UQ_RB + h * LANE + half] = h * per + C_NOPE + half + np.arange(half)
        idx[UQ_RB + h * LANE + half:UQ_RB + h * LANE + C_ROPE] = h * per + C_NOPE + np.arange(half)
    return idx


def _rope_tables(pos, reps):
    half = C_ROPE // 2
    inv = ROPE_THETA ** (-jnp.arange(half, dtype=F32) / half)
    ang = pos.astype(F32)[:, None] * inv[None, :]
    cos, sin = jnp.cos(ang), jnp.sin(ang)
    z = jnp.zeros((pos.shape[0], LANE - C_ROPE), F32)
    ct = jnp.concatenate([cos, cos, z], axis=1)
    st = jnp.concatenate([-sin, sin, z], axis=1)
    return jnp.tile(ct, (reps, 1)), jnp.tile(st, (reps, 1))


def _pad_lanes(v, idx):
    return _take_cols(v[None, :], idx)


def _layer_weights(l, w):
    hp_idx = _pad_heads_idx(0)
    out = {}
    out['nw_mix'] = w['norm_mix_w'][l][None, :]
    out['w1'] = _take_cols(w['w_in'][l], _win_layout()).astype(BF16)
    out['qnw'] = _pad_lanes(w['mla_qnorm_w'][l], np.concatenate([np.arange(Q_LORA), np.full(CQ_P - Q_LORA, -1)]))
    out['kvnw'] = w['mla_kvnorm_w'][l][None, :]
    wuq = w['mla_w_uq'][l].reshape(Q_LORA, C_HEADS * (C_NOPE + C_ROPE))
    wuq = _take_cols(wuq, _wuq_layout())
    out['wuq'] = jnp.concatenate([wuq, jnp.zeros((CQ_P - Q_LORA, N_UQ), F32)], axis=0).astype(BF16)
    wuk = jnp.transpose(w['mla_w_uk'][l], (1, 2, 0))
    eye = jnp.eye(C_HEADS, dtype=F32)
    out['wuk'] = jnp.einsum('hdc,hg->hdgc', wuk, eye).reshape(C_HEADS * C_NOPE, C_HEADS * KV_LORA).astype(BF16)
    wuv = jnp.transpose(w['mla_w_uv'][l], (1, 0, 2))
    out['wuv'] = jnp.einsum('hce,hg->hcge', wuv, eye).reshape(C_HEADS * KV_LORA, C_WIDTH).astype(BF16)
    gb = jnp.concatenate([w['mlstm_bi'][l], w['mlstm_bf'][l]])
    out['gb'] = _pad_lanes(gb, np.concatenate([np.arange(2 * B_HEADS), np.full(LANE - 2 * B_HEADS, -1)]))
    out['mnw'] = _pad_lanes(w['mlstm_norm_w'][l], hp_idx)
    wo = w['w_out'][l]
    rows = np.concatenate([np.arange(A_WIDTH), np.where(hp_idx < 0, -1, hp_idx + A_WIDTH),
                           A_WIDTH + B_WIDTH + np.arange(C_WIDTH)])
    out['wo'] = jnp.transpose(_take_cols(jnp.transpose(wo), rows)).astype(BF16)
    out['nw_x'] = w['norm_x_w'][l][None, :]
    out['nw_mem'] = w['norm_mem_w'][l][None, :]
    out['xwq'] = w['xa_wq'][l].reshape(D_MODEL, D_MODEL).astype(BF16)
    out['xwk'] = w['xa_wk'][l].reshape(D_MODEL, D_MODEL).astype(BF16)
    out['xwv'] = w['xa_wv'][l].reshape(D_MODEL, D_MODEL).astype(BF16)
    out['xwo'] = w['xa_wo'][l].reshape(D_MODEL, D_MODEL).astype(BF16)
    out['nw_ffn'] = w['norm_ffn_w'][l][None, :]
    out['wg'] = w['ffn_wg'][l].astype(BF16)
    out['wu'] = w['ffn_wu'][l].astype(BF16)
    out['wd'] = w['ffn_wd'][l].astype(BF16)
    out['cw'] = w['ffn_conv_w'][l]
    out['cb'] = w['ffn_conv_b'][l][None, :]
    return out


def _chunk_mixers(ws, cb, t_rows):
    reps = CHUNK // t_rows
    wt = ws[:, :t_rows, :t_rows]
    if reps > 1:
        wt = jnp.einsum('ab,hts->hatbs', jnp.eye(reps, dtype=F32), wt).reshape(A_HEADS, CHUNK, CHUNK)
    bias = jnp.tile(jnp.repeat(jnp.transpose(cb[:, :t_rows]), A_HEAD_DIM, axis=1), (reps, 1))
    return wt, bias


def _mixer_block(x, lw, *, mlstm_state, pos_tables, chunk_w, tm, act_dtype, ns, n_tiles, n_chunks, mla_fn):
    m = x.shape[0]
    cos_t, sin_t = pos_tables
    gu, gv, q, k, v, so, qc, lat, gate = _inproj(x, lw['nw_mix'], lw['w1'], lw['qnw'], lw['kvnw'], lw['wuq'],
                                                 lw['wuk'], lw['gb'], cos_t, sin_t, tm=tm, act_dtype=act_dtype)
    ya = _chunkmlp(gu, gv, chunk_w[0], chunk_w[1], tm=tm)
    grow = jnp.transpose(gate[:, :16].reshape(m // CHUNK, CHUNK, 16), (0, 2, 1))
    c0, n0, m0 = mlstm_state
    yb, c_new, n_new, m_rows = _mlstm(q, k, v, so, gate, grow, m0, c0, n0, lw['mnw'],
                                      ns=ns, n_tiles=n_tiles, n_chunks=n_chunks)
    yc = mla_fn(qc, lat)
    return (ya, yb, yc), (c_new, n_new, m_rows), lat, gv


def kernel(x_prompt, x_sample, mem_prompt, cache_mla, page_table, cache_mem_k, cache_mem_v, state_mlstm_C, state_mlstm_n, state_mlstm_m, state_ffn_conv, norm_mix_w, w_in, chunk_ws, chunk_b, mlstm_bi, mlstm_bf, mlstm_norm_w, mla_qnorm_w, mla_kvnorm_w, mla_w_uq, mla_w_uk, mla_w_uv, w_out, norm_x_w, norm_mem_w, xa_wq, xa_wk, xa_wv, xa_wo, norm_ffn_w, ffn_wg, ffn_wu, ffn_conv_w, ffn_conv_b, ffn_wd, norm_final_w):
    w = dict(norm_mix_w=norm_mix_w, w_in=w_in, mlstm_bi=mlstm_bi, mlstm_bf=mlstm_bf, mlstm_norm_w=mlstm_norm_w,
             mla_qnorm_w=mla_qnorm_w, mla_kvnorm_w=mla_kvnorm_w, mla_w_uq=mla_w_uq, mla_w_uk=mla_w_uk,
             mla_w_uv=mla_w_uv, w_out=w_out, norm_x_w=norm_x_w, norm_mem_w=norm_mem_w, xa_wq=xa_wq, xa_wk=xa_wk,
             xa_wv=xa_wv, xa_wo=xa_wo, norm_ffn_w=norm_ffn_w, ffn_wg=ffn_wg, ffn_wu=ffn_wu,
             ffn_conv_w=ffn_conv_w, ffn_conv_b=ffn_conv_b, ffn_wd=ffn_wd)
    depth = w_in.shape[0]
    nbp, S, _ = x_prompt.shape
    nbs, T, _ = x_sample.shape
    n_pages = page_table.shape[1]
    past_len = n_pages * PAGE_SIZE
    mp, ms = nbp * S, nbs * T
    assert S % CHUNK == 0 and CHUNK % T == 0 and T % 8 == 0 and ms % CHUNK == 0

    tm_p = min(512, S)
    tm_s = min(512, ms)
    tq = min(256, S)
    tk = min(512, S)
    ns_s = CHUNK // T
    pages_per_step = min(32, n_pages)
    xg = 4 if nbs % 4 == 0 else 1
    tm_ffn_p = min(512, S)
    tm_ffn_s = min(256, ms)

    pos_p = _rope_tables(jnp.arange(S), 1)
    pos_s = _rope_tables(past_len + jnp.arange(T), tm_s // T)
    fnw = norm_final_w[None, :]

    xp = x_prompt.reshape(mp, D_MODEL)
    xs = x_sample.reshape(ms, D_MODEL)
    mem = mem_prompt.reshape(nbp * MEM_LEN, D_MODEL)

    zero_c = jnp.zeros((nbp, B_HEADS, B_HEAD_DIM, B_HEAD_DIM), F32)
    zero_n = jnp.zeros((nbp, B_HEADS, B_HEAD_DIM), F32)
    zero_m = jnp.zeros((nbp, CHUNK, LANE), F32)
    zero_conv = jnp.zeros((nbp, 1, D_FF), F32)
    cache_t = jnp.swapaxes(cache_mla, 2, 3)

    outs = {k: [] for k in ('lat_p', 'lat_s', 'mk', 'mv', 'cp', 'np', 'mp', 'cs', 'ns', 'ms', 'conv_p', 'conv_s',
                            'chunkv')}
    yp = ys = None
    for l in range(depth):
        lw = _layer_weights(l, w)
        last = l == depth - 1

        mla_p = functools.partial(_mla_prompt, wuv=lw['wuv'], nb=nbp, seq=S, tq=tq, tk=tk)
        ymix, st, lat, _ = _mixer_block(
            xp, lw, mlstm_state=(zero_c, zero_n, zero_m), pos_tables=pos_p,
            chunk_w=_chunk_mixers(chunk_ws[l], chunk_b[l], CHUNK), tm=tm_p, act_dtype=BF16, ns=1, n_tiles=nbp,
            n_chunks=S // CHUNK, mla_fn=lambda qc, lt: mla_p(qc, lt))
        xp, qx_p = _outproj(xp, *ymix, lw['wo'], lw['nw_x'], lw['xwq'], tm=tm_p, q_dtype=BF16)
        outs['lat_p'].append(lat.reshape(nbp, S, LATENT))
        outs['cp'].append(st[0])
        outs['np'].append(st[1])
        outs['mp'].append(st[2][:, CHUNK - 1, :B_HEADS])

        m0_rows = jnp.concatenate([jnp.repeat(state_mlstm_m[l], T, axis=0),
                                   jnp.zeros((ms, LANE - B_HEADS), F32)], axis=1).reshape(ms // CHUNK, CHUNK, LANE)
        mla_s = functools.partial(_mla_sample, page_table, wuv=lw['wuv'], cache_t=cache_t, layer=l, t_new=T,
                                  pages_per_step=pages_per_step, seqs_per_step=2 if nbs % 2 == 0 else 1)
        ymix, st, lat, gv = _mixer_block(
            xs, lw, mlstm_state=(state_mlstm_C[l], state_mlstm_n[l], m0_rows), pos_tables=pos_s,
            chunk_w=_chunk_mixers(chunk_ws[l], chunk_b[l], T), tm=tm_s, act_dtype=F32, ns=ns_s,
            n_tiles=ms // CHUNK, n_chunks=1, mla_fn=lambda qc, lt: mla_s(qc, lt))
        xs, qx_s = _outproj(xs, *ymix, lw['wo'], lw['nw_x'], lw['xwq'], tm=tm_s, q_dtype=F32)
        outs['lat_s'].append(lat.reshape(nbs, T, LATENT))
        outs['cs'].append(st[0])
        outs['ns'].append(st[1])
        outs['ms'].append(st[2].reshape(nbs, T, LANE)[:, T - 1, :B_HEADS])
        outs['chunkv'].append(gv.reshape(nbs, T, A_WIDTH))

        mk, mv, mk_b, mv_b = _memkv(mem, lw['nw_mem'], lw['xwk'], lw['xwv'], tm=min(512, nbp * MEM_LEN))
        outs['mk'].append(mk.reshape(nbp, MEM_LEN, X_HEADS, X_HEAD_DIM))
        outs['mv'].append(mv.reshape(nbp, MEM_LEN, X_HEADS, X_HEAD_DIM))
        xp = _xattn_prompt(xp, qx_p, mk_b, mv_b, lw['xwo'], tm=tm_p, tiles_per_seq=S // tm_p)
        xs = _xattn_sample(xs, qx_s, cache_mem_k, cache_mem_v, lw['xwo'], layer=l, n_seq=xg, t_rows=T)

        res_p = _ffn(xp, lw['nw_ffn'], lw['wg'], lw['wu'], lw['wd'], lw['cw'], lw['cb'], zero_conv, zero_conv, fnw,
                     tm=tm_ffn_p, seq_rows=0, tiles_per_seq=S // tm_ffn_p, final_norm=last)
        buf = state_ffn_conv[l]
        p0 = jnp.repeat(buf[:, 0, :], T, axis=0)
        p1 = jnp.repeat(buf[:, 1, :], T, axis=0)
        res_s = _ffn(xs, lw['nw_ffn'], lw['wg'], lw['wu'], lw['wd'], lw['cw'], lw['cb'], p0, p1, fnw,
                     tm=tm_ffn_s, seq_rows=T, tiles_per_seq=1, final_norm=last)
        xp, xs = res_p[0], res_s[0]
        outs['conv_p'].append(res_p[1][S // tm_ffn_p - 1::S // tm_ffn_p])
        outs['conv_s'].append(res_s[1].reshape(nbs, T, D_FF)[:, T - (CONV_W - 1):, :])
        if last:
            yp, ys = res_p[2], res_s[2]

    st = lambda k: jnp.stack(outs[k])
    return (yp.reshape(nbp, S, D_MODEL), ys.reshape(nbs, T, D_MODEL),
            st('lat_p'), st('lat_s'), st('mk'), st('mv'),
            st('cp'), st('np'), st('mp'), st('cs'), st('ns'), st('ms'),
            st('conv_p'), st('conv_s'), st('chunkv'))
```

```python
import functools
import math

import numpy as np
import jax
import jax.numpy as jnp
from jax import lax
from jax.experimental import pallas as pl
from jax.experimental.pallas import tpu as pltpu

F32 = jnp.float32
BF16 = jnp.bfloat16

D_MODEL = 1024
PAGE_SIZE = 128
A_HEADS = 4
A_WIDTH = 256
A_HEAD_DIM = 64
CHUNK = 128
B_HEADS = 4
B_WIDTH = 384
B_HEAD_DIM = 96
MLSTM_KSCALE = B_HEAD_DIM ** -0.5
C_WIDTH = 384
C_HEADS = 6
C_V_DIM = 64
C_NOPE = 64
C_ROPE = 32
Q_LORA = 192
KV_LORA = 128
LATENT = KV_LORA + C_ROPE
MLA_SCALE = (C_NOPE + C_ROPE) ** -0.5
ROPE_THETA = 10000.0
MEM_LEN = 256
X_HEADS = 4
X_HEAD_DIM = 256
XA_SCALE = X_HEAD_DIM ** -0.5
D_FF = 2816
CONV_W = 3
EPS = 1e-6

LANE = 128
HP = 128
BW_P = B_HEADS * HP
FF_CHUNK = 256
N_FF_CHUNKS = D_FF // FF_CHUNK
VMEM_LIMIT = 48 * 1024 * 1024

S_AU, S_AV, S_BQ, S_BK, S_BV, S_BO = 0, 256, 512, 1024, 1536, 2048
S_CQ, S_CKV, S_R1, S_R2, S_G = 2560, 2816, 2944, 3072, 3200
N_IN = 3328
CQ_P = 256
UQ_NOPE, UQ_RA, UQ_RB, N_UQ = 0, 384, 1152, 1920
QC_SLOT = 256
N_MIX_P = A_WIDTH + BW_P + C_WIDTH


def _rms(x, w, n):
    ms = jnp.sum(x * x, axis=-1, keepdims=True) * (1.0 / n)
    return x * lax.rsqrt(ms + EPS) * w


def _gelu(x):
    c = math.sqrt(2.0 / math.pi)
    return x * (0.5 * (1.0 + jnp.tanh(c * (x + 0.044715 * (x * x * x)))))


def _sigmoid(x):
    return 1.0 / (1.0 + jnp.exp(-x))


def _log_sigmoid(x):
    return jnp.minimum(x, 0.0) - jnp.log1p(jnp.exp(-jnp.abs(x)))


def _dot(a, b):
    return jnp.dot(a, b, preferred_element_type=F32)


def _dot_nt(a, b):
    return lax.dot_general(a, b, (((1,), (1,)), ((), ())), preferred_element_type=F32)


def _dot_tn(a, b):
    return lax.dot_general(a, b, (((0,), (0,)), ((), ())), preferred_element_type=F32)


def _split3(a):
    a1 = a.astype(BF16)
    r1 = a - a1.astype(F32)
    a2 = r1.astype(BF16)
    a3 = (r1 - a2.astype(F32)).astype(BF16)
    return a1, a2, a3


def _wspec(shape):
    nd = len(shape)
    return pl.BlockSpec(shape, lambda *_: (0,) * nd, pipeline_mode=pl.Buffered(1))


def _params(n_axes):
    return pltpu.CompilerParams(dimension_semantics=("arbitrary",) * n_axes,
                                vmem_limit_bytes=VMEM_LIMIT)


def _inproj_kernel(x_ref, nw_ref, w1_ref, qnw_ref, kvnw_ref, wuq_ref, wuk_ref, gb_ref, cos_ref, sin_ref,
                   gu_ref, gv_ref, q_ref, k_ref, v_ref, so_ref, qc_ref, lat_ref, gate_ref):
    xn = _rms(x_ref[...], nw_ref[...], D_MODEL).astype(BF16)

    def seg(a, b):
        return _dot(xn, w1_ref[:, a:b])

    gu_ref[...] = _gelu(seg(S_AU, S_AV))
    gv_ref[...] = _gelu(seg(S_AV, S_BQ))
    q_ref[...] = seg(S_BQ, S_BK).astype(q_ref.dtype)
    k_ref[...] = (seg(S_BK, S_BV) * MLSTM_KSCALE).astype(k_ref.dtype)
    v_ref[...] = seg(S_BV, S_BO).astype(v_ref.dtype)
    so_ref[...] = _sigmoid(seg(S_BO, S_CQ))

    c_q = _rms(seg(S_CQ, S_CKV), qnw_ref[...], Q_LORA).astype(BF16)
    q2 = _dot(c_q, wuq_ref[...])
    qlat = _dot(q2[:, UQ_NOPE:UQ_RA].astype(BF16), wuk_ref[...])
    cos = cos_ref[...]
    sin = sin_ref[...]
    for h in range(C_HEADS):
        ra = q2[:, UQ_RA + LANE * h:UQ_RA + LANE * (h + 1)]
        rb = q2[:, UQ_RB + LANE * h:UQ_RB + LANE * (h + 1)]
        qr = ra * cos + rb * sin
        qc_ref[:, QC_SLOT * h:QC_SLOT * h + LANE] = (qlat[:, LANE * h:LANE * (h + 1)] * MLA_SCALE).astype(qc_ref.dtype)
        qc_ref[:, QC_SLOT * h + LANE:QC_SLOT * (h + 1)] = (qr * MLA_SCALE).astype(qc_ref.dtype)

    lat_ref[:, 0:KV_LORA] = _rms(seg(S_CKV, S_R1), kvnw_ref[...], KV_LORA)
    kr = seg(S_R1, S_R2) * cos + seg(S_R2, S_G) * sin
    lat_ref[:, KV_LORA:LATENT] = kr[:, :C_ROPE]

    gt = seg(S_G, N_IN) + gb_ref[...]
    lane = lax.broadcasted_iota(jnp.int32, gt.shape, 1)
    gate_ref[...] = jnp.where(lane < B_HEADS, gt, jnp.where(lane < 2 * B_HEADS, _log_sigmoid(gt), 0.0))


def _inproj(x, nw, w1, qnw, kvnw, wuq, wuk, gb, cos_t, sin_t, *, tm, act_dtype):
    m = x.shape[0]
    n_pos_blocks = cos_t.shape[0] // tm
    row = lambda w: pl.BlockSpec((tm, w), lambda i: (i, 0))
    pos = pl.BlockSpec((tm, LANE), lambda i: (i % n_pos_blocks, 0))
    out_shapes = (
        jax.ShapeDtypeStruct((m, A_WIDTH), F32), jax.ShapeDtypeStruct((m, A_WIDTH), F32),
        jax.ShapeDtypeStruct((m, BW_P), act_dtype), jax.ShapeDtypeStruct((m, BW_P), act_dtype),
        jax.ShapeDtypeStruct((m, BW_P), act_dtype), jax.ShapeDtypeStruct((m, BW_P), F32),
        jax.ShapeDtypeStruct((m, C_HEADS * QC_SLOT), act_dtype), jax.ShapeDtypeStruct((m, LATENT), F32),
        jax.ShapeDtypeStruct((m, LANE), F32))
    return pl.pallas_call(
        _inproj_kernel,
        grid=(m // tm,),
        in_specs=[row(D_MODEL), _wspec((1, D_MODEL)), _wspec((D_MODEL, N_IN)), _wspec((1, CQ_P)),
                  _wspec((1, KV_LORA)), _wspec((CQ_P, N_UQ)), _wspec((C_HEADS * C_NOPE, C_HEADS * LANE)),
                  _wspec((1, LANE)), pos, pos],
        out_specs=(row(A_WIDTH), row(A_WIDTH), row(BW_P), row(BW_P), row(BW_P), row(BW_P),
                   row(C_HEADS * QC_SLOT), row(LATENT), row(LANE)),
        out_shape=out_shapes,
        compiler_params=_params(1),
        name="inproj",
    )(x, nw, w1, qnw, kvnw, wuq, wuk, gb, cos_t, sin_t)


def _chunkmlp_kernel(gu_ref, gv_ref, ws_ref, bias_ref, ya_ref, *, n_chunks):
    r = lax.broadcasted_iota(jnp.int32, (CHUNK, CHUNK), 0)
    c = lax.broadcasted_iota(jnp.int32, (CHUNK, CHUNK), 1)
    wsm = [jnp.where(r >= c, ws_ref[h], 0.0).astype(BF16) for h in range(A_HEADS)]
    head_of_lane = lax.broadcasted_iota(jnp.int32, (CHUNK, A_WIDTH), 1) >> (A_HEAD_DIM.bit_length() - 1)
    for ci in range(n_chunks):
        rows = slice(ci * CHUNK, (ci + 1) * CHUNK)
        gv = gv_ref[rows, :]
        z = bias_ref[...]
        for h in range(A_HEADS):
            z = z + _dot(wsm[h], jnp.where(head_of_lane == h, gv, 0.0).astype(BF16))
        ya_ref[rows, :] = gu_ref[rows, :] * z


def _chunkmlp(gu, gv, ws, bias, *, tm):
    m = gu.shape[0]
    row = pl.BlockSpec((tm, A_WIDTH), lambda i: (i, 0))
    return pl.pallas_call(
        functools.partial(_chunkmlp_kernel, n_chunks=tm // CHUNK),
        grid=(m // tm,),
        in_specs=[row, row, _wspec((A_HEADS, CHUNK, CHUNK)), _wspec((CHUNK, A_WIDTH))],
        out_specs=row,
        out_shape=jax.ShapeDtypeStruct((m, A_WIDTH), F32),
        compiler_params=_params(1),
        name="chunkmlp",
    )(gu, gv, ws, bias)


_COL_M, _COL_G, _COL_RS = 0, 8, 16


def _mlstm_kernel(q_ref, k_ref, v_ref, so_ref, gcol_ref, grow_ref, m0_ref, c0_ref, n0_ref, nw_ref,
                  yb_ref, cout_ref, nout_ref, mout_ref,
                  cs_ref, ns_ref, mp_ref, col_ref, sv_ref, qc_ref, nr_ref, *, ns, ls, g_tiles):
    @pl.when(pl.program_id(1) == 0)
    def _init():
        cs_ref[...] = jnp.zeros(cs_ref.shape, F32)
        ns_ref[...] = jnp.zeros(ns_ref.shape, F32)
        for s in range(g_tiles):
            for g in range(ns):
                for h in range(B_HEADS):
                    cs_ref[s, g * B_HEADS + h, :B_HEAD_DIM, :B_HEAD_DIM] = c0_ref[s, g, h]
                    ns_ref[s, g * B_HEADS + h, 0:1, :B_HEAD_DIM] = n0_ref[s, g, h:h + 1, :]
        mp_ref[...] = m0_ref[...]

    for s in range(g_tiles):
        at = lambda r: r.at[s]
        _mlstm_tile(at(q_ref), at(k_ref), at(v_ref), at(so_ref), at(gcol_ref), at(grow_ref), nw_ref,
                    at(yb_ref), at(cout_ref), at(nout_ref), at(mout_ref),
                    at(cs_ref), at(ns_ref), at(mp_ref), at(col_ref), at(sv_ref), at(qc_ref), at(nr_ref),
                    ns=ns, ls=ls)


def _mlstm_tile(q_ref, k_ref, v_ref, so_ref, gcol_ref, grow_ref, nw_ref,
                yb_ref, cout_ref, nout_ref, mout_ref,
                cs_ref, ns_ref, mp_ref, col_ref, sv_ref, qc_ref, nr_ref, *, ns, ls):
    L = CHUNK
    ls_shift = ls.bit_length() - 1

    col_ref[...] = jnp.zeros(col_ref.shape, F32)
    row = lax.broadcasted_iota(jnp.int32, (L, L), 0)
    col = lax.broadcasted_iota(jnp.int32, (L, L), 1)
    causal = (col <= row) & ((col >> ls_shift) == (row >> ls_shift))
    cum = jnp.where(causal, 1.0, 0.0).astype(BF16)
    rid = lax.broadcasted_iota(jnp.int32, (L, 1), 0)

    gcol = gcol_ref[...]
    grow = grow_ref[...]
    bcol = sum(_dot(cum, t) for t in _split3(gcol))
    brow = sum(_dot_nt(t, cum) for t in _split3(grow))
    mp = mp_ref[...]

    qs, ks, vs = [], [], []
    for h in range(B_HEADS):
        hs = slice(h * HP, (h + 1) * HP)
        qh = q_ref[:, hs].astype(BF16)
        kh = k_ref[:, hs].astype(BF16)
        vh = v_ref[:, hs].astype(BF16)
        qs.append(qh), ks.append(kh), vs.append(vh)
        ig_r = grow[h:h + 1, :]
        b_r = brow[B_HEADS + h:B_HEADS + h + 1, :]
        b_c = bcol[:, B_HEADS + h:B_HEADS + h + 1]
        bm = b_c + mp[:, h:h + 1]
        logd = jnp.where(causal, ig_r + b_c - b_r, -jnp.inf)
        m_t = jnp.maximum(bm, jnp.max(logd, axis=-1, keepdims=True))
        d = jnp.exp(logd - m_t)
        s = _dot_nt(qh, kh) * d
        sv_ref[h] = _dot(s.astype(BF16), vh)
        col_ref[:, _COL_M + h:_COL_M + h + 1] = m_t
        col_ref[:, _COL_G + h:_COL_G + h + 1] = jnp.exp(bm - m_t)
        col_ref[:, _COL_RS + h:_COL_RS + h + 1] = jnp.sum(s, axis=-1, keepdims=True)

    qc_ref[...] = jnp.zeros(qc_ref.shape, F32)
    nr_ref[...] = jnp.zeros(nr_ref.shape, F32)

    def seq_update(g):
        last = g * ls + (ls - 1)
        in_seq = (rid >> ls_shift) == g
        is_last = rid == last
        for h in range(B_HEADS):
            c_old = cs_ref[g * B_HEADS + h]
            n_old = ns_ref[g * B_HEADS + h, 0:1, :]
            qm = jnp.where(in_seq, qs[h], jnp.zeros_like(qs[h])) if ns > 1 else qs[h]
            qc_ref[h] += _dot_nt(qm, c_old.astype(BF16))
            nr_ref[h] += jnp.where(in_seq, n_old, 0.0)
            m_t = col_ref[:, _COL_M + h:_COL_M + h + 1]
            gdec = col_ref[:, _COL_G + h:_COL_G + h + 1]
            b_c = bcol[:, B_HEADS + h:B_HEADS + h + 1]
            ig_c = gcol[:, h:h + 1]
            pick = lambda a: jnp.sum(jnp.where(is_last, a, 0.0), axis=0, keepdims=True)
            m_new, b_last, decay = pick(m_t), pick(b_c), pick(gdec)
            w_c = jnp.where(in_seq, jnp.exp(ig_c + b_last - b_c - m_new), 0.0)
            vw = (vs[h].astype(F32) * w_c).astype(BF16)
            cs_ref[g * B_HEADS + h] = decay * c_old + _dot_tn(vw, ks[h])
            ns_ref[g * B_HEADS + h, 0:1, :] = (decay * n_old
                                               + jnp.sum(ks[h].astype(F32) * w_c, axis=0, keepdims=True))
            if ns == 1:
                mp_ref[:, h:h + 1] = jnp.broadcast_to(m_new, (L, 1))

    if ns == 1:
        seq_update(0)
    else:
        lax.fori_loop(0, ns, lambda g, c: (seq_update(g), c)[1], 0)

    for h in range(B_HEADS):
        hs = slice(h * HP, (h + 1) * HP)
        m_t = col_ref[:, _COL_M + h:_COL_M + h + 1]
        gdec = col_ref[:, _COL_G + h:_COL_G + h + 1]
        rs = col_ref[:, _COL_RS + h:_COL_RS + h + 1]
        num = gdec * qc_ref[h] + sv_ref[h]
        qn = jnp.sum(qs[h].astype(F32) * nr_ref[h], axis=-1, keepdims=True)
        den = gdec * qn + rs
        hh = num / jnp.maximum(jnp.abs(den), jnp.exp(-m_t))
        yb_ref[:, hs] = so_ref[:, hs] * _rms(hh, nw_ref[:, hs], B_HEAD_DIM)

    mout_ref[...] = col_ref[...]
    for g in range(ns):
        for h in range(B_HEADS):
            cout_ref[g, h] = cs_ref[g * B_HEADS + h, :B_HEAD_DIM, :B_HEAD_DIM]
            nout_ref[g, h:h + 1, :] = ns_ref[g * B_HEADS + h, 0:1, :B_HEAD_DIM]


def _mlstm(q, k, v, so, gcol, grow, m0, c0, n0, nw, *, ns, n_tiles, n_chunks, g_tiles):
    m = q.shape[0]
    nb = c0.shape[0]
    ls = CHUNK // ns
    g = g_tiles
    t3 = lambda a: a.reshape(n_tiles, n_chunks * CHUNK, a.shape[-1])
    rows = lambda w: pl.BlockSpec((g, CHUNK, w), lambda t, c: (t, c, 0))
    cspec = pl.BlockSpec((g, ns, B_HEADS, B_HEAD_DIM, B_HEAD_DIM), lambda t, c: (t, 0, 0, 0, 0))
    nspec = pl.BlockSpec((g, ns, B_HEADS, B_HEAD_DIM), lambda t, c: (t, 0, 0, 0))
    mspec = pl.BlockSpec((g, CHUNK, LANE), lambda t, c: (t, 0, 0))
    yb, c_new, n_new, m_rows = pl.pallas_call(
        functools.partial(_mlstm_kernel, ns=ns, ls=ls, g_tiles=g),
        grid=(n_tiles // g, n_chunks),
        in_specs=[rows(BW_P), rows(BW_P), rows(BW_P), rows(BW_P), rows(LANE),
                  pl.BlockSpec((g, None, 16, CHUNK), lambda t, c: (t, c, 0, 0)),
                  mspec, cspec, nspec, _wspec((1, BW_P))],
        out_specs=(rows(BW_P), cspec, nspec, mspec),
        out_shape=(jax.ShapeDtypeStruct((n_tiles, n_chunks * CHUNK, BW_P), F32),
                   jax.ShapeDtypeStruct((n_tiles, ns, B_HEADS, B_HEAD_DIM, B_HEAD_DIM), F32),
                   jax.ShapeDtypeStruct((n_tiles, ns, B_HEADS, B_HEAD_DIM), F32),
                   jax.ShapeDtypeStruct((n_tiles, CHUNK, LANE), F32)),
        scratch_shapes=[pltpu.VMEM((g, ns * B_HEADS, HP, HP), F32), pltpu.VMEM((g, ns * B_HEADS, 8, HP), F32),
                        pltpu.VMEM((g, CHUNK, LANE), F32), pltpu.VMEM((g, CHUNK, LANE), F32),
                        pltpu.VMEM((g, B_HEADS, CHUNK, HP), F32), pltpu.VMEM((g, B_HEADS, CHUNK, HP), F32),
                        pltpu.VMEM((g, B_HEADS, CHUNK, HP), F32)],
        compiler_params=_params(2),
        name="mlstm",
    )(t3(q), t3(k), t3(v), t3(so), t3(gcol), grow.reshape(n_tiles, n_chunks, 16, CHUNK), m0,
      c0.reshape(n_tiles, ns, B_HEADS, B_HEAD_DIM, B_HEAD_DIM), n0.reshape(n_tiles, ns, B_HEADS, B_HEAD_DIM), nw)
    return (yb.reshape(m, BW_P), c_new.reshape(nb, B_HEADS, B_HEAD_DIM, B_HEAD_DIM),
            n_new.reshape(nb, B_HEADS, B_HEAD_DIM), m_rows)


def _stack_heads(qc_ref, dtype):
    return jnp.concatenate([qc_ref[:, QC_SLOT * h:QC_SLOT * h + LATENT] for h in range(C_HEADS)],
                           axis=0).astype(dtype)


def _unstack_project(o, t, wuv_ref):
    wide = jnp.concatenate([o[h * t:(h + 1) * t] for h in range(C_HEADS)], axis=1)
    return _dot(wide.astype(BF16), wuv_ref[...])


def _softmax_step(s, pv, m_ref, acc_ref):
    reps = s.shape[1] // LANE
    m_prev = m_ref[...]
    m_new = jnp.maximum(m_prev, jnp.max(s, axis=-1, keepdims=True))
    p = jnp.exp(s - jnp.concatenate([m_new] * reps, axis=1))
    corr = jnp.exp(m_prev - m_new)
    acc_ref[...] = jnp.concatenate([corr, corr], axis=1) * acc_ref[...] + pv(p.astype(BF16))
    m_ref[...] = m_new


def _softmax_result(acc_ref):
    acc = acc_ref[...]
    return acc[:, :KV_LORA] / acc[:, KV_LORA:]


def _mla_prompt_kernel(qc_ref, lat_ref, wuv_ref, yc_ref, kb_ref, vb_ref, m_ref, acc_ref, *, tq, tk):
    qi = pl.program_id(1)

    @pl.when(qi == 0)
    def _cast_keys():
        lat = lat_ref[...]
        kb_ref[...] = lat.astype(BF16)
        vb_ref[:, 0:KV_LORA] = lat[:, 0:KV_LORA].astype(BF16)
        vb_ref[:, KV_LORA:2 * KV_LORA] = jnp.ones((lat.shape[0], KV_LORA), BF16)

    qst = _stack_heads(qc_ref, BF16)
    m_ref[...] = jnp.full(m_ref.shape, -jnp.inf, F32)
    acc_ref[...] = jnp.zeros(acc_ref.shape, F32)
    rows = C_HEADS * tq

    def block(j, masked):
        start = pl.multiple_of(j * tk, tk)
        s = _dot_nt(qst, kb_ref[pl.ds(start, tk), :])
        if masked:
            qpos = qi * tq + (lax.broadcasted_iota(jnp.int32, (rows, tk), 0) & (tq - 1))
            kpos = start + lax.broadcasted_iota(jnp.int32, (rows, tk), 1)
            s = jnp.where(kpos <= qpos, s, -jnp.inf)
        _softmax_step(s, lambda p: _dot(p, vb_ref[pl.ds(start, tk), :]), m_ref, acc_ref)

    n_full = (qi * tq) // tk
    lax.fori_loop(0, n_full, lambda j, c: (block(j, False), c)[1], 0)
    block(n_full, True)
    yc_ref[...] = _unstack_project(_softmax_result(acc_ref), tq, wuv_ref)


def _mla_prompt(qc, lat, wuv, *, nb, seq, tq, tk):
    m = qc.shape[0]
    n_q = seq // tq
    rows = C_HEADS * tq
    return pl.pallas_call(
        functools.partial(_mla_prompt_kernel, tq=tq, tk=tk),
        grid=(nb, n_q),
        in_specs=[pl.BlockSpec((tq, C_HEADS * QC_SLOT), lambda b, i: (b * n_q + i, 0)),
                  pl.BlockSpec((seq, LATENT), lambda b, i: (b, 0)),
                  _wspec((C_HEADS * KV_LORA, C_WIDTH))],
        out_specs=pl.BlockSpec((tq, C_WIDTH), lambda b, i: (b * n_q + i, 0)),
        out_shape=jax.ShapeDtypeStruct((m, C_WIDTH), F32),
        scratch_shapes=[pltpu.VMEM((seq, LATENT), BF16), pltpu.VMEM((seq, 2 * KV_LORA), BF16),
                        pltpu.VMEM((rows, LANE), F32), pltpu.VMEM((rows, 2 * KV_LORA), F32)],
        compiler_params=_params(2),
        name="mla_prompt",
    )(qc, lat, wuv)


def _mla_sample_kernel(pt_ref, qc_ref, latn_ref, wuv_ref, *rest, pages_per_step, t_new, seqs_per_step):
    n_pg = seqs_per_step * pages_per_step
    page_refs = rest[:n_pg]
    yc_ref, kt_ref, kn_ref, m_ref, acc_ref = rest[n_pg:]
    j = pl.program_id(1)
    rows = C_HEADS * t_new

    @pl.when(j == 0)
    def _init():
        m_ref[...] = jnp.full(m_ref.shape, -jnp.inf, F32)
        acc_ref[...] = jnp.zeros(acc_ref.shape, F32)

    qsts = []
    for r in range(seqs_per_step):
        rs = slice(r * t_new, (r + 1) * t_new)
        qst = jnp.concatenate([qc_ref[rs, QC_SLOT * h:QC_SLOT * h + LATENT] for h in range(C_HEADS)],
                              axis=0).astype(BF16)
        qsts.append(qst)
        for i in range(pages_per_step):
            kt_ref[r, :, i * PAGE_SIZE:(i + 1) * PAGE_SIZE] = page_refs[r * pages_per_step + i][...].astype(BF16)

        def pv_past(p, r=r):
            o = _dot_nt(p, kt_ref[r, 0:KV_LORA, :])
            return jnp.concatenate([o, jnp.broadcast_to(jnp.sum(p.astype(F32), axis=-1, keepdims=True),
                                                        (rows, KV_LORA))], axis=1)

        _softmax_step(_dot(qst, kt_ref[r]), pv_past, m_ref.at[r], acc_ref.at[r])

    @pl.when(j == pl.num_programs(1) - 1)
    def _new_rows_and_finish():
        tpos = lax.broadcasted_iota(jnp.int32, (rows, PAGE_SIZE), 0) & (t_new - 1)
        kidx = lax.broadcasted_iota(jnp.int32, (rows, PAGE_SIZE), 1)
        for r in range(seqs_per_step):
            kn_ref[r] = jnp.zeros(kn_ref.shape[1:], F32)
            kn_ref[r, 0:t_new, 0:LATENT] = latn_ref[r * t_new:(r + 1) * t_new, :]
            kn_ref[r, :, KV_LORA + LANE:] = jnp.ones((PAGE_SIZE, LANE), F32)
            kn = kn_ref[r].astype(BF16)
            s = jnp.where(kidx <= tpos, _dot_nt(qsts[r], kn[:, 0:LATENT]), -jnp.inf)
            vn = jnp.concatenate([kn[:, 0:KV_LORA], kn[:, KV_LORA + LANE:]], axis=1)
            _softmax_step(s, lambda p, vn=vn: _dot(p, vn), m_ref.at[r], acc_ref.at[r])
            yc_ref[r * t_new:(r + 1) * t_new, :] = _unstack_project(_softmax_result(acc_ref.at[r]), t_new, wuv_ref)


def _mla_sample(page_table, qc, lat_new, wuv, cache_t, *, layer, t_new, pages_per_step, seqs_per_step):
    m = qc.shape[0]
    nb, n_pages = page_table.shape
    n_steps = n_pages // pages_per_step
    rows = C_HEADS * t_new
    tr = seqs_per_step * t_new

    def page_spec(r, i):
        return pl.BlockSpec((None, None, LATENT, PAGE_SIZE),
                            lambda b, j, pt: (layer, pt[b * seqs_per_step + r, j * pages_per_step + i], 0, 0))

    grid_spec = pltpu.PrefetchScalarGridSpec(
        num_scalar_prefetch=1,
        grid=(nb // seqs_per_step, n_steps),
        in_specs=[pl.BlockSpec((tr, C_HEADS * QC_SLOT), lambda b, j, pt: (b, 0)),
                  pl.BlockSpec((tr, LATENT), lambda b, j, pt: (b, 0)),
                  pl.BlockSpec((C_HEADS * KV_LORA, C_WIDTH), lambda b, j, pt: (0, 0))]
                 + [page_spec(r, i) for r in range(seqs_per_step) for i in range(pages_per_step)],
        out_specs=pl.BlockSpec((tr, C_WIDTH), lambda b, j, pt: (b, 0)),
        scratch_shapes=[pltpu.VMEM((seqs_per_step, LATENT, pages_per_step * PAGE_SIZE), BF16),
                        pltpu.VMEM((seqs_per_step, PAGE_SIZE, KV_LORA + 2 * LANE), F32),
                        pltpu.VMEM((seqs_per_step, rows, LANE), F32),
                        pltpu.VMEM((seqs_per_step, rows, 2 * KV_LORA), F32)])
    return pl.pallas_call(
        functools.partial(_mla_sample_kernel, pages_per_step=pages_per_step, t_new=t_new,
                          seqs_per_step=seqs_per_step),
        grid_spec=grid_spec,
        out_shape=jax.ShapeDtypeStruct((m, C_WIDTH), F32),
        compiler_params=_params(2),
        name="mla_sample",
    )(page_table, qc, lat_new, wuv, *([cache_t] * (seqs_per_step * pages_per_step)))


def _outproj_kernel(x_ref, ya_ref, yb_ref, yc_ref, wo_ref, nxw_ref, wq_ref, xo_ref, q_ref):
    y = (_dot(ya_ref[...].astype(BF16), wo_ref[0:A_WIDTH, :])
         + _dot(yb_ref[...].astype(BF16), wo_ref[A_WIDTH:A_WIDTH + BW_P, :])
         + _dot(yc_ref[...].astype(BF16), wo_ref[A_WIDTH + BW_P:N_MIX_P, :]))
    xn = x_ref[...] + y
    xo_ref[...] = xn
    hq = _rms(xn, nxw_ref[...], D_MODEL).astype(BF16)
    q_ref[...] = (_dot(hq, wq_ref[...]) * XA_SCALE).astype(q_ref.dtype)


def _outproj(x, ya, yb, yc, wo, nxw, wq, *, tm, q_dtype):
    m = x.shape[0]
    row = lambda w: pl.BlockSpec((tm, w), lambda i: (i, 0))
    return pl.pallas_call(
        _outproj_kernel,
        grid=(m // tm,),
        in_specs=[row(D_MODEL), row(A_WIDTH), row(BW_P), row(C_WIDTH), _wspec((N_MIX_P, D_MODEL)),
                  _wspec((1, D_MODEL)), _wspec((D_MODEL, D_MODEL))],
        out_specs=(row(D_MODEL), row(D_MODEL)),
        out_shape=(jax.ShapeDtypeStruct((m, D_MODEL), F32), jax.ShapeDtypeStruct((m, D_MODEL), q_dtype)),
        compiler_params=_params(1),
        name="outproj",
    )(x, ya, yb, yc, wo, nxw, wq)


def _memkv_kernel(mem_ref, nw_ref, wk_ref, wv_ref, k_ref, v_ref, kb_ref, vb_ref):
    mn = _rms(mem_ref[...], nw_ref[...], D_MODEL).astype(BF16)
    k = _dot(mn, wk_ref[...])
    v = _dot(mn, wv_ref[...])
    kb_ref[...] = k.astype(BF16)
    vb_ref[...] = v.astype(BF16)
    for h in range(X_HEADS):
        hs = slice(h * X_HEAD_DIM, (h + 1) * X_HEAD_DIM)
        k_ref[:, h, :] = k[:, hs]
        v_ref[:, h, :] = v[:, hs]


def _memkv(mem, nw, wk, wv, *, tm):
    m = mem.shape[0]
    row = pl.BlockSpec((tm, D_MODEL), lambda i: (i, 0))
    out = pl.BlockSpec((tm, X_HEADS, X_HEAD_DIM), lambda i: (i, 0, 0))
    return pl.pallas_call(
        _memkv_kernel,
        grid=(m // tm,),
        in_specs=[row, _wspec((1, D_MODEL)), _wspec((D_MODEL, D_MODEL)), _wspec((D_MODEL, D_MODEL))],
        out_specs=(out, out, row, row),
        out_shape=(jax.ShapeDtypeStruct((m, X_HEADS, X_HEAD_DIM), F32),) * 2
                  + (jax.ShapeDtypeStruct((m, D_MODEL), BF16),) * 2,
        compiler_params=_params(1),
        name="memkv",
    )(mem, nw, wk, wv)


def _softmax_rows(s):
    p = jnp.exp(s - jnp.max(s, axis=-1, keepdims=True))
    return p / jnp.sum(p, axis=-1, keepdims=True)


def _xattn_prompt_kernel(x_ref, q_ref, k_ref, v_ref, wo_ref, xo_ref, o_ref):
    for h in range(X_HEADS):
        hs = slice(h * X_HEAD_DIM, (h + 1) * X_HEAD_DIM)
        p = _softmax_rows(_dot_nt(q_ref[:, hs], k_ref[:, hs]))
        o_ref[:, hs] = _dot(p.astype(BF16), v_ref[:, hs])
    xo_ref[...] = x_ref[...] + _dot(o_ref[...].astype(BF16), wo_ref[...])


def _xattn_prompt(x, q, k, v, wo, *, tm, tiles_per_seq):
    m = x.shape[0]
    row = pl.BlockSpec((tm, D_MODEL), lambda i: (i, 0))
    kv = pl.BlockSpec((MEM_LEN, D_MODEL), lambda i: (i // tiles_per_seq, 0))
    return pl.pallas_call(
        _xattn_prompt_kernel,
        grid=(m // tm,),
        in_specs=[row, row, kv, kv, _wspec((D_MODEL, D_MODEL))],
        out_specs=row,
        out_shape=jax.ShapeDtypeStruct((m, D_MODEL), F32),
        scratch_shapes=[pltpu.VMEM((tm, D_MODEL), F32)],
        compiler_params=_params(1),
        name="xattn_prompt",
    )(x, q, k, v, wo)


def _xattn_sample_kernel(x_ref, q_ref, k_ref, v_ref, wo_ref, xo_ref, o_ref, *, n_seq, t_rows):
    rows = X_HEADS * t_rows
    nk = MEM_LEN * X_HEADS
    r = lax.broadcasted_iota(jnp.int32, (rows, nk), 0)
    c = lax.broadcasted_iota(jnp.int32, (rows, nk), 1)
    keep = (c & (X_HEADS - 1)) == (r >> (t_rows.bit_length() - 1))
    for g in range(n_seq):
        rs = slice(g * t_rows, (g + 1) * t_rows)
        qst = jnp.concatenate([q_ref[rs, h * X_HEAD_DIM:(h + 1) * X_HEAD_DIM] for h in range(X_HEADS)], axis=0)
        k2 = k_ref[g].reshape(nk, X_HEAD_DIM).astype(BF16)
        v2 = v_ref[g].reshape(nk, X_HEAD_DIM).astype(BF16)
        p = _softmax_rows(jnp.where(keep, _dot_nt(qst.astype(BF16), k2), -jnp.inf))
        o = _dot(p.astype(BF16), v2)
        for h in range(X_HEADS):
            o_ref[rs, h * X_HEAD_DIM:(h + 1) * X_HEAD_DIM] = o[h * t_rows:(h + 1) * t_rows]
    xo_ref[...] = x_ref[...] + _dot(o_ref[...].astype(BF16), wo_ref[...])


def _xattn_sample(x, q, k, v, wo, *, layer, n_seq, t_rows):
    m = x.shape[0]
    r = n_seq * t_rows
    row = pl.BlockSpec((r, D_MODEL), lambda i: (i, 0))
    kv = pl.BlockSpec((None, n_seq, MEM_LEN, X_HEADS, X_HEAD_DIM), lambda i: (layer, i, 0, 0, 0))
    return pl.pallas_call(
        functools.partial(_xattn_sample_kernel, n_seq=n_seq, t_rows=t_rows),
        grid=(m // r,),
        in_specs=[row, row, kv, kv, _wspec((D_MODEL, D_MODEL))],
        out_specs=row,
        out_shape=jax.ShapeDtypeStruct((m, D_MODEL), F32),
        scratch_shapes=[pltpu.VMEM((r, D_MODEL), F32)],
        compiler_params=_params(1),
        name="xattn_sample",
    )(x, q, k, v, wo)


def _ffn_kernel(x_ref, nw_ref, wg_ref, wu_ref, wd_ref, cw_ref, cb_ref, p0_ref, p1_ref, fnw_ref, *rest,
                seq_rows, tiles_per_seq, final_norm):
    if final_norm:
        xo_ref, gtail_ref, y_ref, carry_ref, act_ref = rest
    else:
        xo_ref, gtail_ref, carry_ref, act_ref = rest
    tm = x_ref.shape[0]
    x = x_ref[...]
    xn = _rms(x, nw_ref[...], D_MODEL).astype(BF16)
    rid = lax.broadcasted_iota(jnp.int32, (tm, 1), 0)
    if seq_rows:
        pos = rid & (seq_rows - 1)
    else:
        pos = rid

        @pl.when(pl.program_id(0) % tiles_per_seq == 0)
        def _seq_start():
            carry_ref[0:1, :] = p0_ref[...]
            carry_ref[1:2, :] = p1_ref[...]

    for c in range(N_FF_CHUNKS):
        cs = slice(c * FF_CHUNK, (c + 1) * FF_CHUNK)
        g = _dot(xn, wg_ref[:, cs])
        u = _dot(xn, wu_ref[:, cs])
        if seq_rows:
            prev0, prev1 = p0_ref[:, cs], p1_ref[:, cs]
            gtail_ref[:, cs] = g
        else:
            prev0, prev1 = carry_ref[0:1, cs], carry_ref[1:2, cs]
            gtail_ref[:, cs] = g[tm - 2:tm, :]
            carry_ref[0:2, cs] = g[tm - 2:tm, :]
        g1 = jnp.where(pos == 0, prev1, pltpu.roll(g, 1, 0))
        g2 = jnp.where(pos == 0, prev0, jnp.where(pos == 1, prev1, pltpu.roll(g, 2, 0)))
        y = cb_ref[:, cs] + g2 * cw_ref[0:1, cs] + g1 * cw_ref[1:2, cs] + g * cw_ref[2:3, cs]
        act_ref[:, cs] = (_gelu(y) * u).astype(BF16)
    xo = x + _dot(act_ref[...], wd_ref[...])
    xo_ref[...] = xo
    if final_norm:
        y_ref[...] = _rms(xo, fnw_ref[...], D_MODEL)


def _ffn(x, nw, wg, wu, wd, cw, cb, p0, p1, fnw, *, tm, seq_rows, tiles_per_seq, final_norm):
    m = x.shape[0]
    n_tiles = m // tm
    row = pl.BlockSpec((tm, D_MODEL), lambda i: (i, 0))
    if seq_rows:
        pspec = pl.BlockSpec((tm, D_FF), lambda i: (i, 0))
        gspec = pspec
        gshape = jax.ShapeDtypeStruct((m, D_FF), F32)
    else:
        pspec = pl.BlockSpec((None, 1, D_FF), lambda i: (i // tiles_per_seq, 0, 0))
        gspec = pl.BlockSpec((None, 2, D_FF), lambda i: (i, 0, 0))
        gshape = jax.ShapeDtypeStruct((n_tiles, 2, D_FF), F32)
    return pl.pallas_call(
        functools.partial(_ffn_kernel, seq_rows=seq_rows, tiles_per_seq=tiles_per_seq, final_norm=final_norm),
        grid=(n_tiles,),
        in_specs=[row, _wspec((1, D_MODEL)), _wspec((D_MODEL, D_FF)), _wspec((D_MODEL, D_FF)),
                  _wspec((D_FF, D_MODEL)), _wspec((CONV_W, D_FF)), _wspec((1, D_FF)),
                  pspec, pspec, _wspec((1, D_MODEL))],
        out_specs=(row, gspec) + ((row,) if final_norm else ()),
        out_shape=(jax.ShapeDtypeStruct((m, D_MODEL), F32), gshape)
                  + ((jax.ShapeDtypeStruct((m, D_MODEL), F32),) if final_norm else ()),
        scratch_shapes=[pltpu.VMEM((8, D_FF), F32), pltpu.VMEM((tm, D_FF), BF16)],
        compiler_params=_params(1),
        name="ffn",
    )(x, nw, wg, wu, wd, cw, cb, p0, p1, fnw)


def _pad_heads_idx(start):
    idx = np.full((BW_P,), -1, np.int64)
    for h in range(B_HEADS):
        idx[h * HP:h * HP + B_HEAD_DIM] = start + h * B_HEAD_DIM + np.arange(B_HEAD_DIM)
    return idx


def _take_cols(w, idx, axis=-1):
    axis = axis % w.ndim
    pieces, i, n = [], 0, len(idx)
    while i < n:
        j = i + 1
        if idx[i] < 0:
            while j < n and idx[j] < 0:
                j += 1
            shape = w.shape[:axis] + (j - i,) + w.shape[axis + 1:]
            pieces.append(jnp.zeros(shape, w.dtype))
        else:
            while j < n and idx[j] == idx[j - 1] + 1:
                j += 1
            pieces.append(lax.slice_in_dim(w, int(idx[i]), int(idx[i]) + (j - i), axis=axis))
        i = j
    return jnp.concatenate(pieces, axis=axis)


def _win_layout():
    half = C_ROPE // 2
    base = {'au': 0, 'av': 256, 'bq': 512, 'bk': 896, 'bv': 1280, 'bo': 1664, 'bi': 2048, 'bf': 2052,
            'cq': 2056, 'ckv': 2248, 'ckr': 2376}
    idx = np.full((N_IN,), -1, np.int64)
    idx[S_AU:S_AU + 256] = base['au'] + np.arange(256)
    idx[S_AV:S_AV + 256] = base['av'] + np.arange(256)
    for s, name in ((S_BQ, 'bq'), (S_BK, 'bk'), (S_BV, 'bv'), (S_BO, 'bo')):
        idx[s:s + BW_P] = _pad_heads_idx(base[name])
    idx[S_CQ:S_CQ + Q_LORA] = base['cq'] + np.arange(Q_LORA)
    idx[S_CKV:S_CKV + KV_LORA] = base['ckv'] + np.arange(KV_LORA)
    idx[S_R1:S_R1 + C_ROPE] = base['ckr'] + np.arange(C_ROPE)
    idx[S_R2:S_R2 + half] = base['ckr'] + half + np.arange(half)
    idx[S_R2 + half:S_R2 + C_ROPE] = base['ckr'] + np.arange(half)
    idx[S_G:S_G + B_HEADS] = base['bi'] + np.arange(B_HEADS)
    idx[S_G + B_HEADS:S_G + 2 * B_HEADS] = base['bf'] + np.arange(B_HEADS)
    return idx


def _wuq_layout():
    half = C_ROPE // 2
    per = C_NOPE + C_ROPE
    idx = np.full((N_UQ,), -1, np.int64)
    for h in range(C_HEADS):
        idx[UQ_NOPE + h * C_NOPE:UQ_NOPE + (h + 1) * C_NOPE] = h * per + np.arange(C_NOPE)
        idx[UQ_RA + h * LANE:UQ_RA + h * LANE + C_ROPE] = h * per + C_NOPE + np.arange(C_ROPE)
        idx[UQ_RB + h * LANE:UQ_RB + h * LANE + half] = h * per + C_NOPE + half + np.arange(half)
        idx[UQ_RB + h * LANE + half:UQ_RB + h * LANE + C_ROPE] = h * per + C_NOPE + np.arange(half)
    return idx


def _rope_tables(pos, reps):
    half = C_ROPE // 2
    inv = ROPE_THETA ** (-jnp.arange(half, dtype=F32) / half)
    ang = pos.astype(F32)[:, None] * inv[None, :]
    cos, sin = jnp.cos(ang), jnp.sin(ang)
    z = jnp.zeros((pos.shape[0], LANE - C_ROPE), F32)
    ct = jnp.concatenate([cos, cos, z], axis=1)
    st = jnp.concatenate([-sin, sin, z], axis=1)
    return jnp.tile(ct, (reps, 1)), jnp.tile(st, (reps, 1))


def _pad_lanes(v, idx):
    return _take_cols(v[None, :], idx)


def _layer_weights(l, w):
    hp_idx = _pad_heads_idx(0)
    out = {}
    out['nw_mix'] = w['norm_mix_w'][l][None, :]
    out['w1'] = _take_cols(w['w_in'][l], _win_layout()).astype(BF16)
    out['qnw'] = _pad_lanes(w['mla_qnorm_w'][l], np.concatenate([np.arange(Q_LORA), np.full(CQ_P - Q_LORA, -1)]))
    out['kvnw'] = w['mla_kvnorm_w'][l][None, :]
    wuq = w['mla_w_uq'][l].reshape(Q_LORA, C_HEADS * (C_NOPE + C_ROPE))
    wuq = _take_cols(wuq, _wuq_layout())
    out['wuq'] = jnp.concatenate([wuq, jnp.zeros((CQ_P - Q_LORA, N_UQ), F32)], axis=0).astype(BF16)
    wuk = jnp.transpose(w['mla_w_uk'][l], (1, 2, 0))
    eye = jnp.eye(C_HEADS, dtype=F32)
    out['wuk'] = jnp.einsum('hdc,hg->hdgc', wuk, eye).reshape(C_HEADS * C_NOPE, C_HEADS * KV_LORA).astype(BF16)
    wuv = jnp.transpose(w['mla_w_uv'][l], (1, 0, 2))
    out['wuv'] = jnp.einsum('hce,hg->hcge', wuv, eye).reshape(C_HEADS * KV_LORA, C_WIDTH).astype(BF16)
    gb = jnp.concatenate([w['mlstm_bi'][l], w['mlstm_bf'][l]])
    out['gb'] = _pad_lanes(gb, np.concatenate([np.arange(2 * B_HEADS), np.full(LANE - 2 * B_HEADS, -1)]))
    out['mnw'] = _pad_lanes(w['mlstm_norm_w'][l], hp_idx)
    wo = w['w_out'][l]
    rows = np.concatenate([np.arange(A_WIDTH), np.where(hp_idx < 0, -1, hp_idx + A_WIDTH),
                           A_WIDTH + B_WIDTH + np.arange(C_WIDTH)])
    out['wo'] = _take_cols(wo, rows, axis=0).astype(BF16)
    out['nw_x'] = w['norm_x_w'][l][None, :]
    out['nw_mem'] = w['norm_mem_w'][l][None, :]
    out['xwq'] = w['xa_wq'][l].reshape(D_MODEL, D_MODEL).astype(BF16)
    out['xwk'] = w['xa_wk'][l].reshape(D_MODEL, D_MODEL).astype(BF16)
    out['xwv'] = w['xa_wv'][l].reshape(D_MODEL, D_MODEL).astype(BF16)
    out['xwo'] = w['xa_wo'][l].reshape(D_MODEL, D_MODEL).astype(BF16)
    out['nw_ffn'] = w['norm_ffn_w'][l][None, :]
    out['wg'] = w['ffn_wg'][l].astype(BF16)
    out['wu'] = w['ffn_wu'][l].astype(BF16)
    out['wd'] = w['ffn_wd'][l].astype(BF16)
    out['cw'] = w['ffn_conv_w'][l]
    out['cb'] = w['ffn_conv_b'][l][None, :]
    return out


def _chunk_mixers(ws, cb, t_rows):
    reps = CHUNK // t_rows
    wt = ws[:, :t_rows, :t_rows]
    if reps > 1:
        wt = jnp.einsum('ab,hts->hatbs', jnp.eye(reps, dtype=F32), wt).reshape(A_HEADS, CHUNK, CHUNK)
    bias = jnp.tile(jnp.repeat(jnp.transpose(cb[:, :t_rows]), A_HEAD_DIM, axis=1), (reps, 1))
    return wt, bias


def _mixer_block(x, lw, *, mlstm_state, pos_tables, chunk_w, tm, act_dtype, ns, n_tiles, n_chunks, mla_fn):
    m = x.shape[0]
    cos_t, sin_t = pos_tables
    gu, gv, q, k, v, so, qc, lat, gate = _inproj(x, lw['nw_mix'], lw['w1'], lw['qnw'], lw['kvnw'], lw['wuq'],
                                                 lw['wuk'], lw['gb'], cos_t, sin_t, tm=tm, act_dtype=act_dtype)
    ya = _chunkmlp(gu, gv, chunk_w[0], chunk_w[1], tm=tm)
    grow = jnp.transpose(gate[:, :16].reshape(m // CHUNK, CHUNK, 16), (0, 2, 1))
    c0, n0, m0 = mlstm_state
    yb, c_new, n_new, m_rows = _mlstm(q, k, v, so, gate, grow, m0, c0, n0, lw['mnw'], ns=ns, n_tiles=n_tiles,
                                      n_chunks=n_chunks, g_tiles=1)
    yc = mla_fn(qc, lat)
    return (ya, yb, yc), (c_new, n_new, m_rows), lat, gv


def kernel(x_prompt, x_sample, mem_prompt, cache_mla, page_table, cache_mem_k, cache_mem_v, state_mlstm_C, state_mlstm_n, state_mlstm_m, state_ffn_conv, norm_mix_w, w_in, chunk_ws, chunk_b, mlstm_bi, mlstm_bf, mlstm_norm_w, mla_qnorm_w, mla_kvnorm_w, mla_w_uq, mla_w_uk, mla_w_uv, w_out, norm_x_w, norm_mem_w, xa_wq, xa_wk, xa_wv, xa_wo, norm_ffn_w, ffn_wg, ffn_wu, ffn_conv_w, ffn_conv_b, ffn_wd, norm_final_w):
    w = dict(norm_mix_w=norm_mix_w, w_in=w_in, mlstm_bi=mlstm_bi, mlstm_bf=mlstm_bf, mlstm_norm_w=mlstm_norm_w,
             mla_qnorm_w=mla_qnorm_w, mla_kvnorm_w=mla_kvnorm_w, mla_w_uq=mla_w_uq, mla_w_uk=mla_w_uk,
             mla_w_uv=mla_w_uv, w_out=w_out, norm_x_w=norm_x_w, norm_mem_w=norm_mem_w, xa_wq=xa_wq, xa_wk=xa_wk,
             xa_wv=xa_wv, xa_wo=xa_wo, norm_ffn_w=norm_ffn_w, ffn_wg=ffn_wg, ffn_wu=ffn_wu,
             ffn_conv_w=ffn_conv_w, ffn_conv_b=ffn_conv_b, ffn_wd=ffn_wd)
    depth = w_in.shape[0]
    nbp, S, _ = x_prompt.shape
    nbs, T, _ = x_sample.shape
    n_pages = page_table.shape[1]
    past_len = n_pages * PAGE_SIZE
    mp, ms = nbp * S, nbs * T
    assert S % CHUNK == 0 and CHUNK % T == 0 and T % 8 == 0 and ms % CHUNK == 0

    tm_p = min(512, S)
    tm_s = min(512, ms)
    tq = min(256, S)
    tk = min(512, S)
    ns_s = CHUNK // T
    pages_per_step = min(32, n_pages)
    xg = 4 if nbs % 4 == 0 else 1
    tm_ffn_p = min(512, S)
    tm_ffn_s = min(256, ms)

    pos_p = _rope_tables(jnp.arange(S), 1)
    pos_s = _rope_tables(past_len + jnp.arange(T), tm_s // T)
    fnw = norm_final_w[None, :]

    xp = x_prompt.reshape(mp, D_MODEL)
    xs = x_sample.reshape(ms, D_MODEL)
    mem = mem_prompt.reshape(nbp * MEM_LEN, D_MODEL)

    zero_c = jnp.zeros((nbp, B_HEADS, B_HEAD_DIM, B_HEAD_DIM), F32)
    zero_n = jnp.zeros((nbp, B_HEADS, B_HEAD_DIM), F32)
    zero_m = jnp.zeros((nbp, CHUNK, LANE), F32)
    zero_conv = jnp.zeros((nbp, 1, D_FF), F32)
    cache_t = jnp.swapaxes(cache_mla, 2, 3)

    outs = {k: [] for k in ('lat_p', 'lat_s', 'mk', 'mv', 'cp', 'np', 'mp', 'cs', 'ns', 'ms', 'conv_p', 'conv_s',
                            'chunkv')}
    yp = ys = None
    for l in range(depth):
        lw = _layer_weights(l, w)
        last = l == depth - 1

        mla_p = functools.partial(_mla_prompt, wuv=lw['wuv'], nb=nbp, seq=S, tq=tq, tk=tk)
        ymix, st, lat, _ = _mixer_block(
            xp, lw, mlstm_state=(zero_c, zero_n, zero_m), pos_tables=pos_p,
            chunk_w=_chunk_mixers(chunk_ws[l], chunk_b[l], CHUNK), tm=tm_p, act_dtype=BF16, ns=1, n_tiles=nbp,
            n_chunks=S // CHUNK, mla_fn=lambda qc, lt: mla_p(qc, lt))
        xp, qx_p = _outproj(xp, *ymix, lw['wo'], lw['nw_x'], lw['xwq'], tm=tm_p, q_dtype=BF16)
        outs['lat_p'].append(lat.reshape(nbp, S, LATENT))
        outs['cp'].append(st[0])
        outs['np'].append(st[1])
        outs['mp'].append(st[2][:, CHUNK - 1, :B_HEADS])

        m0_rows = jnp.concatenate([jnp.repeat(state_mlstm_m[l], T, axis=0),
                                   jnp.zeros((ms, LANE - B_HEADS), F32)], axis=1).reshape(ms // CHUNK, CHUNK, LANE)
        mla_s = functools.partial(_mla_sample, page_table, wuv=lw['wuv'], cache_t=cache_t, layer=l, t_new=T,
                                  pages_per_step=pages_per_step, seqs_per_step=2 if nbs % 2 == 0 else 1)
        ymix, st, lat, gv = _mixer_block(
            xs, lw, mlstm_state=(state_mlstm_C[l], state_mlstm_n[l], m0_rows), pos_tables=pos_s,
            chunk_w=_chunk_mixers(chunk_ws[l], chunk_b[l], T), tm=tm_s, act_dtype=F32, ns=ns_s,
            n_tiles=ms // CHUNK, n_chunks=1, mla_fn=lambda qc, lt: mla_s(qc, lt))
        xs, qx_s = _outproj(xs, *ymix, lw['wo'], lw['nw_x'], lw['xwq'], tm=tm_s, q_dtype=F32)
        outs['lat_s'].append(lat.reshape(nbs, T, LATENT))
        outs['cs'].append(st[0])
        outs['ns'].append(st[1])
        outs['ms'].append(st[2].reshape(nbs, T, LANE)[:, T - 1, :B_HEADS])
        outs['chunkv'].append(gv.reshape(nbs, T, A_WIDTH))

        mk, mv, mk_b, mv_b = _memkv(mem, lw['nw_mem'], lw['xwk'], lw['xwv'], tm=min(512, nbp * MEM_LEN))
        outs['mk'].append(mk.reshape(nbp, MEM_LEN, X_HEADS, X_HEAD_DIM))
        outs['mv'].append(mv.reshape(nbp, MEM_LEN, X_HEADS, X_HEAD_DIM))
        xp = _xattn_prompt(xp, qx_p, mk_b, mv_b, lw['xwo'], tm=tm_p, tiles_per_seq=S // tm_p)
        xs = _xattn_sample(xs, qx_s, cache_mem_k, cache_mem_v, lw['xwo'], layer=l, n_seq=xg, t_rows=T)

        res_p = _ffn(xp, lw['nw_ffn'], lw['wg'], lw['wu'], lw['wd'], lw['cw'], lw['cb'], zero_conv, zero_conv, fnw,
                     tm=tm_ffn_p, seq_rows=0, tiles_per_seq=S // tm_ffn_p, final_norm=last)
        buf = state_ffn_conv[l]
        p0 = jnp.repeat(buf[:, 0, :], T, axis=0)
        p1 = jnp.repeat(buf[:, 1, :], T, axis=0)
        res_s = _ffn(xs, lw['nw_ffn'], lw['wg'], lw['wu'], lw['wd'], lw['cw'], lw['cb'], p0, p1, fnw,
                     tm=tm_ffn_s, seq_rows=T, tiles_per_seq=1, final_norm=last)
        xp, xs = res_p[0], res_s[0]
        outs['conv_p'].append(res_p[1][S // tm_ffn_p - 1::S // tm_ffn_p])
        outs['conv_s'].append(res_s[1].reshape(nbs, T, D_FF)[:, T - (CONV_W - 1):, :])
        if last:
            yp, ys = res_p[2], res_s[2]

    st = lambda k: jnp.stack(outs[k])
    return (yp.reshape(nbp, S, D_MODEL), ys.reshape(nbs, T, D_MODEL),
            st('lat_p'), st('lat_s'), st('mk'), st('mv'),
            st('cp'), st('np'), st('mp'), st('cs'), st('ns'), st('ms'),
            st('conv_p'), st('conv_s'), st('chunkv'))
```

```python
import functools
import math

import numpy as np
import jax
import jax.numpy as jnp
from jax import lax
from jax.experimental import pallas as pl
from jax.experimental.pallas import tpu as pltpu

F32 = jnp.float32
BF16 = jnp.bfloat16

D_MODEL = 1024
PAGE_SIZE = 128
A_HEADS = 4
A_WIDTH = 256
A_HEAD_DIM = 64
CHUNK = 128
B_HEADS = 4
B_WIDTH = 384
B_HEAD_DIM = 96
MLSTM_KSCALE = B_HEAD_DIM ** -0.5
C_WIDTH = 384
C_HEADS = 6
C_V_DIM = 64
C_NOPE = 64
C_ROPE = 32
Q_LORA = 192
KV_LORA = 128
LATENT = KV_LORA + C_ROPE
MLA_SCALE = (C_NOPE + C_ROPE) ** -0.5
ROPE_THETA = 10000.0
MEM_LEN = 256
X_HEADS = 4
X_HEAD_DIM = 256
XA_SCALE = X_HEAD_DIM ** -0.5
D_FF = 2816
CONV_W = 3
EPS = 1e-6

LANE = 128
HP = 128
BW_P = B_HEADS * HP
FF_CHUNK = 256
N_FF_CHUNKS = D_FF // FF_CHUNK
VMEM_LIMIT = 48 * 1024 * 1024

S_AU, S_AV, S_BQ, S_BK, S_BV, S_BO = 0, 256, 512, 1024, 1536, 2048
S_CQ, S_CKV, S_R1, S_R2, S_G = 2560, 2816, 2944, 3072, 3200
N_IN = 3328
CQ_P = 256
UQ_NOPE, UQ_RA, UQ_RB, N_UQ = 0, 384, 1152, 1920
QC_SLOT = 256
N_MIX_P = A_WIDTH + BW_P + C_WIDTH


def _rms(x, w, n):
    ms = jnp.sum(x * x, axis=-1, keepdims=True) * (1.0 / n)
    return x * lax.rsqrt(ms + EPS) * w


def _gelu(x):
    c = math.sqrt(2.0 / math.pi)
    return x * (0.5 * (1.0 + jnp.tanh(c * (x + 0.044715 * (x * x * x)))))


def _sigmoid(x):
    return 1.0 / (1.0 + jnp.exp(-x))


def _log_sigmoid(x):
    return jnp.minimum(x, 0.0) - jnp.log1p(jnp.exp(-jnp.abs(x)))


def _dot(a, b):
    return jnp.dot(a, b, preferred_element_type=F32)


def _dot_nt(a, b):
    return lax.dot_general(a, b, (((1,), (1,)), ((), ())), preferred_element_type=F32)


def _dot_tn(a, b):
    return lax.dot_general(a, b, (((0,), (0,)), ((), ())), preferred_element_type=F32)


def _split3(a):
    a1 = a.astype(BF16)
    r1 = a - a1.astype(F32)
    a2 = r1.astype(BF16)
    a3 = (r1 - a2.astype(F32)).astype(BF16)
    return a1, a2, a3


def _wspec(shape):
    nd = len(shape)
    return pl.BlockSpec(shape, lambda *_: (0,) * nd, pipeline_mode=pl.Buffered(1))


def _params(n_axes):
    return pltpu.CompilerParams(dimension_semantics=("arbitrary",) * n_axes,
                                vmem_limit_bytes=VMEM_LIMIT)


def _inproj_kernel(x_ref, nw_ref, w1_ref, qnw_ref, kvnw_ref, wuq_ref, wuk_ref, gb_ref, cos_ref, sin_ref,
                   gu_ref, gv_ref, q_ref, k_ref, v_ref, so_ref, qc_ref, lat_ref, gate_ref):
    xn = _rms(x_ref[...], nw_ref[...], D_MODEL).astype(BF16)

    def seg(a, b):
        return _dot(xn, w1_ref[:, a:b])

    gu_ref[...] = _gelu(seg(S_AU, S_AV))
    gv_ref[...] = _gelu(seg(S_AV, S_BQ))
    q_ref[...] = seg(S_BQ, S_BK).astype(q_ref.dtype)
    k_ref[...] = (seg(S_BK, S_BV) * MLSTM_KSCALE).astype(k_ref.dtype)
    v_ref[...] = seg(S_BV, S_BO).astype(v_ref.dtype)
    so_ref[...] = _sigmoid(seg(S_BO, S_CQ))

    c_q = _rms(seg(S_CQ, S_CKV), qnw_ref[...], Q_LORA).astype(BF16)
    q2 = _dot(c_q, wuq_ref[...])
    qlat = _dot(q2[:, UQ_NOPE:UQ_RA].astype(BF16), wuk_ref[...])
    cos = cos_ref[...]
    sin = sin_ref[...]
    for h in range(C_HEADS):
        ra = q2[:, UQ_RA + LANE * h:UQ_RA + LANE * (h + 1)]
        rb = q2[:, UQ_RB + LANE * h:UQ_RB + LANE * (h + 1)]
        qr = ra * cos + rb * sin
        qc_ref[:, QC_SLOT * h:QC_SLOT * h + LANE] = (qlat[:, LANE * h:LANE * (h + 1)] * MLA_SCALE).astype(qc_ref.dtype)
        qc_ref[:, QC_SLOT * h + LANE:QC_SLOT * (h + 1)] = (qr * MLA_SCALE).astype(qc_ref.dtype)

    lat_ref[:, 0:KV_LORA] = _rms(seg(S_CKV, S_R1), kvnw_ref[...], KV_LORA)
    kr = seg(S_R1, S_R2) * cos + seg(S_R2, S_G) * sin
    lat_ref[:, KV_LORA:LATENT] = kr[:, :C_ROPE]

    gt = seg(S_G, N_IN) + gb_ref[...]
    lane = lax.broadcasted_iota(jnp.int32, gt.shape, 1)
    gate_ref[...] = jnp.where(lane < B_HEADS, gt, jnp.where(lane < 2 * B_HEADS, _log_sigmoid(gt), 0.0))


def _inproj(x, nw, w1, qnw, kvnw, wuq, wuk, gb, cos_t, sin_t, *, tm, act_dtype):
    m = x.shape[0]
    n_pos_blocks = cos_t.shape[0] // tm
    row = lambda w: pl.BlockSpec((tm, w), lambda i: (i, 0))
    pos = pl.BlockSpec((tm, LANE), lambda i: (i % n_pos_blocks, 0))
    out_shapes = (
        jax.ShapeDtypeStruct((m, A_WIDTH), F32), jax.ShapeDtypeStruct((m, A_WIDTH), F32),
        jax.ShapeDtypeStruct((m, BW_P), act_dtype), jax.ShapeDtypeStruct((m, BW_P), act_dtype),
        jax.ShapeDtypeStruct((m, BW_P), act_dtype), jax.ShapeDtypeStruct((m, BW_P), F32),
        jax.ShapeDtypeStruct((m, C_HEADS * QC_SLOT), act_dtype), jax.ShapeDtypeStruct((m, LATENT), F32),
        jax.ShapeDtypeStruct((m, LANE), F32))
    return pl.pallas_call(
        _inproj_kernel,
        grid=(m // tm,),
        in_specs=[row(D_MODEL), _wspec((1, D_MODEL)), _wspec((D_MODEL, N_IN)), _wspec((1, CQ_P)),
                  _wspec((1, KV_LORA)), _wspec((CQ_P, N_UQ)), _wspec((C_HEADS * C_NOPE, C_HEADS * LANE)),
                  _wspec((1, LANE)), pos, pos],
        out_specs=(row(A_WIDTH), row(A_WIDTH), row(BW_P), row(BW_P), row(BW_P), row(BW_P),
                   row(C_HEADS * QC_SLOT), row(LATENT), row(LANE)),
        out_shape=out_shapes,
        compiler_params=_params(1),
        name="inproj",
    )(x, nw, w1, qnw, kvnw, wuq, wuk, gb, cos_t, sin_t)


def _chunkmlp_kernel(gu_ref, gv_ref, ws_ref, bias_ref, ya_ref, *, n_chunks):
    r = lax.broadcasted_iota(jnp.int32, (CHUNK, CHUNK), 0)
    c = lax.broadcasted_iota(jnp.int32, (CHUNK, CHUNK), 1)
    wsm = [jnp.where(r >= c, ws_ref[h], 0.0).astype(BF16) for h in range(A_HEADS)]
    head_of_lane = lax.broadcasted_iota(jnp.int32, (CHUNK, A_WIDTH), 1) >> (A_HEAD_DIM.bit_length() - 1)
    for ci in range(n_chunks):
        rows = slice(ci * CHUNK, (ci + 1) * CHUNK)
        gv = gv_ref[rows, :]
        z = bias_ref[...]
        for h in range(A_HEADS):
            z = z + _dot(wsm[h], jnp.where(head_of_lane == h, gv, 0.0).astype(BF16))
        ya_ref[rows, :] = gu_ref[rows, :] * z


def _chunkmlp(gu, gv, ws, bias, *, tm):
    m = gu.shape[0]
    row = pl.BlockSpec((tm, A_WIDTH), lambda i: (i, 0))
    return pl.pallas_call(
        functools.partial(_chunkmlp_kernel, n_chunks=tm // CHUNK),
        grid=(m // tm,),
        in_specs=[row, row, _wspec((A_HEADS, CHUNK, CHUNK)), _wspec((CHUNK, A_WIDTH))],
        out_specs=row,
        out_shape=jax.ShapeDtypeStruct((m, A_WIDTH), F32),
        compiler_params=_params(1),
        name="chunkmlp",
    )(gu, gv, ws, bias)


_COL_M, _COL_G, _COL_RS = 0, 8, 16


def _mlstm_kernel(q_ref, k_ref, v_ref, so_ref, gcol_ref, grow_ref, m0_ref, c0_ref, n0_ref, nw_ref,
                  yb_ref, cout_ref, nout_ref, mout_ref,
                  cs_ref, ns_ref, mp_ref, col_ref, sv_ref, qc_ref, nr_ref, *, ns, ls, g_tiles):
    @pl.when(pl.program_id(1) == 0)
    def _init():
        cs_ref[...] = jnp.zeros(cs_ref.shape, F32)
        ns_ref[...] = jnp.zeros(ns_ref.shape, F32)
        for s in range(g_tiles):
            for g in range(ns):
                for h in range(B_HEADS):
                    cs_ref[s, g * B_HEADS + h, :B_HEAD_DIM, :B_HEAD_DIM] = c0_ref[s, g, h]
                    ns_ref[s, g * B_HEADS + h, 0:1, :B_HEAD_DIM] = n0_ref[s, g, h:h + 1, :]
        mp_ref[...] = m0_ref[...]

    for s in range(g_tiles):
        at = lambda r: r.at[s]
        _mlstm_tile(at(q_ref), at(k_ref), at(v_ref), at(so_ref), at(gcol_ref), at(grow_ref), nw_ref,
                    at(yb_ref), at(cout_ref), at(nout_ref), at(mout_ref),
                    at(cs_ref), at(ns_ref), at(mp_ref), at(col_ref), at(sv_ref), at(qc_ref), at(nr_ref),
                    ns=ns, ls=ls)


def _mlstm_tile(q_ref, k_ref, v_ref, so_ref, gcol_ref, grow_ref, nw_ref,
                yb_ref, cout_ref, nout_ref, mout_ref,
                cs_ref, ns_ref, mp_ref, col_ref, sv_ref, qc_ref, nr_ref, *, ns, ls):
    L = CHUNK
    ls_shift = ls.bit_length() - 1

    col_ref[...] = jnp.zeros(col_ref.shape, F32)
    row = lax.broadcasted_iota(jnp.int32, (L, L), 0)
    col = lax.broadcasted_iota(jnp.int32, (L, L), 1)
    causal = (col <= row) & ((col >> ls_shift) == (row >> ls_shift))
    cum = jnp.where(causal, 1.0, 0.0).astype(BF16)
    rid = lax.broadcasted_iota(jnp.int32, (L, 1), 0)

    gcol = gcol_ref[...]
    grow = grow_ref[...]
    bcol = sum(_dot(cum, t) for t in _split3(gcol))
    brow = sum(_dot_nt(t, cum) for t in _split3(grow))
    mp = mp_ref[...]

    qs, ks, vs = [], [], []
    for h in range(B_HEADS):
        hs = slice(h * HP, (h + 1) * HP)
        qh = q_ref[:, hs].astype(BF16)
        kh = k_ref[:, hs].astype(BF16)
        vh = v_ref[:, hs].astype(BF16)
        qs.append(qh), ks.append(kh), vs.append(vh)
        ig_r = grow[h:h + 1, :]
        b_r = brow[B_HEADS + h:B_HEADS + h + 1, :]
        b_c = bcol[:, B_HEADS + h:B_HEADS + h + 1]
        bm = b_c + mp[:, h:h + 1]
        logd = jnp.where(causal, ig_r + b_c - b_r, -jnp.inf)
        m_t = jnp.maximum(bm, jnp.max(logd, axis=-1, keepdims=True))
        d = jnp.exp(logd - m_t)
        s = _dot_nt(qh, kh) * d
        sv_ref[h] = _dot(s.astype(BF16), vh)
        col_ref[:, _COL_M + h:_COL_M + h + 1] = m_t
        col_ref[:, _COL_G + h:_COL_G + h + 1] = jnp.exp(bm - m_t)
        col_ref[:, _COL_RS + h:_COL_RS + h + 1] = jnp.sum(s, axis=-1, keepdims=True)

    qc_ref[...] = jnp.zeros(qc_ref.shape, F32)
    nr_ref[...] = jnp.zeros(nr_ref.shape, F32)

    def seq_update(g):
        last = g * ls + (ls - 1)
        in_seq = (rid >> ls_shift) == g
        is_last = rid == last
        for h in range(B_HEADS):
            c_old = cs_ref[g * B_HEADS + h]
            n_old = ns_ref[g * B_HEADS + h, 0:1, :]
            qm = jnp.where(in_seq, qs[h], jnp.zeros_like(qs[h])) if ns > 1 else qs[h]
            qc_ref[h] += _dot_nt(qm, c_old.astype(BF16))
            nr_ref[h] += jnp.where(in_seq, n_old, 0.0)
            m_t = col_ref[:, _COL_M + h:_COL_M + h + 1]
            gdec = col_ref[:, _COL_G + h:_COL_G + h + 1]
            b_c = bcol[:, B_HEADS + h:B_HEADS + h + 1]
            ig_c = gcol[:, h:h + 1]
            pick = lambda a: jnp.sum(jnp.where(is_last, a, 0.0), axis=0, keepdims=True)
            m_new, b_last, decay = pick(m_t), pick(b_c), pick(gdec)
            w_c = jnp.where(in_seq, jnp.exp(ig_c + b_last - b_c - m_new), 0.0)
            vw = (vs[h].astype(F32) * w_c).astype(BF16)
            cs_ref[g * B_HEADS + h] = decay * c_old + _dot_tn(vw, ks[h])
            ns_ref[g * B_HEADS + h, 0:1, :] = (decay * n_old
                                               + jnp.sum(ks[h].astype(F32) * w_c, axis=0, keepdims=True))
            if ns == 1:
                mp_ref[:, h:h + 1] = jnp.broadcast_to(m_new, (L, 1))

    if ns == 1:
        seq_update(0)
    else:
        lax.fori_loop(0, ns, lambda g, c: (seq_update(g), c)[1], 0)

    for h in range(B_HEADS):
        hs = slice(h * HP, (h + 1) * HP)
        m_t = col_ref[:, _COL_M + h:_COL_M + h + 1]
        gdec = col_ref[:, _COL_G + h:_COL_G + h + 1]
        rs = col_ref[:, _COL_RS + h:_COL_RS + h + 1]
        num = gdec * qc_ref[h] + sv_ref[h]
        qn = jnp.sum(qs[h].astype(F32) * nr_ref[h], axis=-1, keepdims=True)
        den = gdec * qn + rs
        hh = num / jnp.maximum(jnp.abs(den), jnp.exp(-m_t))
        yb_ref[:, hs] = so_ref[:, hs] * _rms(hh, nw_ref[:, hs], B_HEAD_DIM)

    mout_ref[...] = col_ref[...]
    for g in range(ns):
        for h in range(B_HEADS):
            cout_ref[g, h] = cs_ref[g * B_HEADS + h, :B_HEAD_DIM, :B_HEAD_DIM]
            nout_ref[g, h:h + 1, :] = ns_ref[g * B_HEADS + h, 0:1, :B_HEAD_DIM]


def _mlstm(q, k, v, so, gcol, grow, m0, c0, n0, nw, *, ns, n_tiles, n_chunks, g_tiles):
    m = q.shape[0]
    nb = c0.shape[0]
    ls = CHUNK // ns
    g = g_tiles
    t3 = lambda a: a.reshape(n_tiles, n_chunks * CHUNK, a.shape[-1])
    rows = lambda w: pl.BlockSpec((g, CHUNK, w), lambda t, c: (t, c, 0))
    cspec = pl.BlockSpec((g, ns, B_HEADS, B_HEAD_DIM, B_HEAD_DIM), lambda t, c: (t, 0, 0, 0, 0))
    nspec = pl.BlockSpec((g, ns, B_HEADS, B_HEAD_DIM), lambda t, c: (t, 0, 0, 0))
    mspec = pl.BlockSpec((g, CHUNK, LANE), lambda t, c: (t, 0, 0))
    yb, c_new, n_new, m_rows = pl.pallas_call(
        functools.partial(_mlstm_kernel, ns=ns, ls=ls, g_tiles=g),
        grid=(n_tiles // g, n_chunks),
        in_specs=[rows(BW_P), rows(BW_P), rows(BW_P), rows(BW_P), rows(LANE),
                  pl.BlockSpec((g, None, 16, CHUNK), lambda t, c: (t, c, 0, 0)),
                  mspec, cspec, nspec, _wspec((1, BW_P))],
        out_specs=(rows(BW_P), cspec, nspec, mspec),
        out_shape=(jax.ShapeDtypeStruct((n_tiles, n_chunks * CHUNK, BW_P), F32),
                   jax.ShapeDtypeStruct((n_tiles, ns, B_HEADS, B_HEAD_DIM, B_HEAD_DIM), F32),
                   jax.ShapeDtypeStruct((n_tiles, ns, B_HEADS, B_HEAD_DIM), F32),
                   jax.ShapeDtypeStruct((n_tiles, CHUNK, LANE), F32)),
        scratch_shapes=[pltpu.VMEM((g, ns * B_HEADS, HP, HP), F32), pltpu.VMEM((g, ns * B_HEADS, 8, HP), F32),
                        pltpu.VMEM((g, CHUNK, LANE), F32), pltpu.VMEM((g, CHUNK, LANE), F32),
                        pltpu.VMEM((g, B_HEADS, CHUNK, HP), F32), pltpu.VMEM((g, B_HEADS, CHUNK, HP), F32),
                        pltpu.VMEM((g, B_HEADS, CHUNK, HP), F32)],
        compiler_params=_params(2),
        name="mlstm",
    )(t3(q), t3(k), t3(v), t3(so), t3(gcol), grow.reshape(n_tiles, n_chunks, 16, CHUNK), m0,
      c0.reshape(n_tiles, ns, B_HEADS, B_HEAD_DIM, B_HEAD_DIM), n0.reshape(n_tiles, ns, B_HEADS, B_HEAD_DIM), nw)
    return (yb.reshape(m, BW_P), c_new.reshape(nb, B_HEADS, B_HEAD_DIM, B_HEAD_DIM),
            n_new.reshape(nb, B_HEADS, B_HEAD_DIM), m_rows)


def _stack_heads(qc_ref, dtype):
    return jnp.concatenate([qc_ref[:, QC_SLOT * h:QC_SLOT * h + LATENT] for h in range(C_HEADS)],
                           axis=0).astype(dtype)


def _unstack_project(o, t, wuv_ref):
    wide = jnp.concatenate([o[h * t:(h + 1) * t] for h in range(C_HEADS)], axis=1)
    return _dot(wide.astype(BF16), wuv_ref[...])


def _softmax_step(s, pv, m_ref, acc_ref):
    reps = s.shape[1] // LANE
    m_prev = m_ref[...]
    m_new = jnp.maximum(m_prev, jnp.max(s, axis=-1, keepdims=True))
    p = jnp.exp(s - jnp.concatenate([m_new] * reps, axis=1))
    corr = jnp.exp(m_prev - m_new)
    acc_ref[...] = jnp.concatenate([corr, corr], axis=1) * acc_ref[...] + pv(p.astype(BF16))
    m_ref[...] = m_new


def _softmax_result(acc_ref):
    acc = acc_ref[...]
    return acc[:, :KV_LORA] / acc[:, KV_LORA:]


def _mla_prompt_kernel(qc_ref, lat_ref, wuv_ref, yc_ref, kb_ref, vb_ref, m_ref, acc_ref, *, tq, tk):
    qi = pl.program_id(1)

    @pl.when(qi == 0)
    def _cast_keys():
        lat = lat_ref[...]
        kb_ref[...] = lat.astype(BF16)
        vb_ref[:, 0:KV_LORA] = lat[:, 0:KV_LORA].astype(BF16)
        vb_ref[:, KV_LORA:2 * KV_LORA] = jnp.ones((lat.shape[0], KV_LORA), BF16)

    qst = _stack_heads(qc_ref, BF16)
    m_ref[...] = jnp.full(m_ref.shape, -jnp.inf, F32)
    acc_ref[...] = jnp.zeros(acc_ref.shape, F32)
    rows = C_HEADS * tq

    def block(j, masked):
        start = pl.multiple_of(j * tk, tk)
        s = _dot_nt(qst, kb_ref[pl.ds(start, tk), :])
        if masked:
            qpos = qi * tq + (lax.broadcasted_iota(jnp.int32, (rows, tk), 0) & (tq - 1))
            kpos = start + lax.broadcasted_iota(jnp.int32, (rows, tk), 1)
            s = jnp.where(kpos <= qpos, s, -jnp.inf)
        _softmax_step(s, lambda p: _dot(p, vb_ref[pl.ds(start, tk), :]), m_ref, acc_ref)

    n_full = (qi * tq) // tk
    lax.fori_loop(0, n_full, lambda j, c: (block(j, False), c)[1], 0)
    block(n_full, True)
    yc_ref[...] = _unstack_project(_softmax_result(acc_ref), tq, wuv_ref)


def _mla_prompt(qc, lat, wuv, *, nb, seq, tq, tk):
    m = qc.shape[0]
    n_q = seq // tq
    rows = C_HEADS * tq
    return pl.pallas_call(
        functools.partial(_mla_prompt_kernel, tq=tq, tk=tk),
        grid=(nb, n_q),
        in_specs=[pl.BlockSpec((tq, C_HEADS * QC_SLOT), lambda b, i: (b * n_q + i, 0)),
                  pl.BlockSpec((seq, LATENT), lambda b, i: (b, 0)),
                  _wspec((C_HEADS * KV_LORA, C_WIDTH))],
        out_specs=pl.BlockSpec((tq, C_WIDTH), lambda b, i: (b * n_q + i, 0)),
        out_shape=jax.ShapeDtypeStruct((m, C_WIDTH), F32),
        scratch_shapes=[pltpu.VMEM((seq, LATENT), BF16), pltpu.VMEM((seq, 2 * KV_LORA), BF16),
                        pltpu.VMEM((rows, LANE), F32), pltpu.VMEM((rows, 2 * KV_LORA), F32)],
        compiler_params=_params(2),
        name="mla_prompt",
    )(qc, lat, wuv)


def _mla_sample_kernel(pt_ref, qc_ref, latn_ref, wuv_ref, cache_ref, yc_ref, buf_ref, sem_ref, kt_ref, kn_ref,
                       *, layer, n_pages, t_new):
    b = pl.program_id(0)
    nb = pl.num_programs(0)
    slot = b & 1
    rows = C_HEADS * t_new

    def page_copy(seq, page, to_slot):
        return pltpu.make_async_copy(cache_ref.at[layer, pt_ref[seq, page]], buf_ref.at[to_slot, page],
                                     sem_ref.at[to_slot])

    def request(seq, to_slot):
        for page in range(n_pages):
            page_copy(seq, page, to_slot).start()

    @pl.when(b == 0)
    def _first():
        request(0, 0)

    @pl.when(b + 1 < nb)
    def _prefetch_next():
        request(b + 1, 1 - slot)

    for page in range(n_pages):
        page_copy(b, page, slot).wait()

    qst = _stack_heads(qc_ref, BF16)
    for page in range(n_pages):
        kt_ref[:, page * PAGE_SIZE:(page + 1) * PAGE_SIZE] = buf_ref[slot, page].astype(BF16)
    s_past = _dot(qst, kt_ref[...])

    kn_ref[...] = jnp.zeros(kn_ref.shape, F32)
    kn_ref[0:t_new, :] = latn_ref[...]
    kn = kn_ref[...].astype(BF16)
    tpos = lax.broadcasted_iota(jnp.int32, (rows, PAGE_SIZE), 0) & (t_new - 1)
    kidx = lax.broadcasted_iota(jnp.int32, (rows, PAGE_SIZE), 1)
    s_new = jnp.where(kidx <= tpos, _dot_nt(qst, kn), -jnp.inf)

    m = jnp.maximum(jnp.max(s_past, axis=-1, keepdims=True), jnp.max(s_new, axis=-1, keepdims=True))
    p_past = jnp.exp(s_past - m).astype(BF16)
    p_new = jnp.exp(s_new - m).astype(BF16)
    denom = (jnp.sum(p_past.astype(F32), axis=-1, keepdims=True)
             + jnp.sum(p_new.astype(F32), axis=-1, keepdims=True))
    o = _dot_nt(p_past, kt_ref[0:KV_LORA, :]) + _dot(p_new, kn[:, 0:KV_LORA])
    yc_ref[...] = _unstack_project(o / denom, t_new, wuv_ref)


def _mla_sample(page_table, qc, lat_new, wuv, cache_t, *, layer, t_new):
    m = qc.shape[0]
    nb, n_pages = page_table.shape
    grid_spec = pltpu.PrefetchScalarGridSpec(
        num_scalar_prefetch=1,
        grid=(nb,),
        in_specs=[pl.BlockSpec((t_new, C_HEADS * QC_SLOT), lambda b, pt: (b, 0)),
                  pl.BlockSpec((t_new, LATENT), lambda b, pt: (b, 0)),
                  pl.BlockSpec((C_HEADS * KV_LORA, C_WIDTH), lambda b, pt: (0, 0)),
                  pl.BlockSpec(memory_space=pl.ANY)],
        out_specs=pl.BlockSpec((t_new, C_WIDTH), lambda b, pt: (b, 0)),
        scratch_shapes=[pltpu.VMEM((2, n_pages, LATENT, PAGE_SIZE), F32), pltpu.SemaphoreType.DMA((2,)),
                        pltpu.VMEM((LATENT, n_pages * PAGE_SIZE), BF16), pltpu.VMEM((PAGE_SIZE, LATENT), F32)])
    return pl.pallas_call(
        functools.partial(_mla_sample_kernel, layer=layer, n_pages=n_pages, t_new=t_new),
        grid_spec=grid_spec,
        out_shape=jax.ShapeDtypeStruct((m, C_WIDTH), F32),
        compiler_params=_params(1),
        name="mla_sample",
    )(page_table, qc, lat_new, wuv, cache_t)


def _outproj_kernel(x_ref, ya_ref, yb_ref, yc_ref, wo_ref, nxw_ref, wq_ref, xo_ref, q_ref):
    y = (_dot(ya_ref[...].astype(BF16), wo_ref[0:A_WIDTH, :])
         + _dot(yb_ref[...].astype(BF16), wo_ref[A_WIDTH:A_WIDTH + BW_P, :])
         + _dot(yc_ref[...].astype(BF16), wo_ref[A_WIDTH + BW_P:N_MIX_P, :]))
    xn = x_ref[...] + y
    xo_ref[...] = xn
    hq = _rms(xn, nxw_ref[...], D_MODEL).astype(BF16)
    q_ref[...] = (_dot(hq, wq_ref[...]) * XA_SCALE).astype(q_ref.dtype)


def _outproj(x, ya, yb, yc, wo, nxw, wq, *, tm, q_dtype):
    m = x.shape[0]
    row = lambda w: pl.BlockSpec((tm, w), lambda i: (i, 0))
    return pl.pallas_call(
        _outproj_kernel,
        grid=(m // tm,),
        in_specs=[row(D_MODEL), row(A_WIDTH), row(BW_P), row(C_WIDTH), _wspec((N_MIX_P, D_MODEL)),
                  _wspec((1, D_MODEL)), _wspec((D_MODEL, D_MODEL))],
        out_specs=(row(D_MODEL), row(D_MODEL)),
        out_shape=(jax.ShapeDtypeStruct((m, D_MODEL), F32), jax.ShapeDtypeStruct((m, D_MODEL), q_dtype)),
        compiler_params=_params(1),
        name="outproj",
    )(x, ya, yb, yc, wo, nxw, wq)


def _memkv_kernel(mem_ref, nw_ref, wk_ref, wv_ref, k_ref, v_ref, kb_ref, vb_ref):
    mn = _rms(mem_ref[...], nw_ref[...], D_MODEL).astype(BF16)
    k = _dot(mn, wk_ref[...])
    v = _dot(mn, wv_ref[...])
    kb_ref[...] = k.astype(BF16)
    vb_ref[...] = v.astype(BF16)
    for h in range(X_HEADS):
        hs = slice(h * X_HEAD_DIM, (h + 1) * X_HEAD_DIM)
        k_ref[:, h, :] = k[:, hs]
        v_ref[:, h, :] = v[:, hs]


def _memkv(mem, nw, wk, wv, *, tm):
    m = mem.shape[0]
    row = pl.BlockSpec((tm, D_MODEL), lambda i: (i, 0))
    out = pl.BlockSpec((tm, X_HEADS, X_HEAD_DIM), lambda i: (i, 0, 0))
    return pl.pallas_call(
        _memkv_kernel,
        grid=(m // tm,),
        in_specs=[row, _wspec((1, D_MODEL)), _wspec((D_MODEL, D_MODEL)), _wspec((D_MODEL, D_MODEL))],
        out_specs=(out, out, row, row),
        out_shape=(jax.ShapeDtypeStruct((m, X_HEADS, X_HEAD_DIM), F32),) * 2
                  + (jax.ShapeDtypeStruct((m, D_MODEL), BF16),) * 2,
        compiler_params=_params(1),
        name="memkv",
    )(mem, nw, wk, wv)


def _softmax_rows(s):
    p = jnp.exp(s - jnp.max(s, axis=-1, keepdims=True))
    return p / jnp.sum(p, axis=-1, keepdims=True)


def _xattn_prompt_kernel(x_ref, q_ref, k_ref, v_ref, wo_ref, xo_ref, o_ref):
    for h in range(X_HEADS):
        hs = slice(h * X_HEAD_DIM, (h + 1) * X_HEAD_DIM)
        p = _softmax_rows(_dot_nt(q_ref[:, hs], k_ref[:, hs]))
        o_ref[:, hs] = _dot(p.astype(BF16), v_ref[:, hs])
    xo_ref[...] = x_ref[...] + _dot(o_ref[...].astype(BF16), wo_ref[...])


def _xattn_prompt(x, q, k, v, wo, *, tm, tiles_per_seq):
    m = x.shape[0]
    row = pl.BlockSpec((tm, D_MODEL), lambda i: (i, 0))
    kv = pl.BlockSpec((MEM_LEN, D_MODEL), lambda i: (i // tiles_per_seq, 0))
    return pl.pallas_call(
        _xattn_prompt_kernel,
        grid=(m // tm,),
        in_specs=[row, row, kv, kv, _wspec((D_MODEL, D_MODEL))],
        out_specs=row,
        out_shape=jax.ShapeDtypeStruct((m, D_MODEL), F32),
        scratch_shapes=[pltpu.VMEM((tm, D_MODEL), F32)],
        compiler_params=_params(1),
        name="xattn_prompt",
    )(x, q, k, v, wo)


def _xattn_sample_kernel(x_ref, q_ref, k_ref, v_ref, wo_ref, xo_ref, o_ref, *, n_seq, t_rows):
    rows = X_HEADS * t_rows
    nk = MEM_LEN * X_HEADS
    r = lax.broadcasted_iota(jnp.int32, (rows, nk), 0)
    c = lax.broadcasted_iota(jnp.int32, (rows, nk), 1)
    keep = (c & (X_HEADS - 1)) == (r >> (t_rows.bit_length() - 1))
    for g in range(n_seq):
        rs = slice(g * t_rows, (g + 1) * t_rows)
        qst = jnp.concatenate([q_ref[rs, h * X_HEAD_DIM:(h + 1) * X_HEAD_DIM] for h in range(X_HEADS)], axis=0)
        k2 = k_ref[g].reshape(nk, X_HEAD_DIM).astype(BF16)
        v2 = v_ref[g].reshape(nk, X_HEAD_DIM).astype(BF16)
        p = _softmax_rows(jnp.where(keep, _dot_nt(qst.astype(BF16), k2), -jnp.inf))
        o = _dot(p.astype(BF16), v2)
        for h in range(X_HEADS):
            o_ref[rs, h * X_HEAD_DIM:(h + 1) * X_HEAD_DIM] = o[h * t_rows:(h + 1) * t_rows]
    xo_ref[...] = x_ref[...] + _dot(o_ref[...].astype(BF16), wo_ref[...])


def _xattn_sample(x, q, k, v, wo, *, layer, n_seq, t_rows):
    m = x.shape[0]
    r = n_seq * t_rows
    row = pl.BlockSpec((r, D_MODEL), lambda i: (i, 0))
    kv = pl.BlockSpec((None, n_seq, MEM_LEN, X_HEADS, X_HEAD_DIM), lambda i: (layer, i, 0, 0, 0))
    return pl.pallas_call(
        functools.partial(_xattn_sample_kernel, n_seq=n_seq, t_rows=t_rows),
        grid=(m // r,),
        in_specs=[row, row, kv, kv, _wspec((D_MODEL, D_MODEL))],
        out_specs=row,
        out_shape=jax.ShapeDtypeStruct((m, D_MODEL), F32),
        scratch_shapes=[pltpu.VMEM((r, D_MODEL), F32)],
        compiler_params=_params(1),
        name="xattn_sample",
    )(x, q, k, v, wo)


def _ffn_kernel(x_ref, nw_ref, wg_ref, wu_ref, wd_ref, cw_ref, cb_ref, p0_ref, p1_ref, fnw_ref, *rest,
                seq_rows, tiles_per_seq, final_norm):
    if final_norm:
        xo_ref, gtail_ref, y_ref, carry_ref, act_ref = rest
    else:
        xo_ref, gtail_ref, carry_ref, act_ref = rest
    tm = x_ref.shape[0]
    x = x_ref[...]
    xn = _rms(x, nw_ref[...], D_MODEL).astype(BF16)
    rid = lax.broadcasted_iota(jnp.int32, (tm, 1), 0)
    if seq_rows:
        pos = rid & (seq_rows - 1)
    else:
        pos = rid

        @pl.when(pl.program_id(0) % tiles_per_seq == 0)
        def _seq_start():
            carry_ref[0:1, :] = p0_ref[...]
            carry_ref[1:2, :] = p1_ref[...]

    for c in range(N_FF_CHUNKS):
        cs = slice(c * FF_CHUNK, (c + 1) * FF_CHUNK)
        g = _dot(xn, wg_ref[:, cs])
        u = _dot(xn, wu_ref[:, cs])
        if seq_rows:
            prev0, prev1 = p0_ref[:, cs], p1_ref[:, cs]
            gtail_ref[:, cs] = g
        else:
            prev0, prev1 = carry_ref[0:1, cs], carry_ref[1:2, cs]
            gtail_ref[:, cs] = g[tm - 2:tm, :]
            carry_ref[0:2, cs] = g[tm - 2:tm, :]
        g1 = jnp.where(pos == 0, prev1, pltpu.roll(g, 1, 0))
        g2 = jnp.where(pos == 0, prev0, jnp.where(pos == 1, prev1, pltpu.roll(g, 2, 0)))
        y = cb_ref[:, cs] + g2 * cw_ref[0:1, cs] + g1 * cw_ref[1:2, cs] + g * cw_ref[2:3, cs]
        act_ref[:, cs] = (_gelu(y) * u).astype(BF16)
    xo = x + _dot(act_ref[...], wd_ref[...])
    xo_ref[...] = xo
    if final_norm:
        y_ref[...] = _rms(xo, fnw_ref[...], D_MODEL)


def _ffn(x, nw, wg, wu, wd, cw, cb, p0, p1, fnw, *, tm, seq_rows, tiles_per_seq, final_norm):
    m = x.shape[0]
    n_tiles = m // tm
    row = pl.BlockSpec((tm, D_MODEL), lambda i: (i, 0))
    if seq_rows:
        pspec = pl.BlockSpec((tm, D_FF), lambda i: (i, 0))
        gspec = pspec
        gshape = jax.ShapeDtypeStruct((m, D_FF), F32)
    else:
        pspec = pl.BlockSpec((None, 1, D_FF), lambda i: (i // tiles_per_seq, 0, 0))
        gspec = pl.BlockSpec((None, 2, D_FF), lambda i: (i, 0, 0))
        gshape = jax.ShapeDtypeStruct((n_tiles, 2, D_FF), F32)
    return pl.pallas_call(
        functools.partial(_ffn_kernel, seq_rows=seq_rows, tiles_per_seq=tiles_per_seq, final_norm=final_norm),
        grid=(n_tiles,),
        in_specs=[row, _wspec((1, D_MODEL)), _wspec((D_MODEL, D_FF)), _wspec((D_MODEL, D_FF)),
                  _wspec((D_FF, D_MODEL)), _wspec((CONV_W, D_FF)), _wspec((1, D_FF)),
                  pspec, pspec, _wspec((1, D_MODEL))],
        out_specs=(row, gspec) + ((row,) if final_norm else ()),
        out_shape=(jax.ShapeDtypeStruct((m, D_MODEL), F32), gshape)
                  + ((jax.ShapeDtypeStruct((m, D_MODEL), F32),) if final_norm else ()),
        scratch_shapes=[pltpu.VMEM((8, D_FF), F32), pltpu.VMEM((tm, D_FF), BF16)],
        compiler_params=_params(1),
        name="ffn",
    )(x, nw, wg, wu, wd, cw, cb, p0, p1, fnw)


def _pad_heads_idx(start):
    idx = np.full((BW_P,), -1, np.int64)
    for h in range(B_HEADS):
        idx[h * HP:h * HP + B_HEAD_DIM] = start + h * B_HEAD_DIM + np.arange(B_HEAD_DIM)
    return idx


def _take_cols(w, idx, axis=-1):
    axis = axis % w.ndim
    pieces, i, n = [], 0, len(idx)
    while i < n:
        j = i + 1
        if idx[i] < 0:
            while j < n and idx[j] < 0:
                j += 1
            shape = w.shape[:axis] + (j - i,) + w.shape[axis + 1:]
            pieces.append(jnp.zeros(shape, w.dtype))
        else:
            while j < n and idx[j] == idx[j - 1] + 1:
                j += 1
            pieces.append(lax.slice_in_dim(w, int(idx[i]), int(idx[i]) + (j - i), axis=axis))
        i = j
    return jnp.concatenate(pieces, axis=axis)


def _win_layout():
    half = C_ROPE // 2
    base = {'au': 0, 'av': 256, 'bq': 512, 'bk': 896, 'bv': 1280, 'bo': 1664, 'bi': 2048, 'bf': 2052,
            'cq': 2056, 'ckv': 2248, 'ckr': 2376}
    idx = np.full((N_IN,), -1, np.int64)
    idx[S_AU:S_AU + 256] = base['au'] + np.arange(256)
    idx[S_AV:S_AV + 256] = base['av'] + np.arange(256)
    for s, name in ((S_BQ, 'bq'), (S_BK, 'bk'), (S_BV, 'bv'), (S_BO, 'bo')):
        idx[s:s + BW_P] = _pad_heads_idx(base[name])
    idx[S_CQ:S_CQ + Q_LORA] = base['cq'] + np.arange(Q_LORA)
    idx[S_CKV:S_CKV + KV_LORA] = base['ckv'] + np.arange(KV_LORA)
    idx[S_R1:S_R1 + C_ROPE] = base['ckr'] + np.arange(C_ROPE)
    idx[S_R2:S_R2 + half] = base['ckr'] + half + np.arange(half)
    idx[S_R2 + half:S_R2 + C_ROPE] = base['ckr'] + np.arange(half)
    idx[S_G:S_G + B_HEADS] = base['bi'] + np.arange(B_HEADS)
    idx[S_G + B_HEADS:S_G + 2 * B_HEADS] = base['bf'] + np.arange(B_HEADS)
    return idx


def _wuq_layout():
    half = C_ROPE // 2
    per = C_NOPE + C_ROPE
    idx = np.full((N_UQ,), -1, np.int64)
    for h in range(C_HEADS):
        idx[UQ_NOPE + h * C_NOPE:UQ_NOPE + (h + 1) * C_NOPE] = h * per + np.arange(C_NOPE)
        idx[UQ_RA + h * LANE:UQ_RA + h * LANE + C_ROPE] = h * per + C_NOPE + np.arange(C_ROPE)
        idx[UQ_RB + h * LANE:UQ_RB + h * LANE + half] = h * per + C_NOPE + half + np.arange(half)
        idx[UQ_RB + h * LANE + half:UQ_RB + h * LANE + C_ROPE] = h * per + C_NOPE + np.arange(half)
    return idx


def _rope_tables(pos, reps):
    half = C_ROPE // 2
    inv = ROPE_THETA ** (-jnp.arange(half, dtype=F32) / half)
    ang = pos.astype(F32)[:, None] * inv[None, :]
    cos, sin = jnp.cos(ang), jnp.sin(ang)
    z = jnp.zeros((pos.shape[0], LANE - C_ROPE), F32)
    ct = jnp.concatenate([cos, cos, z], axis=1)
    st = jnp.concatenate([-sin, sin, z], axis=1)
    return jnp.tile(ct, (reps, 1)), jnp.tile(st, (reps, 1))


def _pad_lanes(v, idx):
    return _take_cols(v[None, :], idx)


def _layer_weights(l, w):
    hp_idx = _pad_heads_idx(0)
    out = {}
    out['nw_mix'] = w['norm_mix_w'][l][None, :]
    out['w1'] = _take_cols(w['w_in'][l], _win_layout()).astype(BF16)
    out['qnw'] = _pad_lanes(w['mla_qnorm_w'][l], np.concatenate([np.arange(Q_LORA), np.full(CQ_P - Q_LORA, -1)]))
    out['kvnw'] = w['mla_kvnorm_w'][l][None, :]
    wuq = w['mla_w_uq'][l].reshape(Q_LORA, C_HEADS * (C_NOPE + C_ROPE))
    wuq = _take_cols(wuq, _wuq_layout())
    out['wuq'] = jnp.concatenate([wuq, jnp.zeros((CQ_P - Q_LORA, N_UQ), F32)], axis=0).astype(BF16)
    wuk = jnp.transpose(w['mla_w_uk'][l], (1, 2, 0))
    eye = jnp.eye(C_HEADS, dtype=F32)
    out['wuk'] = jnp.einsum('hdc,hg->hdgc', wuk, eye).reshape(C_HEADS * C_NOPE, C_HEADS * KV_LORA).astype(BF16)
    wuv = jnp.transpose(w['mla_w_uv'][l], (1, 0, 2))
    out['wuv'] = jnp.einsum('hce,hg->hcge', wuv, eye).reshape(C_HEADS * KV_LORA, C_WIDTH).astype(BF16)
    gb = jnp.concatenate([w['mlstm_bi'][l], w['mlstm_bf'][l]])
    out['gb'] = _pad_lanes(gb, np.concatenate([np.arange(2 * B_HEADS), np.full(LANE - 2 * B_HEADS, -1)]))
    out['mnw'] = _pad_lanes(w['mlstm_norm_w'][l], hp_idx)
    wo = w['w_out'][l]
    rows = np.concatenate([np.arange(A_WIDTH), np.where(hp_idx < 0, -1, hp_idx + A_WIDTH),
                           A_WIDTH + B_WIDTH + np.arange(C_WIDTH)])
    out['wo'] = _take_cols(wo, rows, axis=0).astype(BF16)
    out['nw_x'] = w['norm_x_w'][l][None, :]
    out['nw_mem'] = w['norm_mem_w'][l][None, :]
    out['xwq'] = w['xa_wq'][l].reshape(D_MODEL, D_MODEL).astype(BF16)
    out['xwk'] = w['xa_wk'][l].reshape(D_MODEL, D_MODEL).astype(BF16)
    out['xwv'] = w['xa_wv'][l].reshape(D_MODEL, D_MODEL).astype(BF16)
    out['xwo'] = w['xa_wo'][l].reshape(D_MODEL, D_MODEL).astype(BF16)
    out['nw_ffn'] = w['norm_ffn_w'][l][None, :]
    out['wg'] = w['ffn_wg'][l].astype(BF16)
    out['wu'] = w['ffn_wu'][l].astype(BF16)
    out['wd'] = w['ffn_wd'][l].astype(BF16)
    out['cw'] = w['ffn_conv_w'][l]
    out['cb'] = w['ffn_conv_b'][l][None, :]
    return out


def _chunk_mixers(ws, cb, t_rows):
    reps = CHUNK // t_rows
    wt = ws[:, :t_rows, :t_rows]
    if reps > 1:
        wt = jnp.einsum('ab,hts->hatbs', jnp.eye(reps, dtype=F32), wt).reshape(A_HEADS, CHUNK, CHUNK)
    bias = jnp.tile(jnp.repeat(jnp.transpose(cb[:, :t_rows]), A_HEAD_DIM, axis=1), (reps, 1))
    return wt, bias


def _mixer_block(x, lw, *, mlstm_state, pos_tables, chunk_w, tm, act_dtype, ns, n_tiles, n_chunks, mla_fn):
    m = x.shape[0]
    cos_t, sin_t = pos_tables
    gu, gv, q, k, v, so, qc, lat, gate = _inproj(x, lw['nw_mix'], lw['w1'], lw['qnw'], lw['kvnw'], lw['wuq'],
                                                 lw['wuk'], lw['gb'], cos_t, sin_t, tm=tm, act_dtype=act_dtype)
    ya = _chunkmlp(gu, gv, chunk_w[0], chunk_w[1], tm=tm)
    grow = jnp.transpose(gate[:, :16].reshape(m // CHUNK, CHUNK, 16), (0, 2, 1))
    c0, n0, m0 = mlstm_state
    yb, c_new, n_new, m_rows = _mlstm(q, k, v, so, gate, grow, m0, c0, n0, lw['mnw'], ns=ns, n_tiles=n_tiles,
                                      n_chunks=n_chunks, g_tiles=1)
    yc = mla_fn(qc, lat)
    return (ya, yb, yc), (c_new, n_new, m_rows), lat, gv


def kernel(x_prompt, x_sample, mem_prompt, cache_mla, page_table, cache_mem_k, cache_mem_v, state_mlstm_C, state_mlstm_n, state_mlstm_m, state_ffn_conv, norm_mix_w, w_in, chunk_ws, chunk_b, mlstm_bi, mlstm_bf, mlstm_norm_w, mla_qnorm_w, mla_kvnorm_w, mla_w_uq, mla_w_uk, mla_w_uv, w_out, norm_x_w, norm_mem_w, xa_wq, xa_wk, xa_wv, xa_wo, norm_ffn_w, ffn_wg, ffn_wu, ffn_conv_w, ffn_conv_b, ffn_wd, norm_final_w):
    w = dict(norm_mix_w=norm_mix_w, w_in=w_in, mlstm_bi=mlstm_bi, mlstm_bf=mlstm_bf, mlstm_norm_w=mlstm_norm_w,
             mla_qnorm_w=mla_qnorm_w, mla_kvnorm_w=mla_kvnorm_w, mla_w_uq=mla_w_uq, mla_w_uk=mla_w_uk,
             mla_w_uv=mla_w_uv, w_out=w_out, norm_x_w=norm_x_w, norm_mem_w=norm_mem_w, xa_wq=xa_wq, xa_wk=xa_wk,
             xa_wv=xa_wv, xa_wo=xa_wo, norm_ffn_w=norm_ffn_w, ffn_wg=ffn_wg, ffn_wu=ffn_wu,
             ffn_conv_w=ffn_conv_w, ffn_conv_b=ffn_conv_b, ffn_wd=ffn_wd)
    depth = w_in.shape[0]
    nbp, S, _ = x_prompt.shape
    nbs, T, _ = x_sample.shape
    n_pages = page_table.shape[1]
    past_len = n_pages * PAGE_SIZE
    mp, ms = nbp * S, nbs * T
    assert S % CHUNK == 0 and CHUNK % T == 0 and T % 8 == 0 and ms % CHUNK == 0

    tm_p = min(512, S)
    tm_s = min(512, ms)
    tq = min(256, S)
    tk = min(512, S)
    ns_s = CHUNK // T
    xg = 4 if nbs % 4 == 0 else 1
    tm_ffn_p = min(512, S)
    tm_ffn_s = min(256, ms)

    pos_p = _rope_tables(jnp.arange(S), 1)
    pos_s = _rope_tables(past_len + jnp.arange(T), tm_s // T)
    fnw = norm_final_w[None, :]

    xp = x_prompt.reshape(mp, D_MODEL)
    xs = x_sample.reshape(ms, D_MODEL)
    mem = mem_prompt.reshape(nbp * MEM_LEN, D_MODEL)

    zero_c = jnp.zeros((nbp, B_HEADS, B_HEAD_DIM, B_HEAD_DIM), F32)
    zero_n = jnp.zeros((nbp, B_HEADS, B_HEAD_DIM), F32)
    zero_m = jnp.zeros((nbp, CHUNK, LANE), F32)
    zero_conv = jnp.zeros((nbp, 1, D_FF), F32)
    cache_t = jnp.swapaxes(cache_mla, 2, 3)

    outs = {k: [] for k in ('lat_p', 'lat_s', 'mk', 'mv', 'cp', 'np', 'mp', 'cs', 'ns', 'ms', 'conv_p', 'conv_s',
                            'chunkv')}
    yp = ys = None
    for l in range(depth):
        lw = _layer_weights(l, w)
        last = l == depth - 1

        mla_p = functools.partial(_mla_prompt, wuv=lw['wuv'], nb=nbp, seq=S, tq=tq, tk=tk)
        ymix, st, lat, _ = _mixer_block(
            xp, lw, mlstm_state=(zero_c, zero_n, zero_m), pos_tables=pos_p,
            chunk_w=_chunk_mixers(chunk_ws[l], chunk_b[l], CHUNK), tm=tm_p, act_dtype=BF16, ns=1, n_tiles=nbp,
            n_chunks=S // CHUNK, mla_fn=lambda qc, lt: mla_p(qc, lt))
        xp, qx_p = _outproj(xp, *ymix, lw['wo'], lw['nw_x'], lw['xwq'], tm=tm_p, q_dtype=BF16)
        outs['lat_p'].append(lat.reshape(nbp, S, LATENT))
        outs['cp'].append(st[0])
        outs['np'].append(st[1])
        outs['mp'].append(st[2][:, CHUNK - 1, :B_HEADS])

        m0_rows = jnp.concatenate([jnp.repeat(state_mlstm_m[l], T, axis=0),
                                   jnp.zeros((ms, LANE - B_HEADS), F32)], axis=1).reshape(ms // CHUNK, CHUNK, LANE)
        mla_s = functools.partial(_mla_sample, page_table, wuv=lw['wuv'], cache_t=cache_t, layer=l, t_new=T)
        ymix, st, lat, gv = _mixer_block(
            xs, lw, mlstm_state=(state_mlstm_C[l], state_mlstm_n[l], m0_rows), pos_tables=pos_s,
            chunk_w=_chunk_mixers(chunk_ws[l], chunk_b[l], T), tm=tm_s, act_dtype=F32, ns=ns_s,
            n_tiles=ms // CHUNK, n_chunks=1, mla_fn=lambda qc, lt: mla_s(qc, lt))
        xs, qx_s = _outproj(xs, *ymix, lw['wo'], lw['nw_x'], lw['xwq'], tm=tm_s, q_dtype=F32)
        outs['lat_s'].append(lat.reshape(nbs, T, LATENT))
        outs['cs'].append(st[0])
        outs['ns'].append(st[1])
        outs['ms'].append(st[2].reshape(nbs, T, LANE)[:, T - 1, :B_HEADS])
        outs['chunkv'].append(gv.reshape(nbs, T, A_WIDTH))

        mk, mv, mk_b, mv_b = _memkv(mem, lw['nw_mem'], lw['xwk'], lw['xwv'], tm=min(512, nbp * MEM_LEN))
        outs['mk'].append(mk.reshape(nbp, MEM_LEN, X_HEADS, X_HEAD_DIM))
        outs['mv'].append(mv.reshape(nbp, MEM_LEN, X_HEADS, X_HEAD_DIM))
        xp = _xattn_prompt(xp, qx_p, mk_b, mv_b, lw['xwo'], tm=tm_p, tiles_per_seq=S // tm_p)
        xs = _xattn_sample(xs, qx_s, cache_mem_k, cache_mem_v, lw['xwo'], layer=l, n_seq=xg, t_rows=T)

        res_p = _ffn(xp, lw['nw_ffn'], lw['wg'], lw['wu'], lw['wd'], lw['cw'], lw['cb'], zero_conv, zero_conv, fnw,
                     tm=tm_ffn_p, seq_rows=0, tiles_per_seq=S // tm_ffn_p, final_norm=last)
        buf = state_ffn_conv[l]
        p0 = jnp.repeat(buf[:, 0, :], T, axis=0)
        p1 = jnp.repeat(buf[:, 1, :], T, axis=0)
        res_s = _ffn(xs, lw['nw_ffn'], lw['wg'], lw['wu'], lw['wd'], lw['cw'], lw['cb'], p0, p1, fnw,
                     tm=tm_ffn_s, seq_rows=T, tiles_per_seq=1, final_norm=last)
        xp, xs = res_p[0], res_s[0]
        outs['conv_p'].append(res_p[1][S // tm_ffn_p - 1::S // tm_ffn_p])
        outs['conv_s'].append(res_s[1].reshape(nbs, T, D_FF)[:, T - (CONV_W - 1):, :])
        if last:
            yp, ys = res_p[2], res_s[2]

    st = lambda k: jnp.stack(outs[k])
    return (yp.reshape(nbp, S, D_MODEL), ys.reshape(nbs, T, D_MODEL),
            st('lat_p'), st('lat_s'), st('mk'), st('mv'),
            st('cp'), st('np'), st('mp'), st('cs'), st('ns'), st('ms'),
            st('conv_p'), st('conv_s'), st('chunkv'))
```

```python
import functools
import math

import numpy as np
import jax
import jax.numpy as jnp
from jax import lax
from jax.experimental import pallas as pl
from jax.experimental.pallas import tpu as pltpu

F32 = jnp.float32
BF16 = jnp.bfloat16

D_MODEL = 1024
PAGE_SIZE = 128
A_HEADS = 4
A_WIDTH = 256
A_HEAD_DIM = 64
CHUNK = 128
B_HEADS = 4
B_WIDTH = 384
B_HEAD_DIM = 96
MLSTM_KSCALE = B_HEAD_DIM ** -0.5
C_WIDTH = 384
C_HEADS = 6
C_V_DIM = 64
C_NOPE = 64
C_ROPE = 32
Q_LORA = 192
KV_LORA = 128
LATENT = KV_LORA + C_ROPE
MLA_SCALE = (C_NOPE + C_ROPE) ** -0.5
ROPE_THETA = 10000.0
MEM_LEN = 256
X_HEADS = 4
X_HEAD_DIM = 256
XA_SCALE = X_HEAD_DIM ** -0.5
D_FF = 2816
CONV_W = 3
EPS = 1e-6

LANE = 128
HP = 128
BW_P = B_HEADS * HP
FF_CHUNK = 256
N_FF_CHUNKS = D_FF // FF_CHUNK
VMEM_LIMIT = 48 * 1024 * 1024

S_AU, S_AV, S_BQ, S_BK, S_BV, S_BO = 0, 256, 512, 1024, 1536, 2048
S_CQ, S_CKV, S_R1, S_R2, S_G = 2560, 2816, 2944, 3072, 3200
N_IN = 3328
CQ_P = 256
UQ_NOPE, UQ_RA, UQ_RB, N_UQ = 0, 384, 1152, 1920
QC_SLOT = 256
N_MIX_P = A_WIDTH + BW_P + C_WIDTH


def _rms(x, w, n):
    ms = jnp.sum(x * x, axis=-1, keepdims=True) * (1.0 / n)
    return x * lax.rsqrt(ms + EPS) * w


def _gelu(x):
    c = math.sqrt(2.0 / math.pi)
    return x * (0.5 * (1.0 + jnp.tanh(c * (x + 0.044715 * (x * x * x)))))


def _sigmoid(x):
    return 1.0 / (1.0 + jnp.exp(-x))


def _log_sigmoid(x):
    return jnp.minimum(x, 0.0) - jnp.log1p(jnp.exp(-jnp.abs(x)))


def _dot(a, b):
    return jnp.dot(a, b, preferred_element_type=F32)


def _dot_nt(a, b):
    return lax.dot_general(a, b, (((1,), (1,)), ((), ())), preferred_element_type=F32)


def _dot_tn(a, b):
    return lax.dot_general(a, b, (((0,), (0,)), ((), ())), preferred_element_type=F32)


def _split3(a):
    a1 = a.astype(BF16)
    r1 = a - a1.astype(F32)
    a2 = r1.astype(BF16)
    a3 = (r1 - a2.astype(F32)).astype(BF16)
    return a1, a2, a3


def _wspec(shape):
    nd = len(shape)
    return pl.BlockSpec(shape, lambda *_: (0,) * nd, pipeline_mode=pl.Buffered(1))


def _params(n_axes):
    return pltpu.CompilerParams(dimension_semantics=("arbitrary",) * n_axes,
                                vmem_limit_bytes=VMEM_LIMIT)


def _inproj_kernel(x_ref, nw_ref, w1_ref, qnw_ref, kvnw_ref, wuq_ref, wuk_ref, gb_ref, cos_ref, sin_ref,
                   gu_ref, gv_ref, q_ref, k_ref, v_ref, so_ref, qc_ref, lat_ref, gate_ref):
    xn = _rms(x_ref[...], nw_ref[...], D_MODEL).astype(BF16)

    def seg(a, b):
        return _dot(xn, w1_ref[:, a:b])

    gu_ref[...] = _gelu(seg(S_AU, S_AV))
    gv_ref[...] = _gelu(seg(S_AV, S_BQ))
    q_ref[...] = seg(S_BQ, S_BK).astype(q_ref.dtype)
    k_ref[...] = (seg(S_BK, S_BV) * MLSTM_KSCALE).astype(k_ref.dtype)
    v_ref[...] = seg(S_BV, S_BO).astype(v_ref.dtype)
    so_ref[...] = _sigmoid(seg(S_BO, S_CQ))

    c_q = _rms(seg(S_CQ, S_CKV), qnw_ref[...], Q_LORA).astype(BF16)
    q2 = _dot(c_q, wuq_ref[...])
    qlat = _dot(q2[:, UQ_NOPE:UQ_RA].astype(BF16), wuk_ref[...])
    cos = cos_ref[...]
    sin = sin_ref[...]
    for h in range(C_HEADS):
        ra = q2[:, UQ_RA + LANE * h:UQ_RA + LANE * (h + 1)]
        rb = q2[:, UQ_RB + LANE * h:UQ_RB + LANE * (h + 1)]
        qr = ra * cos + rb * sin
        qc_ref[:, QC_SLOT * h:QC_SLOT * h + LANE] = (qlat[:, LANE * h:LANE * (h + 1)] * MLA_SCALE).astype(qc_ref.dtype)
        qc_ref[:, QC_SLOT * h + LANE:QC_SLOT * (h + 1)] = (qr * MLA_SCALE).astype(qc_ref.dtype)

    lat_ref[:, 0:KV_LORA] = _rms(seg(S_CKV, S_R1), kvnw_ref[...], KV_LORA)
    kr = seg(S_R1, S_R2) * cos + seg(S_R2, S_G) * sin
    lat_ref[:, KV_LORA:LATENT] = kr[:, :C_ROPE]

    gt = seg(S_G, N_IN) + gb_ref[...]
    lane = lax.broadcasted_iota(jnp.int32, gt.shape, 1)
    gate_ref[...] = jnp.where(lane < B_HEADS, gt, jnp.where(lane < 2 * B_HEADS, _log_sigmoid(gt), 0.0))


def _inproj(x, nw, w1, qnw, kvnw, wuq, wuk, gb, cos_t, sin_t, *, tm, act_dtype):
    m = x.shape[0]
    n_pos_blocks = cos_t.shape[0] // tm
    row = lambda w: pl.BlockSpec((tm, w), lambda i: (i, 0))
    pos = pl.BlockSpec((tm, LANE), lambda i: (i % n_pos_blocks, 0))
    out_shapes = (
        jax.ShapeDtypeStruct((m, A_WIDTH), F32), jax.ShapeDtypeStruct((m, A_WIDTH), F32),
        jax.ShapeDtypeStruct((m, BW_P), act_dtype), jax.ShapeDtypeStruct((m, BW_P), act_dtype),
        jax.ShapeDtypeStruct((m, BW_P), act_dtype), jax.ShapeDtypeStruct((m, BW_P), F32),
        jax.ShapeDtypeStruct((m, C_HEADS * QC_SLOT), act_dtype), jax.ShapeDtypeStruct((m, LATENT), F32),
        jax.ShapeDtypeStruct((m, LANE), F32))
    return pl.pallas_call(
        _inproj_kernel,
        grid=(m // tm,),
        in_specs=[row(D_MODEL), _wspec((1, D_MODEL)), _wspec((D_MODEL, N_IN)), _wspec((1, CQ_P)),
                  _wspec((1, KV_LORA)), _wspec((CQ_P, N_UQ)), _wspec((C_HEADS * C_NOPE, C_HEADS * LANE)),
                  _wspec((1, LANE)), pos, pos],
        out_specs=(row(A_WIDTH), row(A_WIDTH), row(BW_P), row(BW_P), row(BW_P), row(BW_P),
                   row(C_HEADS * QC_SLOT), row(LATENT), row(LANE)),
        out_shape=out_shapes,
        compiler_params=_params(1),
        name="inproj",
    )(x, nw, w1, qnw, kvnw, wuq, wuk, gb, cos_t, sin_t)


def _chunkmlp_kernel(gu_ref, gv_ref, ws_ref, bias_ref, ya_ref, *, n_chunks):
    r = lax.broadcasted_iota(jnp.int32, (CHUNK, CHUNK), 0)
    c = lax.broadcasted_iota(jnp.int32, (CHUNK, CHUNK), 1)
    wsm = [jnp.where(r >= c, ws_ref[h], 0.0).astype(BF16) for h in range(A_HEADS)]
    head_of_lane = lax.broadcasted_iota(jnp.int32, (CHUNK, A_WIDTH), 1) >> (A_HEAD_DIM.bit_length() - 1)
    for ci in range(n_chunks):
        rows = slice(ci * CHUNK, (ci + 1) * CHUNK)
        gv = gv_ref[rows, :]
        z = bias_ref[...]
        for h in range(A_HEADS):
            z = z + _dot(wsm[h], jnp.where(head_of_lane == h, gv, 0.0).astype(BF16))
        ya_ref[rows, :] = gu_ref[rows, :] * z


def _chunkmlp(gu, gv, ws, bias, *, tm):
    m = gu.shape[0]
    row = pl.BlockSpec((tm, A_WIDTH), lambda i: (i, 0))
    return pl.pallas_call(
        functools.partial(_chunkmlp_kernel, n_chunks=tm // CHUNK),
        grid=(m // tm,),
        in_specs=[row, row, _wspec((A_HEADS, CHUNK, CHUNK)), _wspec((CHUNK, A_WIDTH))],
        out_specs=row,
        out_shape=jax.ShapeDtypeStruct((m, A_WIDTH), F32),
        compiler_params=_params(1),
        name="chunkmlp",
    )(gu, gv, ws, bias)


_COL_M, _COL_G, _COL_RS = 0, 8, 16


def _mlstm_kernel(q_ref, k_ref, v_ref, so_ref, gcol_ref, grow_ref, m0_ref, c0_ref, n0_ref, nw_ref,
                  yb_ref, cout_ref, nout_ref, mout_ref,
                  cs_ref, ns_ref, mp_ref, col_ref, sv_ref, qc_ref, nr_ref, *, ns, ls, g_tiles):
    @pl.when(pl.program_id(1) == 0)
    def _init():
        cs_ref[...] = jnp.zeros(cs_ref.shape, F32)
        ns_ref[...] = jnp.zeros(ns_ref.shape, F32)
        for s in range(g_tiles):
            for g in range(ns):
                for h in range(B_HEADS):
                    cs_ref[s, g * B_HEADS + h, :B_HEAD_DIM, :B_HEAD_DIM] = c0_ref[s, g, h]
                    ns_ref[s, g * B_HEADS + h, 0:1, :B_HEAD_DIM] = n0_ref[s, g, h:h + 1, :]
        if ns == 1:
            for s in range(g_tiles):
                for h in range(B_HEADS):
                    mp_ref[s, h:h + 1, :] = jnp.broadcast_to(m0_ref[s, 0:1, h:h + 1], (1, LANE))
        else:
            mp_ref[...] = m0_ref[...]

    for s in range(g_tiles):
        at = lambda r: r.at[s]
        if ns == 1:
            _mlstm_chunk(at(q_ref), at(k_ref), at(v_ref), at(so_ref), at(gcol_ref), at(grow_ref), nw_ref,
                         at(yb_ref), at(cout_ref), at(nout_ref), at(mout_ref), at(cs_ref), at(ns_ref), at(mp_ref))
        else:
            _mlstm_tile(at(q_ref), at(k_ref), at(v_ref), at(so_ref), at(gcol_ref), at(grow_ref), nw_ref,
                        at(yb_ref), at(cout_ref), at(nout_ref), at(mout_ref),
                        at(cs_ref), at(ns_ref), at(mp_ref), at(col_ref), at(sv_ref), at(qc_ref), at(nr_ref),
                        ns=ns, ls=ls)


def _mlstm_chunk(q_ref, k_ref, v_ref, so_ref, gcol_ref, grow_ref, nw_ref,
                 yb_ref, cout_ref, nout_ref, mout_ref, cs_ref, ns_ref, mp_ref):
    L = CHUNK
    row = lax.broadcasted_iota(jnp.int32, (L, L), 0)
    col = lax.broadcasted_iota(jnp.int32, (L, L), 1)
    causal = col <= row
    cum = jnp.where(causal, 1.0, 0.0).astype(BF16)
    gcol = gcol_ref[...]
    grow = grow_ref[...]
    brow = sum(_dot_nt(t, cum) for t in _split3(grow))
    ones = jnp.ones((L, HP), BF16)
    mout_ref[...] = jnp.zeros(mout_ref.shape, F32)
    for h in range(B_HEADS):
        hs = slice(h * HP, (h + 1) * HP)
        qh = q_ref[:, hs].astype(BF16)
        kh = k_ref[:, hs].astype(BF16)
        vh = v_ref[:, hs].astype(BF16)
        ig = jnp.broadcast_to(gcol[:, h:h + 1], (L, HP))
        lf = jnp.broadcast_to(gcol[:, B_HEADS + h:B_HEADS + h + 1], (L, HP))
        bc = sum(_dot(cum, t) for t in _split3(lf))
        bm = bc + mp_ref[h:h + 1, :]
        logd = jnp.where(causal, grow[h:h + 1, :] + bc - brow[B_HEADS + h:B_HEADS + h + 1, :], -jnp.inf)
        m_t = jnp.maximum(bm, jnp.max(logd, axis=-1, keepdims=True))
        s = (_dot_nt(qh, kh) * jnp.exp(logd - m_t)).astype(BF16)
        sv = _dot(s, jnp.concatenate([vh, ones], axis=1))
        gdec = jnp.exp(bm - m_t)

        c_old = cs_ref[h]
        n_old = ns_ref[h, 0:1, :]
        num = gdec * _dot_nt(qh, c_old.astype(BF16)) + sv[:, 0:HP]
        den = gdec * jnp.sum(qh.astype(F32) * n_old, axis=-1, keepdims=True) + sv[:, HP:2 * HP]
        hh = num / jnp.maximum(jnp.abs(den), jnp.exp(-m_t))
        yb_ref[:, hs] = so_ref[:, hs] * _rms(hh, nw_ref[:, hs], B_HEAD_DIM)

        m_new, b_last, decay = m_t[L - 1:L, :], bc[L - 1:L, :], gdec[L - 1:L, :]
        w = jnp.exp(ig + b_last - bc - m_new)
        c_new = decay * c_old + _dot_tn((vh.astype(F32) * w).astype(BF16), kh)
        n_new = decay * n_old + jnp.sum(kh.astype(F32) * w, axis=0, keepdims=True)
        cs_ref[h] = c_new
        ns_ref[h, 0:1, :] = n_new
        mp_ref[h:h + 1, :] = m_new
        mout_ref[L - 1:L, h:h + 1] = m_new[:, 0:1]
        cout_ref[0, h] = c_new[:B_HEAD_DIM, :B_HEAD_DIM]
        nout_ref[0, h:h + 1, :] = n_new[:, :B_HEAD_DIM]


def _mlstm_tile(q_ref, k_ref, v_ref, so_ref, gcol_ref, grow_ref, nw_ref,
                yb_ref, cout_ref, nout_ref, mout_ref,
                cs_ref, ns_ref, mp_ref, col_ref, sv_ref, qc_ref, nr_ref, *, ns, ls):
    L = CHUNK
    ls_shift = ls.bit_length() - 1

    col_ref[...] = jnp.zeros(col_ref.shape, F32)
    row = lax.broadcasted_iota(jnp.int32, (L, L), 0)
    col = lax.broadcasted_iota(jnp.int32, (L, L), 1)
    causal = (col <= row) & ((col >> ls_shift) == (row >> ls_shift))
    cum = jnp.where(causal, 1.0, 0.0).astype(BF16)
    rid = lax.broadcasted_iota(jnp.int32, (L, 1), 0)

    gcol = gcol_ref[...]
    grow = grow_ref[...]
    bcol = sum(_dot(cum, t) for t in _split3(gcol))
    brow = sum(_dot_nt(t, cum) for t in _split3(grow))
    mp = mp_ref[...]

    qs, ks, vs = [], [], []
    for h in range(B_HEADS):
        hs = slice(h * HP, (h + 1) * HP)
        qh = q_ref[:, hs].astype(BF16)
        kh = k_ref[:, hs].astype(BF16)
        vh = v_ref[:, hs].astype(BF16)
        qs.append(qh), ks.append(kh), vs.append(vh)
        ig_r = grow[h:h + 1, :]
        b_r = brow[B_HEADS + h:B_HEADS + h + 1, :]
        b_c = bcol[:, B_HEADS + h:B_HEADS + h + 1]
        bm = b_c + mp[:, h:h + 1]
        logd = jnp.where(causal, ig_r + b_c - b_r, -jnp.inf)
        m_t = jnp.maximum(bm, jnp.max(logd, axis=-1, keepdims=True))
        d = jnp.exp(logd - m_t)
        s = _dot_nt(qh, kh) * d
        sv_ref[h] = _dot(s.astype(BF16), vh)
        col_ref[:, _COL_M + h:_COL_M + h + 1] = m_t
        col_ref[:, _COL_G + h:_COL_G + h + 1] = jnp.exp(bm - m_t)
        col_ref[:, _COL_RS + h:_COL_RS + h + 1] = jnp.sum(s, axis=-1, keepdims=True)

    qc_ref[...] = jnp.zeros(qc_ref.shape, F32)
    nr_ref[...] = jnp.zeros(nr_ref.shape, F32)

    def seq_update(g):
        last = g * ls + (ls - 1)
        in_seq = (rid >> ls_shift) == g
        is_last = rid == last
        for h in range(B_HEADS):
            c_old = cs_ref[g * B_HEADS + h]
            n_old = ns_ref[g * B_HEADS + h, 0:1, :]
            qm = jnp.where(in_seq, qs[h], jnp.zeros_like(qs[h])) if ns > 1 else qs[h]
            qc_ref[h] += _dot_nt(qm, c_old.astype(BF16))
            nr_ref[h] += jnp.where(in_seq, n_old, 0.0)
            m_t = col_ref[:, _COL_M + h:_COL_M + h + 1]
            gdec = col_ref[:, _COL_G + h:_COL_G + h + 1]
            b_c = bcol[:, B_HEADS + h:B_HEADS + h + 1]
            ig_c = gcol[:, h:h + 1]
            pick = lambda a: jnp.sum(jnp.where(is_last, a, 0.0), axis=0, keepdims=True)
            m_new, b_last, decay = pick(m_t), pick(b_c), pick(gdec)
            w_c = jnp.where(in_seq, jnp.exp(ig_c + b_last - b_c - m_new), 0.0)
            vw = (vs[h].astype(F32) * w_c).astype(BF16)
            cs_ref[g * B_HEADS + h] = decay * c_old + _dot_tn(vw, ks[h])
            ns_ref[g * B_HEADS + h, 0:1, :] = (decay * n_old
                                               + jnp.sum(ks[h].astype(F32) * w_c, axis=0, keepdims=True))
            if ns == 1:
                mp_ref[:, h:h + 1] = jnp.broadcast_to(m_new, (L, 1))

    if ns == 1:
        seq_update(0)
    else:
        lax.fori_loop(0, ns, lambda g, c: (seq_update(g), c)[1], 0)

    for h in range(B_HEADS):
        hs = slice(h * HP, (h + 1) * HP)
        m_t = col_ref[:, _COL_M + h:_COL_M + h + 1]
        gdec = col_ref[:, _COL_G + h:_COL_G + h + 1]
        rs = col_ref[:, _COL_RS + h:_COL_RS + h + 1]
        num = gdec * qc_ref[h] + sv_ref[h]
        qn = jnp.sum(qs[h].astype(F32) * nr_ref[h], axis=-1, keepdims=True)
        den = gdec * qn + rs
        hh = num / jnp.maximum(jnp.abs(den), jnp.exp(-m_t))
        yb_ref[:, hs] = so_ref[:, hs] * _rms(hh, nw_ref[:, hs], B_HEAD_DIM)

    mout_ref[...] = col_ref[...]
    for g in range(ns):
        for h in range(B_HEADS):
            cout_ref[g, h] = cs_ref[g * B_HEADS + h, :B_HEAD_DIM, :B_HEAD_DIM]
            nout_ref[g, h:h + 1, :] = ns_ref[g * B_HEADS + h, 0:1, :B_HEAD_DIM]


def _mlstm(q, k, v, so, gcol, grow, m0, c0, n0, nw, *, ns, n_tiles, n_chunks, g_tiles):
    m = q.shape[0]
    nb = c0.shape[0]
    ls = CHUNK // ns
    g = g_tiles
    t3 = lambda a: a.reshape(n_tiles, n_chunks * CHUNK, a.shape[-1])
    rows = lambda w: pl.BlockSpec((g, CHUNK, w), lambda t, c: (t, c, 0))
    cspec = pl.BlockSpec((g, ns, B_HEADS, B_HEAD_DIM, B_HEAD_DIM), lambda t, c: (t, 0, 0, 0, 0))
    nspec = pl.BlockSpec((g, ns, B_HEADS, B_HEAD_DIM), lambda t, c: (t, 0, 0, 0))
    mspec = pl.BlockSpec((g, CHUNK, LANE), lambda t, c: (t, 0, 0))
    yb, c_new, n_new, m_rows = pl.pallas_call(
        functools.partial(_mlstm_kernel, ns=ns, ls=ls, g_tiles=g),
        grid=(n_tiles // g, n_chunks),
        in_specs=[rows(BW_P), rows(BW_P), rows(BW_P), rows(BW_P), rows(LANE),
                  pl.BlockSpec((g, None, 16, CHUNK), lambda t, c: (t, c, 0, 0)),
                  mspec, cspec, nspec, _wspec((1, BW_P))],
        out_specs=(rows(BW_P), cspec, nspec, mspec),
        out_shape=(jax.ShapeDtypeStruct((n_tiles, n_chunks * CHUNK, BW_P), F32),
                   jax.ShapeDtypeStruct((n_tiles, ns, B_HEADS, B_HEAD_DIM, B_HEAD_DIM), F32),
                   jax.ShapeDtypeStruct((n_tiles, ns, B_HEADS, B_HEAD_DIM), F32),
                   jax.ShapeDtypeStruct((n_tiles, CHUNK, LANE), F32)),
        scratch_shapes=[pltpu.VMEM((g, ns * B_HEADS, HP, HP), F32), pltpu.VMEM((g, ns * B_HEADS, 8, HP), F32),
                        pltpu.VMEM((g, CHUNK, LANE), F32), pltpu.VMEM((g, CHUNK, LANE), F32),
                        pltpu.VMEM((g, B_HEADS, CHUNK, HP), F32), pltpu.VMEM((g, B_HEADS, CHUNK, HP), F32),
                        pltpu.VMEM((g, B_HEADS, CHUNK, HP), F32)],
        compiler_params=_params(2),
        name="mlstm",
    )(t3(q), t3(k), t3(v), t3(so), t3(gcol), grow.reshape(n_tiles, n_chunks, 16, CHUNK), m0,
      c0.reshape(n_tiles, ns, B_HEADS, B_HEAD_DIM, B_HEAD_DIM), n0.reshape(n_tiles, ns, B_HEADS, B_HEAD_DIM), nw)
    return (yb.reshape(m, BW_P), c_new.reshape(nb, B_HEADS, B_HEAD_DIM, B_HEAD_DIM),
            n_new.reshape(nb, B_HEADS, B_HEAD_DIM), m_rows)


def _stack_heads(qc_ref, dtype):
    return jnp.concatenate([qc_ref[:, QC_SLOT * h:QC_SLOT * h + LATENT] for h in range(C_HEADS)],
                           axis=0).astype(dtype)


def _unstack_project(o, t, wuv_ref):
    wide = jnp.concatenate([o[h * t:(h + 1) * t] for h in range(C_HEADS)], axis=1)
    return _dot(wide.astype(BF16), wuv_ref[...])


def _softmax_step(s, pv, m_ref, acc_ref):
    reps = s.shape[1] // LANE
    m_prev = m_ref[...]
    m_new = jnp.maximum(m_prev, jnp.max(s, axis=-1, keepdims=True))
    p = jnp.exp(s - jnp.concatenate([m_new] * reps, axis=1))
    corr = jnp.exp(m_prev - m_new)
    acc_ref[...] = jnp.concatenate([corr, corr], axis=1) * acc_ref[...] + pv(p.astype(BF16))
    m_ref[...] = m_new


def _softmax_result(acc_ref):
    acc = acc_ref[...]
    return acc[:, :KV_LORA] / acc[:, KV_LORA:]


def _mla_prompt_kernel(qc_ref, lat_ref, wuv_ref, yc_ref, kb_ref, vb_ref, m_ref, acc_ref, *, tq, tk):
    qi = pl.program_id(1)

    @pl.when(qi == 0)
    def _cast_keys():
        lat = lat_ref[...]
        kb_ref[...] = lat.astype(BF16)
        vb_ref[:, 0:KV_LORA] = lat[:, 0:KV_LORA].astype(BF16)
        vb_ref[:, KV_LORA:2 * KV_LORA] = jnp.ones((lat.shape[0], KV_LORA), BF16)

    qst = _stack_heads(qc_ref, BF16)
    m_ref[...] = jnp.full(m_ref.shape, -jnp.inf, F32)
    acc_ref[...] = jnp.zeros(acc_ref.shape, F32)
    rows = C_HEADS * tq

    def block(j, masked):
        start = pl.multiple_of(j * tk, tk)
        s = _dot_nt(qst, kb_ref[pl.ds(start, tk), :])
        if masked:
            qpos = qi * tq + (lax.broadcasted_iota(jnp.int32, (rows, tk), 0) & (tq - 1))
            kpos = start + lax.broadcasted_iota(jnp.int32, (rows, tk), 1)
            s = jnp.where(kpos <= qpos, s, -jnp.inf)
        _softmax_step(s, lambda p: _dot(p, vb_ref[pl.ds(start, tk), :]), m_ref, acc_ref)

    n_full = (qi * tq) // tk
    lax.fori_loop(0, n_full, lambda j, c: (block(j, False), c)[1], 0)
    block(n_full, True)
    yc_ref[...] = _unstack_project(_softmax_result(acc_ref), tq, wuv_ref)


def _mla_prompt(qc, lat, wuv, *, nb, seq, tq, tk):
    m = qc.shape[0]
    n_q = seq // tq
    rows = C_HEADS * tq
    return pl.pallas_call(
        functools.partial(_mla_prompt_kernel, tq=tq, tk=tk),
        grid=(nb, n_q),
        in_specs=[pl.BlockSpec((tq, C_HEADS * QC_SLOT), lambda b, i: (b * n_q + i, 0)),
                  pl.BlockSpec((seq, LATENT), lambda b, i: (b, 0)),
                  _wspec((C_HEADS * KV_LORA, C_WIDTH))],
        out_specs=pl.BlockSpec((tq, C_WIDTH), lambda b, i: (b * n_q + i, 0)),
        out_shape=jax.ShapeDtypeStruct((m, C_WIDTH), F32),
        scratch_shapes=[pltpu.VMEM((seq, LATENT), BF16), pltpu.VMEM((seq, 2 * KV_LORA), BF16),
                        pltpu.VMEM((rows, LANE), F32), pltpu.VMEM((rows, 2 * KV_LORA), F32)],
        compiler_params=_params(2),
        name="mla_prompt",
    )(qc, lat, wuv)


def _mla_sample_kernel(pt_ref, qc_ref, latn_ref, wuv_ref, cache_ref, yc_ref, buf_ref, sem_ref, kt_ref, kn_ref,
                       *, layer, n_pages, t_new):
    b = pl.program_id(0)
    nb = pl.num_programs(0)
    slot = b & 1
    rows = C_HEADS * t_new

    def page_copy(seq, page, to_slot):
        return pltpu.make_async_copy(cache_ref.at[layer, pt_ref[seq, page]], buf_ref.at[to_slot, page],
                                     sem_ref.at[to_slot])

    def request(seq, to_slot):
        for page in range(n_pages):
            page_copy(seq, page, to_slot).start()

    @pl.when(b == 0)
    def _first():
        request(0, 0)

    @pl.when(b + 1 < nb)
    def _prefetch_next():
        request(b + 1, 1 - slot)

    for page in range(n_pages):
        page_copy(b, page, slot).wait()

    qst = _stack_heads(qc_ref, BF16)
    for page in range(n_pages):
        kt_ref[:, page * PAGE_SIZE:(page + 1) * PAGE_SIZE] = buf_ref[slot, page].astype(BF16)
    s_past = _dot(qst, kt_ref[...])

    kn_ref[...] = jnp.zeros(kn_ref.shape, F32)
    kn_ref[0:t_new, :] = latn_ref[...]
    kn = kn_ref[...].astype(BF16)
    tpos = lax.broadcasted_iota(jnp.int32, (rows, PAGE_SIZE), 0) & (t_new - 1)
    kidx = lax.broadcasted_iota(jnp.int32, (rows, PAGE_SIZE), 1)
    s_new = jnp.where(kidx <= tpos, _dot_nt(qst, kn), -jnp.inf)

    m = jnp.maximum(jnp.max(s_past, axis=-1, keepdims=True), jnp.max(s_new, axis=-1, keepdims=True))
    p_past = jnp.exp(s_past - m).astype(BF16)
    p_new = jnp.exp(s_new - m).astype(BF16)
    denom = (jnp.sum(p_past.astype(F32), axis=-1, keepdims=True)
             + jnp.sum(p_new.astype(F32), axis=-1, keepdims=True))
    o = _dot_nt(p_past, kt_ref[0:KV_LORA, :]) + _dot(p_new, kn[:, 0:KV_LORA])
    yc_ref[...] = _unstack_project(o / denom, t_new, wuv_ref)


def _mla_sample(page_table, qc, lat_new, wuv, cache_t, *, layer, t_new):
    m = qc.shape[0]
    nb, n_pages = page_table.shape
    grid_spec = pltpu.PrefetchScalarGridSpec(
        num_scalar_prefetch=1,
        grid=(nb,),
        in_specs=[pl.BlockSpec((t_new, C_HEADS * QC_SLOT), lambda b, pt: (b, 0)),
                  pl.BlockSpec((t_new, LATENT), lambda b, pt: (b, 0)),
                  pl.BlockSpec((C_HEADS * KV_LORA, C_WIDTH), lambda b, pt: (0, 0)),
                  pl.BlockSpec(memory_space=pl.ANY)],
        out_specs=pl.BlockSpec((t_new, C_WIDTH), lambda b, pt: (b, 0)),
        scratch_shapes=[pltpu.VMEM((2, n_pages, LATENT, PAGE_SIZE), F32), pltpu.SemaphoreType.DMA((2,)),
                        pltpu.VMEM((LATENT, n_pages * PAGE_SIZE), BF16), pltpu.VMEM((PAGE_SIZE, LATENT), F32)])
    return pl.pallas_call(
        functools.partial(_mla_sample_kernel, layer=layer, n_pages=n_pages, t_new=t_new),
        grid_spec=grid_spec,
        out_shape=jax.ShapeDtypeStruct((m, C_WIDTH), F32),
        compiler_params=_params(1),
        name="mla_sample",
    )(page_table, qc, lat_new, wuv, cache_t)


def _outproj_kernel(x_ref, ya_ref, yb_ref, yc_ref, wo_ref, nxw_ref, wq_ref, xo_ref, q_ref):
    y = (_dot(ya_ref[...].astype(BF16), wo_ref[0:A_WIDTH, :])
         + _dot(yb_ref[...].astype(BF16), wo_ref[A_WIDTH:A_WIDTH + BW_P, :])
         + _dot(yc_ref[...].astype(BF16), wo_ref[A_WIDTH + BW_P:N_MIX_P, :]))
    xn = x_ref[...] + y
    xo_ref[...] = xn
    hq = _rms(xn, nxw_ref[...], D_MODEL).astype(BF16)
    q_ref[...] = (_dot(hq, wq_ref[...]) * XA_SCALE).astype(q_ref.dtype)


def _outproj(x, ya, yb, yc, wo, nxw, wq, *, tm, q_dtype):
    m = x.shape[0]
    row = lambda w: pl.BlockSpec((tm, w), lambda i: (i, 0))
    return pl.pallas_call(
        _outproj_kernel,
        grid=(m // tm,),
        in_specs=[row(D_MODEL), row(A_WIDTH), row(BW_P), row(C_WIDTH), _wspec((N_MIX_P, D_MODEL)),
                  _wspec((1, D_MODEL)), _wspec((D_MODEL, D_MODEL))],
        out_specs=(row(D_MODEL), row(D_MODEL)),
        out_shape=(jax.ShapeDtypeStruct((m, D_MODEL), F32), jax.ShapeDtypeStruct((m, D_MODEL), q_dtype)),
        compiler_params=_params(1),
        name="outproj",
    )(x, ya, yb, yc, wo, nxw, wq)


def _memkv_kernel(mem_ref, nw_ref, wk_ref, wv_ref, k_ref, v_ref, kb_ref, vb_ref):
    mn = _rms(mem_ref[...], nw_ref[...], D_MODEL).astype(BF16)
    k = _dot(mn, wk_ref[...])
    v = _dot(mn, wv_ref[...])
    kb_ref[...] = k.astype(BF16)
    vb_ref[...] = v.astype(BF16)
    for h in range(X_HEADS):
        hs = slice(h * X_HEAD_DIM, (h + 1) * X_HEAD_DIM)
        k_ref[:, h, :] = k[:, hs]
        v_ref[:, h, :] = v[:, hs]


def _memkv(mem, nw, wk, wv, *, tm):
    m = mem.shape[0]
    row = pl.BlockSpec((tm, D_MODEL), lambda i: (i, 0))
    out = pl.BlockSpec((tm, X_HEADS, X_HEAD_DIM), lambda i: (i, 0, 0))
    return pl.pallas_call(
        _memkv_kernel,
        grid=(m // tm,),
        in_specs=[row, _wspec((1, D_MODEL)), _wspec((D_MODEL, D_MODEL)), _wspec((D_MODEL, D_MODEL))],
        out_specs=(out, out, row, row),
        out_shape=(jax.ShapeDtypeStruct((m, X_HEADS, X_HEAD_DIM), F32),) * 2
                  + (jax.ShapeDtypeStruct((m, D_MODEL), BF16),) * 2,
        compiler_params=_params(1),
        name="memkv",
    )(mem, nw, wk, wv)


def _softmax_rows(s):
    p = jnp.exp(s - jnp.max(s, axis=-1, keepdims=True))
    return p / jnp.sum(p, axis=-1, keepdims=True)


def _xattn_prompt_kernel(x_ref, q_ref, k_ref, v_ref, wo_ref, xo_ref, o_ref):
    for h in range(X_HEADS):
        hs = slice(h * X_HEAD_DIM, (h + 1) * X_HEAD_DIM)
        p = _softmax_rows(_dot_nt(q_ref[:, hs], k_ref[:, hs]))
        o_ref[:, hs] = _dot(p.astype(BF16), v_ref[:, hs])
    xo_ref[...] = x_ref[...] + _dot(o_ref[...].astype(BF16), wo_ref[...])


def _xattn_prompt(x, q, k, v, wo, *, tm, tiles_per_seq):
    m = x.shape[0]
    row = pl.BlockSpec((tm, D_MODEL), lambda i: (i, 0))
    kv = pl.BlockSpec((MEM_LEN, D_MODEL), lambda i: (i // tiles_per_seq, 0))
    return pl.pallas_call(
        _xattn_prompt_kernel,
        grid=(m // tm,),
        in_specs=[row, row, kv, kv, _wspec((D_MODEL, D_MODEL))],
        out_specs=row,
        out_shape=jax.ShapeDtypeStruct((m, D_MODEL), F32),
        scratch_shapes=[pltpu.VMEM((tm, D_MODEL), F32)],
        compiler_params=_params(1),
        name="xattn_prompt",
    )(x, q, k, v, wo)


def _xattn_sample_kernel(x_ref, q_ref, k_ref, v_ref, wo_ref, xo_ref, o_ref, *, n_seq, t_rows):
    rows = X_HEADS * t_rows
    nk = MEM_LEN * X_HEADS
    r = lax.broadcasted_iota(jnp.int32, (rows, nk), 0)
    c = lax.broadcasted_iota(jnp.int32, (rows, nk), 1)
    keep = (c & (X_HEADS - 1)) == (r >> (t_rows.bit_length() - 1))
    for g in range(n_seq):
        rs = slice(g * t_rows, (g + 1) * t_rows)
        qst = jnp.concatenate([q_ref[rs, h * X_HEAD_DIM:(h + 1) * X_HEAD_DIM] for h in range(X_HEADS)], axis=0)
        k2 = k_ref[g].reshape(nk, X_HEAD_DIM).astype(BF16)
        v2 = v_ref[g].reshape(nk, X_HEAD_DIM).astype(BF16)
        p = _softmax_rows(jnp.where(keep, _dot_nt(qst.astype(BF16), k2), -jnp.inf))
        o = _dot(p.astype(BF16), v2)
        for h in range(X_HEADS):
            o_ref[rs, h * X_HEAD_DIM:(h + 1) * X_HEAD_DIM] = o[h * t_rows:(h + 1) * t_rows]
    xo_ref[...] = x_ref[...] + _dot(o_ref[...].astype(BF16), wo_ref[...])


def _xattn_sample(x, q, k, v, wo, *, layer, n_seq, t_rows):
    m = x.shape[0]
    r = n_seq * t_rows
    row = pl.BlockSpec((r, D_MODEL), lambda i: (i, 0))
    kv = pl.BlockSpec((None, n_seq, MEM_LEN, X_HEADS, X_HEAD_DIM), lambda i: (layer, i, 0, 0, 0))
    return pl.pallas_call(
        functools.partial(_xattn_sample_kernel, n_seq=n_seq, t_rows=t_rows),
        grid=(m // r,),
        in_specs=[row, row, kv, kv, _wspec((D_MODEL, D_MODEL))],
        out_specs=row,
        out_shape=jax.ShapeDtypeStruct((m, D_MODEL), F32),
        scratch_shapes=[pltpu.VMEM((r, D_MODEL), F32)],
        compiler_params=_params(1),
        name="xattn_sample",
    )(x, q, k, v, wo)


def _ffn_kernel(x_ref, nw_ref, wg_ref, wu_ref, wd_ref, cw_ref, cb_ref, p0_ref, p1_ref, fnw_ref, *rest,
                seq_rows, tiles_per_seq, final_norm):
    if final_norm:
        xo_ref, gtail_ref, y_ref, carry_ref, act_ref = rest
    else:
        xo_ref, gtail_ref, carry_ref, act_ref = rest
    tm = x_ref.shape[0]
    x = x_ref[...]
    xn = _rms(x, nw_ref[...], D_MODEL).astype(BF16)
    rid = lax.broadcasted_iota(jnp.int32, (tm, 1), 0)
    if seq_rows:
        pos = rid & (seq_rows - 1)
    else:
        pos = rid

        @pl.when(pl.program_id(0) % tiles_per_seq == 0)
        def _seq_start():
            carry_ref[0:1, :] = p0_ref[...]
            carry_ref[1:2, :] = p1_ref[...]

    for c in range(N_FF_CHUNKS):
        cs = slice(c * FF_CHUNK, (c + 1) * FF_CHUNK)
        g = _dot(xn, wg_ref[:, cs])
        u = _dot(xn, wu_ref[:, cs])
        if seq_rows:
            prev0, prev1 = p0_ref[:, cs], p1_ref[:, cs]
            gtail_ref[:, cs] = g
        else:
            prev0, prev1 = carry_ref[0:1, cs], carry_ref[1:2, cs]
            gtail_ref[:, cs] = g[tm - 2:tm, :]
            carry_ref[0:2, cs] = g[tm - 2:tm, :]
        g1 = jnp.where(pos == 0, prev1, pltpu.roll(g, 1, 0))
        g2 = jnp.where(pos == 0, prev0, jnp.where(pos == 1, prev1, pltpu.roll(g, 2, 0)))
        y = cb_ref[:, cs] + g2 * cw_ref[0:1, cs] + g1 * cw_ref[1:2, cs] + g * cw_ref[2:3, cs]
        act_ref[:, cs] = (_gelu(y) * u).astype(BF16)
    xo = x + _dot(act_ref[...], wd_ref[...])
    xo_ref[...] = xo
    if final_norm:
        y_ref[...] = _rms(xo, fnw_ref[...], D_MODEL)


def _ffn(x, nw, wg, wu, wd, cw, cb, p0, p1, fnw, *, tm, seq_rows, tiles_per_seq, final_norm):
    m = x.shape[0]
    n_tiles = m // tm
    row = pl.BlockSpec((tm, D_MODEL), lambda i: (i, 0))
    if seq_rows:
        pspec = pl.BlockSpec((tm, D_FF), lambda i: (i, 0))
        gspec = pspec
        gshape = jax.ShapeDtypeStruct((m, D_FF), F32)
    else:
        pspec = pl.BlockSpec((None, 1, D_FF), lambda i: (i // tiles_per_seq, 0, 0))
        gspec = pl.BlockSpec((None, 2, D_FF), lambda i: (i, 0, 0))
        gshape = jax.ShapeDtypeStruct((n_tiles, 2, D_FF), F32)
    return pl.pallas_call(
        functools.partial(_ffn_kernel, seq_rows=seq_rows, tiles_per_seq=tiles_per_seq, final_norm=final_norm),
        grid=(n_tiles,),
        in_specs=[row, _wspec((1, D_MODEL)), _wspec((D_MODEL, D_FF)), _wspec((D_MODEL, D_FF)),
                  _wspec((D_FF, D_MODEL)), _wspec((CONV_W, D_FF)), _wspec((1, D_FF)),
                  pspec, pspec, _wspec((1, D_MODEL))],
        out_specs=(row, gspec) + ((row,) if final_norm else ()),
        out_shape=(jax.ShapeDtypeStruct((m, D_MODEL), F32), gshape)
                  + ((jax.ShapeDtypeStruct((m, D_MODEL), F32),) if final_norm else ()),
        scratch_shapes=[pltpu.VMEM((8, D_FF), F32), pltpu.VMEM((tm, D_FF), BF16)],
        compiler_params=_params(1),
        name="ffn",
    )(x, nw, wg, wu, wd, cw, cb, p0, p1, fnw)


def _pad_heads_idx(start):
    idx = np.full((BW_P,), -1, np.int64)
    for h in range(B_HEADS):
        idx[h * HP:h * HP + B_HEAD_DIM] = start + h * B_HEAD_DIM + np.arange(B_HEAD_DIM)
    return idx


def _take_cols(w, idx, axis=-1):
    axis = axis % w.ndim
    pieces, i, n = [], 0, len(idx)
    while i < n:
        j = i + 1
        if idx[i] < 0:
            while j < n and idx[j] < 0:
                j += 1
            shape = w.shape[:axis] + (j - i,) + w.shape[axis + 1:]
            pieces.append(jnp.zeros(shape, w.dtype))
        else:
            while j < n and idx[j] == idx[j - 1] + 1:
                j += 1
            pieces.append(lax.slice_in_dim(w, int(idx[i]), int(idx[i]) + (j - i), axis=axis))
        i = j
    return jnp.concatenate(pieces, axis=axis)


def _win_layout():
    half = C_ROPE // 2
    base = {'au': 0, 'av': 256, 'bq': 512, 'bk': 896, 'bv': 1280, 'bo': 1664, 'bi': 2048, 'bf': 2052,
            'cq': 2056, 'ckv': 2248, 'ckr': 2376}
    idx = np.full((N_IN,), -1, np.int64)
    idx[S_AU:S_AU + 256] = base['au'] + np.arange(256)
    idx[S_AV:S_AV + 256] = base['av'] + np.arange(256)
    for s, name in ((S_BQ, 'bq'), (S_BK, 'bk'), (S_BV, 'bv'), (S_BO, 'bo')):
        idx[s:s + BW_P] = _pad_heads_idx(base[name])
    idx[S_CQ:S_CQ + Q_LORA] = base['cq'] + np.arange(Q_LORA)
    idx[S_CKV:S_CKV + KV_LORA] = base['ckv'] + np.arange(KV_LORA)
    idx[S_R1:S_R1 + C_ROPE] = base['ckr'] + np.arange(C_ROPE)
    idx[S_R2:S_R2 + half] = base['ckr'] + half + np.arange(half)
    idx[S_R2 + half:S_R2 + C_ROPE] = base['ckr'] + np.arange(half)
    idx[S_G:S_G + B_HEADS] = base['bi'] + np.arange(B_HEADS)
    idx[S_G + B_HEADS:S_G + 2 * B_HEADS] = base['bf'] + np.arange(B_HEADS)
    return idx


def _wuq_layout():
    half = C_ROPE // 2
    per = C_NOPE + C_ROPE
    idx = np.full((N_UQ,), -1, np.int64)
    for h in range(C_HEADS):
        idx[UQ_NOPE + h * C_NOPE:UQ_NOPE + (h + 1) * C_NOPE] = h * per + np.arange(C_NOPE)
        idx[UQ_RA + h * LANE:UQ_RA + h * LANE + C_ROPE] = h * per + C_NOPE + np.arange(C_ROPE)
        idx[UQ_RB + h * LANE:UQ_RB + h * LANE + half] = h * per + C_NOPE + half + np.arange(half)
        idx[UQ_RB + h * LANE + half:UQ_RB + h * LANE + C_ROPE] = h * per + C_NOPE + np.arange(half)
    return idx


def _rope_tables(pos, reps):
    half = C_ROPE // 2
    inv = ROPE_THETA ** (-jnp.arange(half, dtype=F32) / half)
    ang = pos.astype(F32)[:, None] * inv[None, :]
    cos, sin = jnp.cos(ang), jnp.sin(ang)
    z = jnp.zeros((pos.shape[0], LANE - C_ROPE), F32)
    ct = jnp.concatenate([cos, cos, z], axis=1)
    st = jnp.concatenate([-sin, sin, z], axis=1)
    return jnp.tile(ct, (reps, 1)), jnp.tile(st, (reps, 1))


def _pad_lanes(v, idx):
    return _take_cols(v[None, :], idx)


def _layer_weights(l, w):
    hp_idx = _pad_heads_idx(0)
    out = {}
    out['nw_mix'] = w['norm_mix_w'][l][None, :]
    out['w1'] = _take_cols(w['w_in'][l], _win_layout()).astype(BF16)
    out['qnw'] = _pad_lanes(w['mla_qnorm_w'][l], np.concatenate([np.arange(Q_LORA), np.full(CQ_P - Q_LORA, -1)]))
    out['kvnw'] = w['mla_kvnorm_w'][l][None, :]
    wuq = w['mla_w_uq'][l].reshape(Q_LORA, C_HEADS * (C_NOPE + C_ROPE))
    wuq = _take_cols(wuq, _wuq_layout())
    out['wuq'] = jnp.concatenate([wuq, jnp.zeros((CQ_P - Q_LORA, N_UQ), F32)], axis=0).astype(BF16)
    wuk = jnp.transpose(w['mla_w_uk'][l], (1, 2, 0))
    eye = jnp.eye(C_HEADS, dtype=F32)
    out['wuk'] = jnp.einsum('hdc,hg->hdgc', wuk, eye).reshape(C_HEADS * C_NOPE, C_HEADS * KV_LORA).astype(BF16)
    wuv = jnp.transpose(w['mla_w_uv'][l], (1, 0, 2))
    out['wuv'] = jnp.einsum('hce,hg->hcge', wuv, eye).reshape(C_HEADS * KV_LORA, C_WIDTH).astype(BF16)
    gb = jnp.concatenate([w['mlstm_bi'][l], w['mlstm_bf'][l]])
    out['gb'] = _pad_lanes(gb, np.concatenate([np.arange(2 * B_HEADS), np.full(LANE - 2 * B_HEADS, -1)]))
    out['mnw'] = _pad_lanes(w['mlstm_norm_w'][l], hp_idx)
    wo = w['w_out'][l]
    rows = np.concatenate([np.arange(A_WIDTH), np.where(hp_idx < 0, -1, hp_idx + A_WIDTH),
                           A_WIDTH + B_WIDTH + np.arange(C_WIDTH)])
    out['wo'] = _take_cols(wo, rows, axis=0).astype(BF16)
    out['nw_x'] = w['norm_x_w'][l][None, :]
    out['nw_mem'] = w['norm_mem_w'][l][None, :]
    out['xwq'] = w['xa_wq'][l].reshape(D_MODEL, D_MODEL).astype(BF16)
    out['xwk'] = w['xa_wk'][l].reshape(D_MODEL, D_MODEL).astype(BF16)
    out['xwv'] = w['xa_wv'][l].reshape(D_MODEL, D_MODEL).astype(BF16)
    out['xwo'] = w['xa_wo'][l].reshape(D_MODEL, D_MODEL).astype(BF16)
    out['nw_ffn'] = w['norm_ffn_w'][l][None, :]
    out['wg'] = w['ffn_wg'][l].astype(BF16)
    out['wu'] = w['ffn_wu'][l].astype(BF16)
    out['wd'] = w['ffn_wd'][l].astype(BF16)
    out['cw'] = w['ffn_conv_w'][l]
    out['cb'] = w['ffn_conv_b'][l][None, :]
    return out


def _chunk_mixers(ws, cb, t_rows):
    reps = CHUNK // t_rows
    wt = ws[:, :t_rows, :t_rows]
    if reps > 1:
        wt = jnp.einsum('ab,hts->hatbs', jnp.eye(reps, dtype=F32), wt).reshape(A_HEADS, CHUNK, CHUNK)
    bias = jnp.tile(jnp.repeat(jnp.transpose(cb[:, :t_rows]), A_HEAD_DIM, axis=1), (reps, 1))
    return wt, bias


def _mixer_block(x, lw, *, mlstm_state, pos_tables, chunk_w, tm, act_dtype, ns, n_tiles, n_chunks, mla_fn):
    m = x.shape[0]
    cos_t, sin_t = pos_tables
    gu, gv, q, k, v, so, qc, lat, gate = _inproj(x, lw['nw_mix'], lw['w1'], lw['qnw'], lw['kvnw'], lw['wuq'],
                                                 lw['wuk'], lw['gb'], cos_t, sin_t, tm=tm, act_dtype=act_dtype)
    ya = _chunkmlp(gu, gv, chunk_w[0], chunk_w[1], tm=tm)
    grow = jnp.transpose(gate[:, :16].reshape(m // CHUNK, CHUNK, 16), (0, 2, 1))
    c0, n0, m0 = mlstm_state
    yb, c_new, n_new, m_rows = _mlstm(q, k, v, so, gate, grow, m0, c0, n0, lw['mnw'], ns=ns, n_tiles=n_tiles,
                                      n_chunks=n_chunks, g_tiles=2 if (ns == 1 and n_tiles % 2 == 0) else 1)
    yc = mla_fn(qc, lat)
    return (ya, yb, yc), (c_new, n_new, m_rows), lat, gv


def kernel(x_prompt, x_sample, mem_prompt, cache_mla, page_table, cache_mem_k, cache_mem_v, state_mlstm_C, state_mlstm_n, state_mlstm_m, state_ffn_conv, norm_mix_w, w_in, chunk_ws, chunk_b, mlstm_bi, mlstm_bf, mlstm_norm_w, mla_qnorm_w, mla_kvnorm_w, mla_w_uq, mla_w_uk, mla_w_uv, w_out, norm_x_w, norm_mem_w, xa_wq, xa_wk, xa_wv, xa_wo, norm_ffn_w, ffn_wg, ffn_wu, ffn_conv_w, ffn_conv_b, ffn_wd, norm_final_w):
    w = dict(norm_mix_w=norm_mix_w, w_in=w_in, mlstm_bi=mlstm_bi, mlstm_bf=mlstm_bf, mlstm_norm_w=mlstm_norm_w,
             mla_qnorm_w=mla_qnorm_w, mla_kvnorm_w=mla_kvnorm_w, mla_w_uq=mla_w_uq, mla_w_uk=mla_w_uk,
             mla_w_uv=mla_w_uv, w_out=w_out, norm_x_w=norm_x_w, norm_mem_w=norm_mem_w, xa_wq=xa_wq, xa_wk=xa_wk,
             xa_wv=xa_wv, xa_wo=xa_wo, norm_ffn_w=norm_ffn_w, ffn_wg=ffn_wg, ffn_wu=ffn_wu,
             ffn_conv_w=ffn_conv_w, ffn_conv_b=ffn_conv_b, ffn_wd=ffn_wd)
    depth = w_in.shape[0]
    nbp, S, _ = x_prompt.shape
    nbs, T, _ = x_sample.shape
    n_pages = page_table.shape[1]
    past_len = n_pages * PAGE_SIZE
    mp, ms = nbp * S, nbs * T
    assert S % CHUNK == 0 and CHUNK % T == 0 and T % 8 == 0 and ms % CHUNK == 0

    tm_p = min(512, S)
    tm_s = min(512, ms)
    tq = min(256, S)
    tk = min(512, S)
    ns_s = CHUNK // T
    xg = 4 if nbs % 4 == 0 else 1
    tm_ffn_p = min(512, S)
    tm_ffn_s = min(256, ms)

    pos_p = _rope_tables(jnp.arange(S), 1)
    pos_s = _rope_tables(past_len + jnp.arange(T), tm_s // T)
    fnw = norm_final_w[None, :]

    xp = x_prompt.reshape(mp, D_MODEL)
    xs = x_sample.reshape(ms, D_MODEL)
    mem = mem_prompt.reshape(nbp * MEM_LEN, D_MODEL)

    zero_c = jnp.zeros((nbp, B_HEADS, B_HEAD_DIM, B_HEAD_DIM), F32)
    zero_n = jnp.zeros((nbp, B_HEADS, B_HEAD_DIM), F32)
    zero_m = jnp.zeros((nbp, CHUNK, LANE), F32)
    zero_conv = jnp.zeros((nbp, 1, D_FF), F32)
    cache_t = jnp.swapaxes(cache_mla, 2, 3)

    outs = {k: [] for k in ('lat_p', 'lat_s', 'mk', 'mv', 'cp', 'np', 'mp', 'cs', 'ns', 'ms', 'conv_p', 'conv_s',
                            'chunkv')}
    yp = ys = None
    for l in range(depth):
        lw = _layer_weights(l, w)
        last = l == depth - 1

        mla_p = functools.partial(_mla_prompt, wuv=lw['wuv'], nb=nbp, seq=S, tq=tq, tk=tk)
        ymix, st, lat, _ = _mixer_block(
            xp, lw, mlstm_state=(zero_c, zero_n, zero_m), pos_tables=pos_p,
            chunk_w=_chunk_mixers(chunk_ws[l], chunk_b[l], CHUNK), tm=tm_p, act_dtype=BF16, ns=1, n_tiles=nbp,
            n_chunks=S // CHUNK, mla_fn=lambda qc, lt: mla_p(qc, lt))
        xp, qx_p = _outproj(xp, *ymix, lw['wo'], lw['nw_x'], lw['xwq'], tm=tm_p, q_dtype=BF16)
        outs['lat_p'].append(lat.reshape(nbp, S, LATENT))
        outs['cp'].append(st[0])
        outs['np'].append(st[1])
        outs['mp'].append(st[2][:, CHUNK - 1, :B_HEADS])

        m0_rows = jnp.concatenate([jnp.repeat(state_mlstm_m[l], T, axis=0),
                                   jnp.zeros((ms, LANE - B_HEADS), F32)], axis=1).reshape(ms // CHUNK, CHUNK, LANE)
        mla_s = functools.partial(_mla_sample, page_table, wuv=lw['wuv'], cache_t=cache_t, layer=l, t_new=T)
        ymix, st, lat, gv = _mixer_block(
            xs, lw, mlstm_state=(state_mlstm_C[l], state_mlstm_n[l], m0_rows), pos_tables=pos_s,
            chunk_w=_chunk_mixers(chunk_ws[l], chunk_b[l], T), tm=tm_s, act_dtype=F32, ns=ns_s,
            n_tiles=ms // CHUNK, n_chunks=1, mla_fn=lambda qc, lt: mla_s(qc, lt))
        xs, qx_s = _outproj(xs, *ymix, lw['wo'], lw['nw_x'], lw['xwq'], tm=tm_s, q_dtype=F32)
        outs['lat_s'].append(lat.reshape(nbs, T, LATENT))
        outs['cs'].append(st[0])
        outs['ns'].append(st[1])
        outs['ms'].append(st[2].reshape(nbs, T, LANE)[:, T - 1, :B_HEADS])
        outs['chunkv'].append(gv.reshape(nbs, T, A_WIDTH))

        mk, mv, mk_b, mv_b = _memkv(mem, lw['nw_mem'], lw['xwk'], lw['xwv'], tm=min(512, nbp * MEM_LEN))
        outs['mk'].append(mk.reshape(nbp, MEM_LEN, X_HEADS, X_HEAD_DIM))
        outs['mv'].append(mv.reshape(nbp, MEM_LEN, X_HEADS, X_HEAD_DIM))
        xp = _xattn_prompt(xp, qx_p, mk_b, mv_b, lw['xwo'], tm=tm_p, tiles_per_seq=S // tm_p)
        xs = _xattn_sample(xs, qx_s, cache_mem_k, cache_mem_v, lw['xwo'], layer=l, n_seq=xg, t_rows=T)

        res_p = _ffn(xp, lw['nw_ffn'], lw['wg'], lw['wu'], lw['wd'], lw['cw'], lw['cb'], zero_conv, zero_conv, fnw,
                     tm=tm_ffn_p, seq_rows=0, tiles_per_seq=S // tm_ffn_p, final_norm=last)
        buf = state_ffn_conv[l]
        p0 = jnp.repeat(buf[:, 0, :], T, axis=0)
        p1 = jnp.repeat(buf[:, 1, :], T, axis=0)
        res_s = _ffn(xs, lw['nw_ffn'], lw['wg'], lw['wu'], lw['wd'], lw['cw'], lw['cb'], p0, p1, fnw,
                     tm=tm_ffn_s, seq_rows=T, tiles_per_seq=1, final_norm=last)
        xp, xs = res_p[0], res_s[0]
        outs['conv_p'].append(res_p[1][S // tm_ffn_p - 1::S // tm_ffn_p])
        outs['conv_s'].append(res_s[1].reshape(nbs, T, D_FF)[:, T - (CONV_W - 1):, :])
        if last:
            yp, ys = res_p[2], res_s[2]

    st = lambda k: jnp.stack(outs[k])
    return (yp.reshape(nbp, S, D_MODEL), ys.reshape(nbs, T, D_MODEL),
            st('lat_p'), st('lat_s'), st('mk'), st('mv'),
            st('cp'), st('np'), st('mp'), st('cs'), st('ns'), st('ms'),
            st('conv_p'), st('conv_s'), st('chunkv'))
```

```python
import functools
import math

import numpy as np
import jax
import jax.numpy as jnp
from jax import lax
from jax.experimental import pallas as pl
from jax.experimental.pallas import tpu as pltpu

F32 = jnp.float32
BF16 = jnp.bfloat16

D_MODEL = 1024
PAGE_SIZE = 128
A_HEADS = 4
A_WIDTH = 256
A_HEAD_DIM = 64
CHUNK = 128
B_HEADS = 4
B_WIDTH = 384
B_HEAD_DIM = 96
MLSTM_KSCALE = B_HEAD_DIM ** -0.5
C_WIDTH = 384
C_HEADS = 6
C_V_DIM = 64
C_NOPE = 64
C_ROPE = 32
Q_LORA = 192
KV_LORA = 128
LATENT = KV_LORA + C_ROPE
MLA_SCALE = (C_NOPE + C_ROPE) ** -0.5
ROPE_THETA = 10000.0
MEM_LEN = 256
X_HEADS = 4
X_HEAD_DIM = 256
XA_SCALE = X_HEAD_DIM ** -0.5
D_FF = 2816
CONV_W = 3
EPS = 1e-6

LANE = 128
HP = 128
BW_P = B_HEADS * HP
FF_CHUNK = 256
N_FF_CHUNKS = D_FF // FF_CHUNK
VMEM_LIMIT = 48 * 1024 * 1024

S_AU, S_AV, S_BQ, S_BK, S_BV, S_BO = 0, 256, 512, 1024, 1536, 2048
S_CQ, S_CKV, S_R1, S_R2, S_G = 2560, 2816, 2944, 3072, 3200
N_IN = 3328
CQ_P = 256
UQ_NOPE, UQ_RA, UQ_RB, N_UQ = 0, 384, 1152, 1920
QC_SLOT = 256
N_MIX_P = A_WIDTH + BW_P + C_WIDTH


def _rms(x, w, n):
    ms = jnp.sum(x * x, axis=-1, keepdims=True) * (1.0 / n)
    return x * lax.rsqrt(ms + EPS) * w


def _gelu(x):
    c = math.sqrt(2.0 / math.pi)
    return x * (0.5 * (1.0 + jnp.tanh(c * (x + 0.044715 * (x * x * x)))))


def _sigmoid(x):
    return 1.0 / (1.0 + jnp.exp(-x))


def _log_sigmoid(x):
    return jnp.minimum(x, 0.0) - jnp.log1p(jnp.exp(-jnp.abs(x)))


def _dot(a, b):
    return jnp.dot(a, b, preferred_element_type=F32)


def _dot_nt(a, b):
    return lax.dot_general(a, b, (((1,), (1,)), ((), ())), preferred_element_type=F32)


def _dot_tn(a, b):
    return lax.dot_general(a, b, (((0,), (0,)), ((), ())), preferred_element_type=F32)


def _split3(a):
    a1 = a.astype(BF16)
    r1 = a - a1.astype(F32)
    a2 = r1.astype(BF16)
    a3 = (r1 - a2.astype(F32)).astype(BF16)
    return a1, a2, a3


def _wspec(shape):
    nd = len(shape)
    return pl.BlockSpec(shape, lambda *_: (0,) * nd, pipeline_mode=pl.Buffered(1))


def _params(n_axes):
    return pltpu.CompilerParams(dimension_semantics=("arbitrary",) * n_axes,
                                vmem_limit_bytes=VMEM_LIMIT)


def _inproj_kernel(x_ref, nw_ref, w1_ref, qnw_ref, kvnw_ref, wuq_ref, wuk_ref, gb_ref, cos_ref, sin_ref,
                   gu_ref, gv_ref, q_ref, k_ref, v_ref, so_ref, qc_ref, lat_ref, gate_ref):
    xn = _rms(x_ref[...], nw_ref[...], D_MODEL).astype(BF16)

    def seg(a, b):
        return _dot(xn, w1_ref[:, a:b])

    gu_ref[...] = _gelu(seg(S_AU, S_AV))
    gv_ref[...] = _gelu(seg(S_AV, S_BQ))
    q_ref[...] = seg(S_BQ, S_BK).astype(q_ref.dtype)
    k_ref[...] = (seg(S_BK, S_BV) * MLSTM_KSCALE).astype(k_ref.dtype)
    v_ref[...] = seg(S_BV, S_BO).astype(v_ref.dtype)
    so_ref[...] = _sigmoid(seg(S_BO, S_CQ))

    c_q = _rms(seg(S_CQ, S_CKV), qnw_ref[...], Q_LORA).astype(BF16)
    q2 = _dot(c_q, wuq_ref[...])
    qlat = _dot(q2[:, UQ_NOPE:UQ_RA].astype(BF16), wuk_ref[...])
    cos = cos_ref[...]
    sin = sin_ref[...]
    for h in range(C_HEADS):
        ra = q2[:, UQ_RA + LANE * h:UQ_RA + LANE * (h + 1)]
        rb = q2[:, UQ_RB + LANE * h:UQ_RB + LANE * (h + 1)]
        qr = ra * cos + rb * sin
        qc_ref[:, QC_SLOT * h:QC_SLOT * h + LANE] = (qlat[:, LANE * h:LANE * (h + 1)] * MLA_SCALE).astype(qc_ref.dtype)
        qc_ref[:, QC_SLOT * h + LANE:QC_SLOT * (h + 1)] = (qr * MLA_SCALE).astype(qc_ref.dtype)

    lat_ref[:, 0:KV_LORA] = _rms(seg(S_CKV, S_R1), kvnw_ref[...], KV_LORA)
    kr = seg(S_R1, S_R2) * cos + seg(S_R2, S_G) * sin
    lat_ref[:, KV_LORA:LATENT] = kr[:, :C_ROPE]

    gt = seg(S_G, N_IN) + gb_ref[...]
    lane = lax.broadcasted_iota(jnp.int32, gt.shape, 1)
    gate_ref[...] = jnp.where(lane < B_HEADS, gt, jnp.where(lane < 2 * B_HEADS, _log_sigmoid(gt), 0.0))


def _inproj(x, nw, w1, qnw, kvnw, wuq, wuk, gb, cos_t, sin_t, *, tm, act_dtype):
    m = x.shape[0]
    n_pos_blocks = cos_t.shape[0] // tm
    row = lambda w: pl.BlockSpec((tm, w), lambda i: (i, 0))
    pos = pl.BlockSpec((tm, LANE), lambda i: (i % n_pos_blocks, 0))
    out_shapes = (
        jax.ShapeDtypeStruct((m, A_WIDTH), F32), jax.ShapeDtypeStruct((m, A_WIDTH), F32),
        jax.ShapeDtypeStruct((m, BW_P), act_dtype), jax.ShapeDtypeStruct((m, BW_P), act_dtype),
        jax.ShapeDtypeStruct((m, BW_P), act_dtype), jax.ShapeDtypeStruct((m, BW_P), F32),
        jax.ShapeDtypeStruct((m, C_HEADS * QC_SLOT), act_dtype), jax.ShapeDtypeStruct((m, LATENT), F32),
        jax.ShapeDtypeStruct((m, LANE), F32))
    return pl.pallas_call(
        _inproj_kernel,
        grid=(m // tm,),
        in_specs=[row(D_MODEL), _wspec((1, D_MODEL)), _wspec((D_MODEL, N_IN)), _wspec((1, CQ_P)),
                  _wspec((1, KV_LORA)), _wspec((CQ_P, N_UQ)), _wspec((C_HEADS * C_NOPE, C_HEADS * LANE)),
                  _wspec((1, LANE)), pos, pos],
        out_specs=(row(A_WIDTH), row(A_WIDTH), row(BW_P), row(BW_P), row(BW_P), row(BW_P),
                   row(C_HEADS * QC_SLOT), row(LATENT), row(LANE)),
        out_shape=out_shapes,
        compiler_params=_params(1),
        name="inproj",
    )(x, nw, w1, qnw, kvnw, wuq, wuk, gb, cos_t, sin_t)


def _chunkmlp_kernel(gu_ref, gv_ref, ws_ref, bias_ref, ya_ref, *, n_chunks):
    r = lax.broadcasted_iota(jnp.int32, (CHUNK, CHUNK), 0)
    c = lax.broadcasted_iota(jnp.int32, (CHUNK, CHUNK), 1)
    wsm = [jnp.where(r >= c, ws_ref[h], 0.0).astype(BF16) for h in range(A_HEADS)]
    head_of_lane = lax.broadcasted_iota(jnp.int32, (CHUNK, A_WIDTH), 1) >> (A_HEAD_DIM.bit_length() - 1)
    for ci in range(n_chunks):
        rows = slice(ci * CHUNK, (ci + 1) * CHUNK)
        gv = gv_ref[rows, :]
        z = bias_ref[...]
        for h in range(A_HEADS):
            z = z + _dot(wsm[h], jnp.where(head_of_lane == h, gv, 0.0).astype(BF16))
        ya_ref[rows, :] = gu_ref[rows, :] * z


def _chunkmlp(gu, gv, ws, bias, *, tm):
    m = gu.shape[0]
    row = pl.BlockSpec((tm, A_WIDTH), lambda i: (i, 0))
    return pl.pallas_call(
        functools.partial(_chunkmlp_kernel, n_chunks=tm // CHUNK),
        grid=(m // tm,),
        in_specs=[row, row, _wspec((A_HEADS, CHUNK, CHUNK)), _wspec((CHUNK, A_WIDTH))],
        out_specs=row,
        out_shape=jax.ShapeDtypeStruct((m, A_WIDTH), F32),
        compiler_params=_params(1),
        name="chunkmlp",
    )(gu, gv, ws, bias)


_COL_M, _COL_G, _COL_RS = 0, 8, 16


def _mlstm_kernel(q_ref, k_ref, v_ref, so_ref, gcol_ref, grow_ref, m0_ref, c0_ref, n0_ref, nw_ref,
                  yb_ref, cout_ref, nout_ref, mout_ref,
                  cs_ref, ns_ref, mp_ref, col_ref, sv_ref, qc_ref, nr_ref, *, ns, ls, g_tiles):
    @pl.when(pl.program_id(1) == 0)
    def _init():
        cs_ref[...] = jnp.zeros(cs_ref.shape, F32)
        ns_ref[...] = jnp.zeros(ns_ref.shape, F32)
        for s in range(g_tiles):
            for g in range(ns):
                for h in range(B_HEADS):
                    cs_ref[s, g * B_HEADS + h, :B_HEAD_DIM, :B_HEAD_DIM] = c0_ref[s, g, h]
                    ns_ref[s, g * B_HEADS + h, 0:1, :B_HEAD_DIM] = n0_ref[s, g, h:h + 1, :]
        if ns == 1:
            for s in range(g_tiles):
                for h in range(B_HEADS):
                    mp_ref[s, h:h + 1, :] = jnp.broadcast_to(m0_ref[s, 0:1, h:h + 1], (1, LANE))
        else:
            mp_ref[...] = m0_ref[...]

    for s in range(g_tiles):
        at = lambda r: r.at[s]
        if ns == 1:
            _mlstm_chunk(at(q_ref), at(k_ref), at(v_ref), at(so_ref), at(gcol_ref), at(grow_ref), nw_ref,
                         at(yb_ref), at(cout_ref), at(nout_ref), at(mout_ref), at(cs_ref), at(ns_ref), at(mp_ref))
        else:
            _mlstm_tile(at(q_ref), at(k_ref), at(v_ref), at(so_ref), at(gcol_ref), at(grow_ref), nw_ref,
                        at(yb_ref), at(cout_ref), at(nout_ref), at(mout_ref),
                        at(cs_ref), at(ns_ref), at(mp_ref), at(col_ref), at(sv_ref), at(qc_ref), at(nr_ref),
                        ns=ns, ls=ls)


def _mlstm_chunk(q_ref, k_ref, v_ref, so_ref, gcol_ref, grow_ref, nw_ref,
                 yb_ref, cout_ref, nout_ref, mout_ref, cs_ref, ns_ref, mp_ref):
    L = CHUNK
    row = lax.broadcasted_iota(jnp.int32, (L, L), 0)
    col = lax.broadcasted_iota(jnp.int32, (L, L), 1)
    causal = col <= row
    cum = jnp.where(causal, 1.0, 0.0).astype(BF16)
    gcol = gcol_ref[...]
    grow = grow_ref[...]
    brow = sum(_dot_nt(t, cum) for t in _split3(grow))
    ones = jnp.ones((L, HP), BF16)
    mout_ref[...] = jnp.zeros(mout_ref.shape, F32)
    for h in range(B_HEADS):
        hs = slice(h * HP, (h + 1) * HP)
        qh = q_ref[:, hs].astype(BF16)
        kh = k_ref[:, hs].astype(BF16)
        vh = v_ref[:, hs].astype(BF16)
        ig = jnp.broadcast_to(gcol[:, h:h + 1], (L, HP))
        lf = jnp.broadcast_to(gcol[:, B_HEADS + h:B_HEADS + h + 1], (L, HP))
        bc = sum(_dot(cum, t) for t in _split3(lf))
        bm = bc + mp_ref[h:h + 1, :]
        logd = jnp.where(causal, grow[h:h + 1, :] + bc - brow[B_HEADS + h:B_HEADS + h + 1, :], -jnp.inf)
        m_t = jnp.maximum(bm, jnp.max(logd, axis=-1, keepdims=True))
        s = (_dot_nt(qh, kh) * jnp.exp(logd - m_t)).astype(BF16)
        sv = _dot(s, jnp.concatenate([vh, ones], axis=1))
        gdec = jnp.exp(bm - m_t)

        c_old = cs_ref[h]
        n_old = ns_ref[h, 0:1, :]
        num = gdec * _dot_nt(qh, c_old.astype(BF16)) + sv[:, 0:HP]
        den = gdec * jnp.sum(qh.astype(F32) * n_old, axis=-1, keepdims=True) + sv[:, HP:2 * HP]
        hh = num / jnp.maximum(jnp.abs(den), jnp.exp(-m_t))
        yb_ref[:, hs] = so_ref[:, hs] * _rms(hh, nw_ref[:, hs], B_HEAD_DIM)

        m_new, b_last, decay = m_t[L - 1:L, :], bc[L - 1:L, :], gdec[L - 1:L, :]
        w = jnp.exp(ig + b_last - bc - m_new)
        c_new = decay * c_old + _dot_tn((vh.astype(F32) * w).astype(BF16), kh)
        n_new = decay * n_old + jnp.sum(kh.astype(F32) * w, axis=0, keepdims=True)
        cs_ref[h] = c_new
        ns_ref[h, 0:1, :] = n_new
        mp_ref[h:h + 1, :] = m_new
        mout_ref[L - 1:L, h:h + 1] = m_new[:, 0:1]
        cout_ref[0, h] = c_new[:B_HEAD_DIM, :B_HEAD_DIM]
        nout_ref[0, h:h + 1, :] = n_new[:, :B_HEAD_DIM]


def _mlstm_tile(q_ref, k_ref, v_ref, so_ref, gcol_ref, grow_ref, nw_ref,
                yb_ref, cout_ref, nout_ref, mout_ref,
                cs_ref, ns_ref, mp_ref, col_ref, sv_ref, qc_ref, nr_ref, *, ns, ls):
    L = CHUNK
    ls_shift = ls.bit_length() - 1

    col_ref[...] = jnp.zeros(col_ref.shape, F32)
    row = lax.broadcasted_iota(jnp.int32, (L, L), 0)
    col = lax.broadcasted_iota(jnp.int32, (L, L), 1)
    causal = (col <= row) & ((col >> ls_shift) == (row >> ls_shift))
    cum = jnp.where(causal, 1.0, 0.0).astype(BF16)
    rid = lax.broadcasted_iota(jnp.int32, (L, 1), 0)

    gcol = gcol_ref[...]
    grow = grow_ref[...]
    bcol = sum(_dot(cum, t) for t in _split3(gcol))
    brow = sum(_dot_nt(t, cum) for t in _split3(grow))
    mp = mp_ref[...]

    qs, ks, vs = [], [], []
    for h in range(B_HEADS):
        hs = slice(h * HP, (h + 1) * HP)
        qh = q_ref[:, hs].astype(BF16)
        kh = k_ref[:, hs].astype(BF16)
        vh = v_ref[:, hs].astype(BF16)
        qs.append(qh), ks.append(kh), vs.append(vh)
        ig_r = grow[h:h + 1, :]
        b_r = brow[B_HEADS + h:B_HEADS + h + 1, :]
        b_c = bcol[:, B_HEADS + h:B_HEADS + h + 1]
        bm = b_c + mp[:, h:h + 1]
        logd = jnp.where(causal, ig_r + b_c - b_r, -jnp.inf)
        m_t = jnp.maximum(bm, jnp.max(logd, axis=-1, keepdims=True))
        d = jnp.exp(logd - m_t)
        s = _dot_nt(qh, kh) * d
        sv_ref[h] = _dot(s.astype(BF16), vh)
        col_ref[:, _COL_M + h:_COL_M + h + 1] = m_t
        col_ref[:, _COL_G + h:_COL_G + h + 1] = jnp.exp(bm - m_t)
        col_ref[:, _COL_RS + h:_COL_RS + h + 1] = jnp.sum(s, axis=-1, keepdims=True)

    qc_ref[...] = jnp.zeros(qc_ref.shape, F32)
    nr_ref[...] = jnp.zeros(nr_ref.shape, F32)

    def seq_update(g):
        last = g * ls + (ls - 1)
        in_seq = (rid >> ls_shift) == g
        is_last = rid == last
        for h in range(B_HEADS):
            c_old = cs_ref[g * B_HEADS + h]
            n_old = ns_ref[g * B_HEADS + h, 0:1, :]
            qm = jnp.where(in_seq, qs[h], jnp.zeros_like(qs[h])) if ns > 1 else qs[h]
            qc_ref[h] += _dot_nt(qm, c_old.astype(BF16))
            nr_ref[h] += jnp.where(in_seq, n_old, 0.0)
            m_t = col_ref[:, _COL_M + h:_COL_M + h + 1]
            gdec = col_ref[:, _COL_G + h:_COL_G + h + 1]
            b_c = bcol[:, B_HEADS + h:B_HEADS + h + 1]
            ig_c = gcol[:, h:h + 1]
            pick = lambda a: jnp.sum(jnp.where(is_last, a, 0.0), axis=0, keepdims=True)
            m_new, b_last, decay = pick(m_t), pick(b_c), pick(gdec)
            w_c = jnp.where(in_seq, jnp.exp(ig_c + b_last - b_c - m_new), 0.0)
            vw = (vs[h].astype(F32) * w_c).astype(BF16)
            cs_ref[g * B_HEADS + h] = decay * c_old + _dot_tn(vw, ks[h])
            ns_ref[g * B_HEADS + h, 0:1, :] = (decay * n_old
                                               + jnp.sum(ks[h].astype(F32) * w_c, axis=0, keepdims=True))
            if ns == 1:
                mp_ref[:, h:h + 1] = jnp.broadcast_to(m_new, (L, 1))

    if ns == 1:
        seq_update(0)
    else:
        lax.fori_loop(0, ns, lambda g, c: (seq_update(g), c)[1], 0)

    for h in range(B_HEADS):
        hs = slice(h * HP, (h + 1) * HP)
        m_t = col_ref[:, _COL_M + h:_COL_M + h + 1]
        gdec = col_ref[:, _COL_G + h:_COL_G + h + 1]
        rs = col_ref[:, _COL_RS + h:_COL_RS + h + 1]
        num = gdec * qc_ref[h] + sv_ref[h]
        qn = jnp.sum(qs[h].astype(F32) * nr_ref[h], axis=-1, keepdims=True)
        den = gdec * qn + rs
        hh = num / jnp.maximum(jnp.abs(den), jnp.exp(-m_t))
        yb_ref[:, hs] = so_ref[:, hs] * _rms(hh, nw_ref[:, hs], B_HEAD_DIM)

    mout_ref[...] = col_ref[...]
    for g in range(ns):
        for h in range(B_HEADS):
            cout_ref[g, h] = cs_ref[g * B_HEADS + h, :B_HEAD_DIM, :B_HEAD_DIM]
            nout_ref[g, h:h + 1, :] = ns_ref[g * B_HEADS + h, 0:1, :B_HEAD_DIM]


def _mlstm(q, k, v, so, gcol, grow, m0, c0, n0, nw, *, ns, n_tiles, n_chunks, g_tiles):
    m = q.shape[0]
    nb = c0.shape[0]
    ls = CHUNK // ns
    g = g_tiles
    t3 = lambda a: a.reshape(n_tiles, n_chunks * CHUNK, a.shape[-1])
    rows = lambda w: pl.BlockSpec((g, CHUNK, w), lambda t, c: (t, c, 0))
    cspec = pl.BlockSpec((g, ns, B_HEADS, B_HEAD_DIM, B_HEAD_DIM), lambda t, c: (t, 0, 0, 0, 0))
    nspec = pl.BlockSpec((g, ns, B_HEADS, B_HEAD_DIM), lambda t, c: (t, 0, 0, 0))
    mspec = pl.BlockSpec((g, CHUNK, LANE), lambda t, c: (t, 0, 0))
    yb, c_new, n_new, m_rows = pl.pallas_call(
        functools.partial(_mlstm_kernel, ns=ns, ls=ls, g_tiles=g),
        grid=(n_tiles // g, n_chunks),
        in_specs=[rows(BW_P), rows(BW_P), rows(BW_P), rows(BW_P), rows(LANE),
                  pl.BlockSpec((g, None, 16, CHUNK), lambda t, c: (t, c, 0, 0)),
                  mspec, cspec, nspec, _wspec((1, BW_P))],
        out_specs=(rows(BW_P), cspec, nspec, mspec),
        out_shape=(jax.ShapeDtypeStruct((n_tiles, n_chunks * CHUNK, BW_P), F32),
                   jax.ShapeDtypeStruct((n_tiles, ns, B_HEADS, B_HEAD_DIM, B_HEAD_DIM), F32),
                   jax.ShapeDtypeStruct((n_tiles, ns, B_HEADS, B_HEAD_DIM), F32),
                   jax.ShapeDtypeStruct((n_tiles, CHUNK, LANE), F32)),
        scratch_shapes=[pltpu.VMEM((g, ns * B_HEADS, HP, HP), F32), pltpu.VMEM((g, ns * B_HEADS, 8, HP), F32),
                        pltpu.VMEM((g, CHUNK, LANE), F32), pltpu.VMEM((g, CHUNK, LANE), F32),
                        pltpu.VMEM((g, B_HEADS, CHUNK, HP), F32), pltpu.VMEM((g, B_HEADS, CHUNK, HP), F32),
                        pltpu.VMEM((g, B_HEADS, CHUNK, HP), F32)],
        compiler_params=_params(2),
        name="mlstm",
    )(t3(q), t3(k), t3(v), t3(so), t3(gcol), grow.reshape(n_tiles, n_chunks, 16, CHUNK), m0,
      c0.reshape(n_tiles, ns, B_HEADS, B_HEAD_DIM, B_HEAD_DIM), n0.reshape(n_tiles, ns, B_HEADS, B_HEAD_DIM), nw)
    return (yb.reshape(m, BW_P), c_new.reshape(nb, B_HEADS, B_HEAD_DIM, B_HEAD_DIM),
            n_new.reshape(nb, B_HEADS, B_HEAD_DIM), m_rows)


def _stack_heads(qc_ref, dtype):
    return jnp.concatenate([qc_ref[:, QC_SLOT * h:QC_SLOT * h + LATENT] for h in range(C_HEADS)],
                           axis=0).astype(dtype)


def _unstack_project(o, t, wuv_ref):
    wide = jnp.concatenate([o[h * t:(h + 1) * t] for h in range(C_HEADS)], axis=1)
    return _dot(wide.astype(BF16), wuv_ref[...])


def _softmax_step(s, pv, m_ref, acc_ref):
    reps = s.shape[1] // LANE
    m_prev = m_ref[...]
    m_new = jnp.maximum(m_prev, jnp.max(s, axis=-1, keepdims=True))
    p = jnp.exp(s - jnp.concatenate([m_new] * reps, axis=1))
    corr = jnp.exp(m_prev - m_new)
    acc_ref[...] = jnp.concatenate([corr, corr], axis=1) * acc_ref[...] + pv(p.astype(BF16))
    m_ref[...] = m_new


def _softmax_result(acc_ref):
    acc = acc_ref[...]
    return acc[:, :KV_LORA] / acc[:, KV_LORA:]


def _mla_prompt_kernel(qc_ref, lat_ref, wuv_ref, yc_ref, kb_ref, vb_ref, m_ref, acc_ref, *, tq, tk, row_groups):
    qi = pl.program_id(1)

    @pl.when(qi == 0)
    def _cast_keys():
        lat = lat_ref[...]
        kb_ref[...] = lat.astype(BF16)
        vb_ref[:, 0:KV_LORA] = lat[:, 0:KV_LORA].astype(BF16)
        vb_ref[:, KV_LORA:2 * KV_LORA] = jnp.ones((lat.shape[0], KV_LORA), BF16)

    qst = _stack_heads(qc_ref, BF16)
    m_ref[...] = jnp.full(m_ref.shape, -jnp.inf, F32)
    acc_ref[...] = jnp.zeros(acc_ref.shape, F32)
    rows = C_HEADS * tq

    gr = rows // row_groups

    def block(j, masked):
        start = pl.multiple_of(j * tk, tk)
        kj = kb_ref[pl.ds(start, tk), :]
        vj = vb_ref[pl.ds(start, tk), :]
        for g in range(row_groups):
            rs = slice(g * gr, (g + 1) * gr)
            s = _dot_nt(qst[rs], kj)
            if masked:
                r = g * gr + lax.broadcasted_iota(jnp.int32, (gr, tk), 0)
                qpos = qi * tq + (r & (tq - 1))
                kpos = start + lax.broadcasted_iota(jnp.int32, (gr, tk), 1)
                s = jnp.where(kpos <= qpos, s, -jnp.inf)
            _softmax_step(s, lambda p: _dot(p, vj), m_ref.at[rs], acc_ref.at[rs])

    n_full = (qi * tq) // tk
    lax.fori_loop(0, n_full, lambda j, c: (block(j, False), c)[1], 0)
    block(n_full, True)
    yc_ref[...] = _unstack_project(_softmax_result(acc_ref), tq, wuv_ref)


def _mla_prompt(qc, lat, wuv, *, nb, seq, tq, tk):
    m = qc.shape[0]
    n_q = seq // tq
    rows = C_HEADS * tq
    return pl.pallas_call(
        functools.partial(_mla_prompt_kernel, tq=tq, tk=tk, row_groups=6),
        grid=(nb, n_q),
        in_specs=[pl.BlockSpec((tq, C_HEADS * QC_SLOT), lambda b, i: (b * n_q + i, 0)),
                  pl.BlockSpec((seq, LATENT), lambda b, i: (b, 0)),
                  _wspec((C_HEADS * KV_LORA, C_WIDTH))],
        out_specs=pl.BlockSpec((tq, C_WIDTH), lambda b, i: (b * n_q + i, 0)),
        out_shape=jax.ShapeDtypeStruct((m, C_WIDTH), F32),
        scratch_shapes=[pltpu.VMEM((seq, LATENT), BF16), pltpu.VMEM((seq, 2 * KV_LORA), BF16),
                        pltpu.VMEM((rows, LANE), F32), pltpu.VMEM((rows, 2 * KV_LORA), F32)],
        compiler_params=_params(2),
        name="mla_prompt",
    )(qc, lat, wuv)


def _mla_sample_kernel(pt_ref, qc_ref, latn_ref, wuv_ref, cache_ref, yc_ref, buf_ref, sem_ref, kt_ref, kn_ref,
                       *, layer, n_pages, t_new):
    b = pl.program_id(0)
    nb = pl.num_programs(0)
    slot = b & 1
    rows = C_HEADS * t_new

    def page_copy(seq, page, to_slot):
        return pltpu.make_async_copy(cache_ref.at[layer, pt_ref[seq, page]], buf_ref.at[to_slot, page],
                                     sem_ref.at[to_slot])

    def request(seq, to_slot):
        for page in range(n_pages):
            page_copy(seq, page, to_slot).start()

    @pl.when(b == 0)
    def _first():
        request(0, 0)

    @pl.when(b + 1 < nb)
    def _prefetch_next():
        request(b + 1, 1 - slot)

    for page in range(n_pages):
        page_copy(b, page, slot).wait()

    qst = _stack_heads(qc_ref, BF16)
    for page in range(n_pages):
        kt_ref[:, page * PAGE_SIZE:(page + 1) * PAGE_SIZE] = buf_ref[slot, page].astype(BF16)
    s_past = _dot(qst, kt_ref[...])

    kn_ref[...] = jnp.zeros(kn_ref.shape, F32)
    kn_ref[0:t_new, :] = latn_ref[...]
    kn = kn_ref[...].astype(BF16)
    tpos = lax.broadcasted_iota(jnp.int32, (rows, PAGE_SIZE), 0) & (t_new - 1)
    kidx = lax.broadcasted_iota(jnp.int32, (rows, PAGE_SIZE), 1)
    s_new = jnp.where(kidx <= tpos, _dot_nt(qst, kn), -jnp.inf)

    m = jnp.maximum(jnp.max(s_past, axis=-1, keepdims=True), jnp.max(s_new, axis=-1, keepdims=True))
    p_past = jnp.exp(s_past - m).astype(BF16)
    p_new = jnp.exp(s_new - m).astype(BF16)
    denom = (jnp.sum(p_past.astype(F32), axis=-1, keepdims=True)
             + jnp.sum(p_new.astype(F32), axis=-1, keepdims=True))
    o = _dot_nt(p_past, kt_ref[0:KV_LORA, :]) + _dot(p_new, kn[:, 0:KV_LORA])
    yc_ref[...] = _unstack_project(o / denom, t_new, wuv_ref)


def _mla_sample(page_table, qc, lat_new, wuv, cache_t, *, layer, t_new):
    m = qc.shape[0]
    nb, n_pages = page_table.shape
    grid_spec = pltpu.PrefetchScalarGridSpec(
        num_scalar_prefetch=1,
        grid=(nb,),
        in_specs=[pl.BlockSpec((t_new, C_HEADS * QC_SLOT), lambda b, pt: (b, 0)),
                  pl.BlockSpec((t_new, LATENT), lambda b, pt: (b, 0)),
                  pl.BlockSpec((C_HEADS * KV_LORA, C_WIDTH), lambda b, pt: (0, 0)),
                  pl.BlockSpec(memory_space=pl.ANY)],
        out_specs=pl.BlockSpec((t_new, C_WIDTH), lambda b, pt: (b, 0)),
        scratch_shapes=[pltpu.VMEM((2, n_pages, LATENT, PAGE_SIZE), F32), pltpu.SemaphoreType.DMA((2,)),
                        pltpu.VMEM((LATENT, n_pages * PAGE_SIZE), BF16), pltpu.VMEM((PAGE_SIZE, LATENT), F32)])
    return pl.pallas_call(
        functools.partial(_mla_sample_kernel, layer=layer, n_pages=n_pages, t_new=t_new),
        grid_spec=grid_spec,
        out_shape=jax.ShapeDtypeStruct((m, C_WIDTH), F32),
        compiler_params=_params(1),
        name="mla_sample",
    )(page_table, qc, lat_new, wuv, cache_t)


def _outproj_kernel(x_ref, ya_ref, yb_ref, yc_ref, wo_ref, nxw_ref, wq_ref, xo_ref, q_ref):
    y = (_dot(ya_ref[...].astype(BF16), wo_ref[0:A_WIDTH, :])
         + _dot(yb_ref[...].astype(BF16), wo_ref[A_WIDTH:A_WIDTH + BW_P, :])
         + _dot(yc_ref[...].astype(BF16), wo_ref[A_WIDTH + BW_P:N_MIX_P, :]))
    xn = x_ref[...] + y
    xo_ref[...] = xn
    hq = _rms(xn, nxw_ref[...], D_MODEL).astype(BF16)
    q_ref[...] = (_dot(hq, wq_ref[...]) * XA_SCALE).astype(q_ref.dtype)


def _outproj(x, ya, yb, yc, wo, nxw, wq, *, tm, q_dtype):
    m = x.shape[0]
    row = lambda w: pl.BlockSpec((tm, w), lambda i: (i, 0))
    return pl.pallas_call(
        _outproj_kernel,
        grid=(m // tm,),
        in_specs=[row(D_MODEL), row(A_WIDTH), row(BW_P), row(C_WIDTH), _wspec((N_MIX_P, D_MODEL)),
                  _wspec((1, D_MODEL)), _wspec((D_MODEL, D_MODEL))],
        out_specs=(row(D_MODEL), row(D_MODEL)),
        out_shape=(jax.ShapeDtypeStruct((m, D_MODEL), F32), jax.ShapeDtypeStruct((m, D_MODEL), q_dtype)),
        compiler_params=_params(1),
        name="outproj",
    )(x, ya, yb, yc, wo, nxw, wq)


def _memkv_kernel(mem_ref, nw_ref, wk_ref, wv_ref, k_ref, v_ref, kb_ref, vb_ref):
    mn = _rms(mem_ref[...], nw_ref[...], D_MODEL).astype(BF16)
    k = _dot(mn, wk_ref[...])
    v = _dot(mn, wv_ref[...])
    kb_ref[...] = k.astype(BF16)
    vb_ref[...] = v.astype(BF16)
    for h in range(X_HEADS):
        hs = slice(h * X_HEAD_DIM, (h + 1) * X_HEAD_DIM)
        k_ref[:, h, :] = k[:, hs]
        v_ref[:, h, :] = v[:, hs]


def _memkv(mem, nw, wk, wv, *, tm):
    m = mem.shape[0]
    row = pl.BlockSpec((tm, D_MODEL), lambda i: (i, 0))
    out = pl.BlockSpec((tm, X_HEADS, X_HEAD_DIM), lambda i: (i, 0, 0))
    return pl.pallas_call(
        _memkv_kernel,
        grid=(m // tm,),
        in_specs=[row, _wspec((1, D_MODEL)), _wspec((D_MODEL, D_MODEL)), _wspec((D_MODEL, D_MODEL))],
        out_specs=(out, out, row, row),
        out_shape=(jax.ShapeDtypeStruct((m, X_HEADS, X_HEAD_DIM), F32),) * 2
                  + (jax.ShapeDtypeStruct((m, D_MODEL), BF16),) * 2,
        compiler_params=_params(1),
        name="memkv",
    )(mem, nw, wk, wv)


def _softmax_rows(s):
    p = jnp.exp(s - jnp.max(s, axis=-1, keepdims=True))
    return p / jnp.sum(p, axis=-1, keepdims=True)


def _xattn_prompt_kernel(x_ref, q_ref, k_ref, v_ref, wo_ref, xo_ref, o_ref):
    for h in range(X_HEADS):
        hs = slice(h * X_HEAD_DIM, (h + 1) * X_HEAD_DIM)
        p = _softmax_rows(_dot_nt(q_ref[:, hs], k_ref[:, hs]))
        o_ref[:, hs] = _dot(p.astype(BF16), v_ref[:, hs])
    xo_ref[...] = x_ref[...] + _dot(o_ref[...].astype(BF16), wo_ref[...])


def _xattn_prompt(x, q, k, v, wo, *, tm, tiles_per_seq):
    m = x.shape[0]
    row = pl.BlockSpec((tm, D_MODEL), lambda i: (i, 0))
    kv = pl.BlockSpec((MEM_LEN, D_MODEL), lambda i: (i // tiles_per_seq, 0))
    return pl.pallas_call(
        _xattn_prompt_kernel,
        grid=(m // tm,),
        in_specs=[row, row, kv, kv, _wspec((D_MODEL, D_MODEL))],
        out_specs=row,
        out_shape=jax.ShapeDtypeStruct((m, D_MODEL), F32),
        scratch_shapes=[pltpu.VMEM((tm, D_MODEL), F32)],
        compiler_params=_params(1),
        name="xattn_prompt",
    )(x, q, k, v, wo)


def _xattn_sample_kernel(x_ref, q_ref, k_ref, v_ref, wo_ref, xo_ref, o_ref, *, n_seq, t_rows):
    rows = X_HEADS * t_rows
    nk = MEM_LEN * X_HEADS
    r = lax.broadcasted_iota(jnp.int32, (rows, nk), 0)
    c = lax.broadcasted_iota(jnp.int32, (rows, nk), 1)
    keep = (c & (X_HEADS - 1)) == (r >> (t_rows.bit_length() - 1))
    for g in range(n_seq):
        rs = slice(g * t_rows, (g + 1) * t_rows)
        qst = jnp.concatenate([q_ref[rs, h * X_HEAD_DIM:(h + 1) * X_HEAD_DIM] for h in range(X_HEADS)], axis=0)
        k2 = k_ref[g].reshape(nk, X_HEAD_DIM).astype(BF16)
        v2 = v_ref[g].reshape(nk, X_HEAD_DIM).astype(BF16)
        p = _softmax_rows(jnp.where(keep, _dot_nt(qst.astype(BF16), k2), -jnp.inf))
        o = _dot(p.astype(BF16), v2)
        for h in range(X_HEADS):
            o_ref[rs, h * X_HEAD_DIM:(h + 1) * X_HEAD_DIM] = o[h * t_rows:(h + 1) * t_rows]
    xo_ref[...] = x_ref[...] + _dot(o_ref[...].astype(BF16), wo_ref[...])


def _xattn_sample(x, q, k, v, wo, *, layer, n_seq, t_rows):
    m = x.shape[0]
    r = n_seq * t_rows
    row = pl.BlockSpec((r, D_MODEL), lambda i: (i, 0))
    kv = pl.BlockSpec((None, n_seq, MEM_LEN, X_HEADS, X_HEAD_DIM), lambda i: (layer, i, 0, 0, 0))
    return pl.pallas_call(
        functools.partial(_xattn_sample_kernel, n_seq=n_seq, t_rows=t_rows),
        grid=(m // r,),
        in_specs=[row, row, kv, kv, _wspec((D_MODEL, D_MODEL))],
        out_specs=row,
        out_shape=jax.ShapeDtypeStruct((m, D_MODEL), F32),
        scratch_shapes=[pltpu.VMEM((r, D_MODEL), F32)],
        compiler_params=_params(1),
        name="xattn_sample",
    )(x, q, k, v, wo)


def _ffn_kernel(x_ref, nw_ref, wg_ref, wu_ref, wd_ref, cw_ref, cb_ref, p0_ref, p1_ref, fnw_ref, *rest,
                seq_rows, tiles_per_seq, final_norm):
    if final_norm:
        xo_ref, gtail_ref, y_ref, carry_ref, act_ref = rest
    else:
        xo_ref, gtail_ref, carry_ref, act_ref = rest
    tm = x_ref.shape[0]
    x = x_ref[...]
    xn = _rms(x, nw_ref[...], D_MODEL).astype(BF16)
    rid = lax.broadcasted_iota(jnp.int32, (tm, 1), 0)
    if seq_rows:
        pos = rid & (seq_rows - 1)
    else:
        pos = rid

        @pl.when(pl.program_id(0) % tiles_per_seq == 0)
        def _seq_start():
            carry_ref[0:1, :] = p0_ref[...]
            carry_ref[1:2, :] = p1_ref[...]

    for c in range(N_FF_CHUNKS):
        cs = slice(c * FF_CHUNK, (c + 1) * FF_CHUNK)
        g = _dot(xn, wg_ref[:, cs])
        u = _dot(xn, wu_ref[:, cs])
        if seq_rows:
            prev0, prev1 = p0_ref[:, cs], p1_ref[:, cs]
            gtail_ref[:, cs] = g
        else:
            prev0, prev1 = carry_ref[0:1, cs], carry_ref[1:2, cs]
            gtail_ref[:, cs] = g[tm - 2:tm, :]
            carry_ref[0:2, cs] = g[tm - 2:tm, :]
        g1 = jnp.where(pos == 0, prev1, pltpu.roll(g, 1, 0))
        g2 = jnp.where(pos == 0, prev0, jnp.where(pos == 1, prev1, pltpu.roll(g, 2, 0)))
        y = cb_ref[:, cs] + g2 * cw_ref[0:1, cs] + g1 * cw_ref[1:2, cs] + g * cw_ref[2:3, cs]
        act_ref[:, cs] = (_gelu(y) * u).astype(BF16)
    xo = x + _dot(act_ref[...], wd_ref[...])
    xo_ref[...] = xo
    if final_norm:
        y_ref[...] = _rms(xo, fnw_ref[...], D_MODEL)


def _ffn(x, nw, wg, wu, wd, cw, cb, p0, p1, fnw, *, tm, seq_rows, tiles_per_seq, final_norm):
    m = x.shape[0]
    n_tiles = m // tm
    row = pl.BlockSpec((tm, D_MODEL), lambda i: (i, 0))
    if seq_rows:
        pspec = pl.BlockSpec((tm, D_FF), lambda i: (i, 0))
        gspec = pspec
        gshape = jax.ShapeDtypeStruct((m, D_FF), F32)
    else:
        pspec = pl.BlockSpec((None, 1, D_FF), lambda i: (i // tiles_per_seq, 0, 0))
        gspec = pl.BlockSpec((None, 2, D_FF), lambda i: (i, 0, 0))
        gshape = jax.ShapeDtypeStruct((n_tiles, 2, D_FF), F32)
    return pl.pallas_call(
        functools.partial(_ffn_kernel, seq_rows=seq_rows, tiles_per_seq=tiles_per_seq, final_norm=final_norm),
        grid=(n_tiles,),
        in_specs=[row, _wspec((1, D_MODEL)), _wspec((D_MODEL, D_FF)), _wspec((D_MODEL, D_FF)),
                  _wspec((D_FF, D_MODEL)), _wspec((CONV_W, D_FF)), _wspec((1, D_FF)),
                  pspec, pspec, _wspec((1, D_MODEL))],
        out_specs=(row, gspec) + ((row,) if final_norm else ()),
        out_shape=(jax.ShapeDtypeStruct((m, D_MODEL), F32), gshape)
                  + ((jax.ShapeDtypeStruct((m, D_MODEL), F32),) if final_norm else ()),
        scratch_shapes=[pltpu.VMEM((8, D_FF), F32), pltpu.VMEM((tm, D_FF), BF16)],
        compiler_params=_params(1),
        name="ffn",
    )(x, nw, wg, wu, wd, cw, cb, p0, p1, fnw)


def _pad_heads_idx(start):
    idx = np.full((BW_P,), -1, np.int64)
    for h in range(B_HEADS):
        idx[h * HP:h * HP + B_HEAD_DIM] = start + h * B_HEAD_DIM + np.arange(B_HEAD_DIM)
    return idx


def _take_cols(w, idx, axis=-1):
    axis = axis % w.ndim
    pieces, i, n = [], 0, len(idx)
    while i < n:
        j = i + 1
        if idx[i] < 0:
            while j < n and idx[j] < 0:
                j += 1
            shape = w.shape[:axis] + (j - i,) + w.shape[axis + 1:]
            pieces.append(jnp.zeros(shape, w.dtype))
        else:
            while j < n and idx[j] == idx[j - 1] + 1:
                j += 1
            pieces.append(lax.slice_in_dim(w, int(idx[i]), int(idx[i]) + (j - i), axis=axis))
        i = j
    return jnp.concatenate(pieces, axis=axis)


def _win_layout():
    half = C_ROPE // 2
    base = {'au': 0, 'av': 256, 'bq': 512, 'bk': 896, 'bv': 1280, 'bo': 1664, 'bi': 2048, 'bf': 2052,
            'cq': 2056, 'ckv': 2248, 'ckr': 2376}
    idx = np.full((N_IN,), -1, np.int64)
    idx[S_AU:S_AU + 256] = base['au'] + np.arange(256)
    idx[S_AV:S_AV + 256] = base['av'] + np.arange(256)
    for s, name in ((S_BQ, 'bq'), (S_BK, 'bk'), (S_BV, 'bv'), (S_BO, 'bo')):
        idx[s:s + BW_P] = _pad_heads_idx(base[name])
    idx[S_CQ:S_CQ + Q_LORA] = base['cq'] + np.arange(Q_LORA)
    idx[S_CKV:S_CKV + KV_LORA] = base['ckv'] + np.arange(KV_LORA)
    idx[S_R1:S_R1 + C_ROPE] = base['ckr'] + np.arange(C_ROPE)
    idx[S_R2:S_R2 + half] = base['ckr'] + half + np.arange(half)
    idx[S_R2 + half:S_R2 + C_ROPE] = base['ckr'] + np.arange(half)
    idx[S_G:S_G + B_HEADS] = base['bi'] + np.arange(B_HEADS)
    idx[S_G + B_HEADS:S_G + 2 * B_HEADS] = base['bf'] + np.arange(B_HEADS)
    return idx


def _wuq_layout():
    half = C_ROPE // 2
    per = C_NOPE + C_ROPE
    idx = np.full((N_UQ,), -1, np.int64)
    for h in range(C_HEADS):
        idx[UQ_NOPE + h * C_NOPE:UQ_NOPE + (h + 1) * C_NOPE] = h * per + np.arange(C_NOPE)
        idx[UQ_RA + h * LANE:UQ_RA + h * LANE + C_ROPE] = h * per + C_NOPE + np.arange(C_ROPE)
        idx[UQ_RB + h * LANE:UQ_RB + h * LANE + half] = h * per + C_NOPE + half + np.arange(half)
        idx[UQ_RB + h * LANE + half:UQ_RB + h * LANE + C_ROPE] = h * per + C_NOPE + np.arange(half)
    return idx


def _rope_tables(pos, reps):
    half = C_ROPE // 2
    inv = ROPE_THETA ** (-jnp.arange(half, dtype=F32) / half)
    ang = pos.astype(F32)[:, None] * inv[None, :]
    cos, sin = jnp.cos(ang), jnp.sin(ang)
    z = jnp.zeros((pos.shape[0], LANE - C_ROPE), F32)
    ct = jnp.concatenate([cos, cos, z], axis=1)
    st = jnp.concatenate([-sin, sin, z], axis=1)
    return jnp.tile(ct, (reps, 1)), jnp.tile(st, (reps, 1))


def _pad_lanes(v, idx):
    return _take_cols(v[None, :], idx)


def _layer_weights(l, w):
    hp_idx = _pad_heads_idx(0)
    out = {}
    out['nw_mix'] = w['norm_mix_w'][l][None, :]
    out['w1'] = _take_cols(w['w_in'][l], _win_layout()).astype(BF16)
    out['qnw'] = _pad_lanes(w['mla_qnorm_w'][l], np.concatenate([np.arange(Q_LORA), np.full(CQ_P - Q_LORA, -1)]))
    out['kvnw'] = w['mla_kvnorm_w'][l][None, :]
    wuq = w['mla_w_uq'][l].reshape(Q_LORA, C_HEADS * (C_NOPE + C_ROPE))
    wuq = _take_cols(wuq, _wuq_layout())
    out['wuq'] = jnp.concatenate([wuq, jnp.zeros((CQ_P - Q_LORA, N_UQ), F32)], axis=0).astype(BF16)
    wuk = jnp.transpose(w['mla_w_uk'][l], (1, 2, 0))
    eye = jnp.eye(C_HEADS, dtype=F32)
    out['wuk'] = jnp.einsum('hdc,hg->hdgc', wuk, eye).reshape(C_HEADS * C_NOPE, C_HEADS * KV_LORA).astype(BF16)
    wuv = jnp.transpose(w['mla_w_uv'][l], (1, 0, 2))
    out['wuv'] = jnp.einsum('hce,hg->hcge', wuv, eye).reshape(C_HEADS * KV_LORA, C_WIDTH).astype(BF16)
    gb = jnp.concatenate([w['mlstm_bi'][l], w['mlstm_bf'][l]])
    out['gb'] = _pad_lanes(gb, np.concatenate([np.arange(2 * B_HEADS), np.full(LANE - 2 * B_HEADS, -1)]))
    out['mnw'] = _pad_lanes(w['mlstm_norm_w'][l], hp_idx)
    wo = w['w_out'][l]
    rows = np.concatenate([np.arange(A_WIDTH), np.where(hp_idx < 0, -1, hp_idx + A_WIDTH),
                           A_WIDTH + B_WIDTH + np.arange(C_WIDTH)])
    out['wo'] = _take_cols(wo, rows, axis=0).astype(BF16)
    out['nw_x'] = w['norm_x_w'][l][None, :]
    out['nw_mem'] = w['norm_mem_w'][l][None, :]
    out['xwq'] = w['xa_wq'][l].reshape(D_MODEL, D_MODEL).astype(BF16)
    out['xwk'] = w['xa_wk'][l].reshape(D_MODEL, D_MODEL).astype(BF16)
    out['xwv'] = w['xa_wv'][l].reshape(D_MODEL, D_MODEL).astype(BF16)
    out['xwo'] = w['xa_wo'][l].reshape(D_MODEL, D_MODEL).astype(BF16)
    out['nw_ffn'] = w['norm_ffn_w'][l][None, :]
    out['wg'] = w['ffn_wg'][l].astype(BF16)
    out['wu'] = w['ffn_wu'][l].astype(BF16)
    out['wd'] = w['ffn_wd'][l].astype(BF16)
    out['cw'] = w['ffn_conv_w'][l]
    out['cb'] = w['ffn_conv_b'][l][None, :]
    return out


def _chunk_mixers(ws, cb, t_rows):
    reps = CHUNK // t_rows
    wt = ws[:, :t_rows, :t_rows]
    if reps > 1:
        wt = jnp.einsum('ab,hts->hatbs', jnp.eye(reps, dtype=F32), wt).reshape(A_HEADS, CHUNK, CHUNK)
    bias = jnp.tile(jnp.repeat(jnp.transpose(cb[:, :t_rows]), A_HEAD_DIM, axis=1), (reps, 1))
    return wt, bias


def _mixer_block(x, lw, *, mlstm_state, pos_tables, chunk_w, tm, act_dtype, ns, n_tiles, n_chunks, mla_fn):
    m = x.shape[0]
    cos_t, sin_t = pos_tables
    gu, gv, q, k, v, so, qc, lat, gate = _inproj(x, lw['nw_mix'], lw['w1'], lw['qnw'], lw['kvnw'], lw['wuq'],
                                                 lw['wuk'], lw['gb'], cos_t, sin_t, tm=tm, act_dtype=act_dtype)
    ya = _chunkmlp(gu, gv, chunk_w[0], chunk_w[1], tm=tm)
    grow = jnp.transpose(gate[:, :16].reshape(m // CHUNK, CHUNK, 16), (0, 2, 1))
    c0, n0, m0 = mlstm_state
    yb, c_new, n_new, m_rows = _mlstm(q, k, v, so, gate, grow, m0, c0, n0, lw['mnw'], ns=ns, n_tiles=n_tiles,
                                      n_chunks=n_chunks, g_tiles=2 if (ns == 1 and n_tiles % 2 == 0) else 1)
    yc = mla_fn(qc, lat)
    return (ya, yb, yc), (c_new, n_new, m_rows), lat, gv


def kernel(x_prompt, x_sample, mem_prompt, cache_mla, page_table, cache_mem_k, cache_mem_v, state_mlstm_C, state_mlstm_n, state_mlstm_m, state_ffn_conv, norm_mix_w, w_in, chunk_ws, chunk_b, mlstm_bi, mlstm_bf, mlstm_norm_w, mla_qnorm_w, mla_kvnorm_w, mla_w_uq, mla_w_uk, mla_w_uv, w_out, norm_x_w, norm_mem_w, xa_wq, xa_wk, xa_wv, xa_wo, norm_ffn_w, ffn_wg, ffn_wu, ffn_conv_w, ffn_conv_b, ffn_wd, norm_final_w):
    w = dict(norm_mix_w=norm_mix_w, w_in=w_in, mlstm_bi=mlstm_bi, mlstm_bf=mlstm_bf, mlstm_norm_w=mlstm_norm_w,
             mla_qnorm_w=mla_qnorm_w, mla_kvnorm_w=mla_kvnorm_w, mla_w_uq=mla_w_uq, mla_w_uk=mla_w_uk,
             mla_w_uv=mla_w_uv, w_out=w_out, norm_x_w=norm_x_w, norm_mem_w=norm_mem_w, xa_wq=xa_wq, xa_wk=xa_wk,
             xa_wv=xa_wv, xa_wo=xa_wo, norm_ffn_w=norm_ffn_w, ffn_wg=ffn_wg, ffn_wu=ffn_wu,
             ffn_conv_w=ffn_conv_w, ffn_conv_b=ffn_conv_b, ffn_wd=ffn_wd)
    depth = w_in.shape[0]
    nbp, S, _ = x_prompt.shape
    nbs, T, _ = x_sample.shape
    n_pages = page_table.shape[1]
    past_len = n_pages * PAGE_SIZE
    mp, ms = nbp * S, nbs * T
    assert S % CHUNK == 0 and CHUNK % T == 0 and T % 8 == 0 and ms % CHUNK == 0

    tm_p = min(512, S)
    tm_s = min(512, ms)
    tq = min(256, S)
    tk = min(512, S)
    ns_s = CHUNK // T
    xg = 4 if nbs % 4 == 0 else 1
    tm_ffn_p = min(512, S)
    tm_ffn_s = min(256, ms)

    pos_p = _rope_tables(jnp.arange(S), 1)
    pos_s = _rope_tables(past_len + jnp.arange(T), tm_s // T)
    fnw = norm_final_w[None, :]

    xp = x_prompt.reshape(mp, D_MODEL)
    xs = x_sample.reshape(ms, D_MODEL)
    mem = mem_prompt.reshape(nbp * MEM_LEN, D_MODEL)

    zero_c = jnp.zeros((nbp, B_HEADS, B_HEAD_DIM, B_HEAD_DIM), F32)
    zero_n = jnp.zeros((nbp, B_HEADS, B_HEAD_DIM), F32)
    zero_m = jnp.zeros((nbp, CHUNK, LANE), F32)
    zero_conv = jnp.zeros((nbp, 1, D_FF), F32)
    cache_t = jnp.swapaxes(cache_mla, 2, 3)

    outs = {k: [] for k in ('lat_p', 'lat_s', 'mk', 'mv', 'cp', 'np', 'mp', 'cs', 'ns', 'ms', 'conv_p', 'conv_s',
                            'chunkv')}
    yp = ys = None
    for l in range(depth):
        lw = _layer_weights(l, w)
        last = l == depth - 1

        mla_p = functools.partial(_mla_prompt, wuv=lw['wuv'], nb=nbp, seq=S, tq=tq, tk=tk)
        ymix, st, lat, _ = _mixer_block(
            xp, lw, mlstm_state=(zero_c, zero_n, zero_m), pos_tables=pos_p,
            chunk_w=_chunk_mixers(chunk_ws[l], chunk_b[l], CHUNK), tm=tm_p, act_dtype=BF16, ns=1, n_tiles=nbp,
            n_chunks=S // CHUNK, mla_fn=lambda qc, lt: mla_p(qc, lt))
        xp, qx_p = _outproj(xp, *ymix, lw['wo'], lw['nw_x'], lw['xwq'], tm=tm_p, q_dtype=BF16)
        outs['lat_p'].append(lat.reshape(nbp, S, LATENT))
        outs['cp'].append(st[0])
        outs['np'].append(st[1])
        outs['mp'].append(st[2][:, CHUNK - 1, :B_HEADS])

        m0_rows = jnp.concatenate([jnp.repeat(state_mlstm_m[l], T, axis=0),
                                   jnp.zeros((ms, LANE - B_HEADS), F32)], axis=1).reshape(ms // CHUNK, CHUNK, LANE)
        mla_s = functools.partial(_mla_sample, page_table, wuv=lw['wuv'], cache_t=cache_t, layer=l, t_new=T)
        ymix, st, lat, gv = _mixer_block(
            xs, lw, mlstm_state=(state_mlstm_C[l], state_mlstm_n[l], m0_rows), pos_tables=pos_s,
            chunk_w=_chunk_mixers(chunk_ws[l], chunk_b[l], T), tm=tm_s, act_dtype=F32, ns=ns_s,
            n_tiles=ms // CHUNK, n_chunks=1, mla_fn=lambda qc, lt: mla_s(qc, lt))
        xs, qx_s = _outproj(xs, *ymix, lw['wo'], lw['nw_x'], lw['xwq'], tm=tm_s, q_dtype=F32)
        outs['lat_s'].append(lat.reshape(nbs, T, LATENT))
        outs['cs'].append(st[0])
        outs['ns'].append(st[1])
        outs['ms'].append(st[2].reshape(nbs, T, LANE)[:, T - 1, :B_HEADS])
        outs['chunkv'].append(gv.reshape(nbs, T, A_WIDTH))

        mk, mv, mk_b, mv_b = _memkv(mem, lw['nw_mem'], lw['xwk'], lw['xwv'], tm=min(512, nbp * MEM_LEN))
        outs['mk'].append(mk.reshape(nbp, MEM_LEN, X_HEADS, X_HEAD_DIM))
        outs['mv'].append(mv.reshape(nbp, MEM_LEN, X_HEADS, X_HEAD_DIM))
        xp = _xattn_prompt(xp, qx_p, mk_b, mv_b, lw['xwo'], tm=tm_p, tiles_per_seq=S // tm_p)
        xs = _xattn_sample(xs, qx_s, cache_mem_k, cache_mem_v, lw['xwo'], layer=l, n_seq=xg, t_rows=T)

        res_p = _ffn(xp, lw['nw_ffn'], lw['wg'], lw['wu'], lw['wd'], lw['cw'], lw['cb'], zero_conv, zero_conv, fnw,
                     tm=tm_ffn_p, seq_rows=0, tiles_per_seq=S // tm_ffn_p, final_norm=last)
        buf = state_ffn_conv[l]
        p0 = jnp.repeat(buf[:, 0, :], T, axis=0)
        p1 = jnp.repeat(buf[:, 1, :], T, axis=0)
        res_s = _ffn(xs, lw['nw_ffn'], lw['wg'], lw['wu'], lw['wd'], lw['cw'], lw['cb'], p0, p1, fnw,
                     tm=tm_ffn_s, seq_rows=T, tiles_per_seq=1, final_norm=last)
        xp, xs = res_p[0], res_s[0]
        outs['conv_p'].append(res_p[1][S // tm_ffn_p - 1::S // tm_ffn_p])
        outs['conv_s'].append(res_s[1].reshape(nbs, T, D_FF)[:, T - (CONV_W - 1):, :])
        if last:
            yp, ys = res_p[2], res_s[2]

    st = lambda k: jnp.stack(outs[k])
    return (yp.reshape(nbp, S, D_MODEL), ys.reshape(nbs, T, D_MODEL),
            st('lat_p'), st('lat_s'), st('mk'), st('mv'),
            st('cp'), st('np'), st('mp'), st('cs'), st('ns'), st('ms'),
            st('conv_p'), st('conv_s'), st('chunkv'))
```

```python
import functools
import math

import numpy as np
import jax
import jax.numpy as jnp
from jax import lax
from jax.experimental import pallas as pl
from jax.experimental.pallas import tpu as pltpu

F32 = jnp.float32
BF16 = jnp.bfloat16

D_MODEL = 1024
PAGE_SIZE = 128
A_HEADS = 4
A_WIDTH = 256
A_HEAD_DIM = 64
CHUNK = 128
B_HEADS = 4
B_WIDTH = 384
B_HEAD_DIM = 96
MLSTM_KSCALE = B_HEAD_DIM ** -0.5
C_WIDTH = 384
C_HEADS = 6
C_V_DIM = 64
C_NOPE = 64
C_ROPE = 32
Q_LORA = 192
KV_LORA = 128
LATENT = KV_LORA + C_ROPE
MLA_SCALE = (C_NOPE + C_ROPE) ** -0.5
ROPE_THETA = 10000.0
MEM_LEN = 256
X_HEADS = 4
X_HEAD_DIM = 256
XA_SCALE = X_HEAD_DIM ** -0.5
D_FF = 2816
CONV_W = 3
EPS = 1e-6

LANE = 128
HP = 128
BW_P = B_HEADS * HP
FF_CHUNK = 256
N_FF_CHUNKS = D_FF // FF_CHUNK
VMEM_LIMIT = 48 * 1024 * 1024

S_AU, S_AV, S_BQ, S_BK, S_BV, S_BO = 0, 256, 512, 1024, 1536, 2048
S_CQ, S_CKV, S_R1, S_R2, S_G = 2560, 2816, 2944, 3072, 3200
N_IN = 3328
CQ_P = 256
UQ_NOPE, UQ_RA, UQ_RB, N_UQ = 0, 384, 1152, 1920
QC_SLOT = 256
N_MIX_P = A_WIDTH + BW_P + C_WIDTH


def _rms(x, w, n):
    ms = jnp.sum(x * x, axis=-1, keepdims=True) * (1.0 / n)
    return x * lax.rsqrt(ms + EPS) * w


def _gelu(x):
    c = math.sqrt(2.0 / math.pi)
    return x * (0.5 * (1.0 + jnp.tanh(c * (x + 0.044715 * (x * x * x)))))


def _sigmoid(x):
    return 1.0 / (1.0 + jnp.exp(-x))


def _log_sigmoid(x):
    return jnp.minimum(x, 0.0) - jnp.log1p(jnp.exp(-jnp.abs(x)))


def _dot(a, b):
    return jnp.dot(a, b, preferred_element_type=F32)


def _dot_nt(a, b):
    return lax.dot_general(a, b, (((1,), (1,)), ((), ())), preferred_element_type=F32)


def _dot_tn(a, b):
    return lax.dot_general(a, b, (((0,), (0,)), ((), ())), preferred_element_type=F32)


def _split3(a):
    a1 = a.astype(BF16)
    r1 = a - a1.astype(F32)
    a2 = r1.astype(BF16)
    a3 = (r1 - a2.astype(F32)).astype(BF16)
    return a1, a2, a3


def _wspec(shape):
    nd = len(shape)
    return pl.BlockSpec(shape, lambda *_: (0,) * nd, pipeline_mode=pl.Buffered(1))


def _params(n_axes):
    return pltpu.CompilerParams(dimension_semantics=("arbitrary",) * n_axes,
                                vmem_limit_bytes=VMEM_LIMIT)


def _inproj_kernel(x_ref, nw_ref, w1_ref, qnw_ref, kvnw_ref, wuq_ref, wuk_ref, gb_ref, cos_ref, sin_ref,
                   gu_ref, gv_ref, q_ref, k_ref, v_ref, so_ref, qc_ref, lat_ref, gate_ref):
    xn = _rms(x_ref[...], nw_ref[...], D_MODEL).astype(BF16)

    def seg(a, b):
        return _dot(xn, w1_ref[:, a:b])

    gu_ref[...] = _gelu(seg(S_AU, S_AV))
    gv_ref[...] = _gelu(seg(S_AV, S_BQ))
    q_ref[...] = seg(S_BQ, S_BK).astype(q_ref.dtype)
    k_ref[...] = (seg(S_BK, S_BV) * MLSTM_KSCALE).astype(k_ref.dtype)
    v_ref[...] = seg(S_BV, S_BO).astype(v_ref.dtype)
    so_ref[...] = _sigmoid(seg(S_BO, S_CQ))

    c_q = _rms(seg(S_CQ, S_CKV), qnw_ref[...], Q_LORA).astype(BF16)
    q2 = _dot(c_q, wuq_ref[...])
    qlat = _dot(q2[:, UQ_NOPE:UQ_RA].astype(BF16), wuk_ref[...])
    cos = cos_ref[...]
    sin = sin_ref[...]
    for h in range(C_HEADS):
        ra = q2[:, UQ_RA + LANE * h:UQ_RA + LANE * (h + 1)]
        rb = q2[:, UQ_RB + LANE * h:UQ_RB + LANE * (h + 1)]
        qr = ra * cos + rb * sin
        qc_ref[:, QC_SLOT * h:QC_SLOT * h + LANE] = (qlat[:, LANE * h:LANE * (h + 1)] * MLA_SCALE).astype(qc_ref.dtype)
        qc_ref[:, QC_SLOT * h + LANE:QC_SLOT * (h + 1)] = (qr * MLA_SCALE).astype(qc_ref.dtype)

    lat_ref[:, 0:KV_LORA] = _rms(seg(S_CKV, S_R1), kvnw_ref[...], KV_LORA)
    kr = seg(S_R1, S_R2) * cos + seg(S_R2, S_G) * sin
    lat_ref[:, KV_LORA:LATENT] = kr[:, :C_ROPE]

    gt = seg(S_G, N_IN) + gb_ref[...]
    lane = lax.broadcasted_iota(jnp.int32, gt.shape, 1)
    gate_ref[...] = jnp.where(lane < B_HEADS, gt, jnp.where(lane < 2 * B_HEADS, _log_sigmoid(gt), 0.0))


def _inproj(x, nw, w1, qnw, kvnw, wuq, wuk, gb, cos_t, sin_t, *, tm, act_dtype):
    m = x.shape[0]
    n_pos_blocks = cos_t.shape[0] // tm
    row = lambda w: pl.BlockSpec((tm, w), lambda i: (i, 0))
    pos = pl.BlockSpec((tm, LANE), lambda i: (i % n_pos_blocks, 0))
    out_shapes = (
        jax.ShapeDtypeStruct((m, A_WIDTH), F32), jax.ShapeDtypeStruct((m, A_WIDTH), F32),
        jax.ShapeDtypeStruct((m, BW_P), act_dtype), jax.ShapeDtypeStruct((m, BW_P), act_dtype),
        jax.ShapeDtypeStruct((m, BW_P), act_dtype), jax.ShapeDtypeStruct((m, BW_P), F32),
        jax.ShapeDtypeStruct((m, C_HEADS * QC_SLOT), act_dtype), jax.ShapeDtypeStruct((m, LATENT), F32),
        jax.ShapeDtypeStruct((m, LANE), F32))
    return pl.pallas_call(
        _inproj_kernel,
        grid=(m // tm,),
        in_specs=[row(D_MODEL), _wspec((1, D_MODEL)), _wspec((D_MODEL, N_IN)), _wspec((1, CQ_P)),
                  _wspec((1, KV_LORA)), _wspec((CQ_P, N_UQ)), _wspec((C_HEADS * C_NOPE, C_HEADS * LANE)),
                  _wspec((1, LANE)), pos, pos],
        out_specs=(row(A_WIDTH), row(A_WIDTH), row(BW_P), row(BW_P), row(BW_P), row(BW_P),
                   row(C_HEADS * QC_SLOT), row(LATENT), row(LANE)),
        out_shape=out_shapes,
        compiler_params=_params(1),
        name="inproj",
    )(x, nw, w1, qnw, kvnw, wuq, wuk, gb, cos_t, sin_t)


def _chunkmlp_kernel(gu_ref, gv_ref, ws_ref, bias_ref, ya_ref, *, n_chunks):
    r = lax.broadcasted_iota(jnp.int32, (CHUNK, CHUNK), 0)
    c = lax.broadcasted_iota(jnp.int32, (CHUNK, CHUNK), 1)
    wsm = [jnp.where(r >= c, ws_ref[h], 0.0).astype(BF16) for h in range(A_HEADS)]
    head_of_lane = lax.broadcasted_iota(jnp.int32, (CHUNK, A_WIDTH), 1) >> (A_HEAD_DIM.bit_length() - 1)
    for ci in range(n_chunks):
        rows = slice(ci * CHUNK, (ci + 1) * CHUNK)
        gv = gv_ref[rows, :]
        z = bias_ref[...]
        for h in range(A_HEADS):
            z = z + _dot(wsm[h], jnp.where(head_of_lane == h, gv, 0.0).astype(BF16))
        ya_ref[rows, :] = gu_ref[rows, :] * z


def _chunkmlp(gu, gv, ws, bias, *, tm):
    m = gu.shape[0]
    row = pl.BlockSpec((tm, A_WIDTH), lambda i: (i, 0))
    return pl.pallas_call(
        functools.partial(_chunkmlp_kernel, n_chunks=tm // CHUNK),
        grid=(m // tm,),
        in_specs=[row, row, _wspec((A_HEADS, CHUNK, CHUNK)), _wspec((CHUNK, A_WIDTH))],
        out_specs=row,
        out_shape=jax.ShapeDtypeStruct((m, A_WIDTH), F32),
        compiler_params=_params(1),
        name="chunkmlp",
    )(gu, gv, ws, bias)


_COL_M, _COL_G, _COL_RS = 0, 8, 16


def _mlstm_kernel(q_ref, k_ref, v_ref, so_ref, gcol_ref, grow_ref, m0_ref, c0_ref, n0_ref, nw_ref,
                  yb_ref, cout_ref, nout_ref, mout_ref,
                  cs_ref, ns_ref, mp_ref, col_ref, sv_ref, qc_ref, nr_ref, *, ns, ls, g_tiles):
    @pl.when(pl.program_id(1) == 0)
    def _init():
        cs_ref[...] = jnp.zeros(cs_ref.shape, F32)
        ns_ref[...] = jnp.zeros(ns_ref.shape, F32)
        for s in range(g_tiles):
            for g in range(ns):
                for h in range(B_HEADS):
                    cs_ref[s, g * B_HEADS + h, :B_HEAD_DIM, :B_HEAD_DIM] = c0_ref[s, g, h]
                    ns_ref[s, g * B_HEADS + h, 0:1, :B_HEAD_DIM] = n0_ref[s, g, h:h + 1, :]
        if ns == 1:
            for s in range(g_tiles):
                for h in range(B_HEADS):
                    mp_ref[s, h:h + 1, :] = jnp.broadcast_to(m0_ref[s, 0:1, h:h + 1], (1, LANE))
        else:
            mp_ref[...] = m0_ref[...]

    for s in range(g_tiles):
        at = lambda r: r.at[s]
        if ns == 1:
            _mlstm_chunk(at(q_ref), at(k_ref), at(v_ref), at(so_ref), at(gcol_ref), at(grow_ref), nw_ref,
                         at(yb_ref), at(cout_ref), at(nout_ref), at(mout_ref), at(cs_ref), at(ns_ref), at(mp_ref))
        else:
            _mlstm_tile(at(q_ref), at(k_ref), at(v_ref), at(so_ref), at(gcol_ref), at(grow_ref), nw_ref,
                        at(yb_ref), at(cout_ref), at(nout_ref), at(mout_ref),
                        at(cs_ref), at(ns_ref), at(mp_ref), at(col_ref), at(sv_ref), at(qc_ref), at(nr_ref),
                        ns=ns, ls=ls)


def _mlstm_chunk(q_ref, k_ref, v_ref, so_ref, gcol_ref, grow_ref, nw_ref,
                 yb_ref, cout_ref, nout_ref, mout_ref, cs_ref, ns_ref, mp_ref):
    L = CHUNK
    row = lax.broadcasted_iota(jnp.int32, (L, L), 0)
    col = lax.broadcasted_iota(jnp.int32, (L, L), 1)
    causal = col <= row
    cum = jnp.where(causal, 1.0, 0.0).astype(BF16)
    gcol = gcol_ref[...]
    grow = grow_ref[...]
    brow = sum(_dot_nt(t, cum) for t in _split3(grow))
    ones = jnp.ones((L, HP), BF16)
    mout_ref[...] = jnp.zeros(mout_ref.shape, F32)
    for h in range(B_HEADS):
        hs = slice(h * HP, (h + 1) * HP)
        qh = q_ref[:, hs].astype(BF16)
        kh = k_ref[:, hs].astype(BF16)
        vh = v_ref[:, hs].astype(BF16)
        ig = jnp.broadcast_to(gcol[:, h:h + 1], (L, HP))
        lf = jnp.broadcast_to(gcol[:, B_HEADS + h:B_HEADS + h + 1], (L, HP))
        bc = sum(_dot(cum, t) for t in _split3(lf))
        bm = bc + mp_ref[h:h + 1, :]
        logd = jnp.where(causal, grow[h:h + 1, :] + bc - brow[B_HEADS + h:B_HEADS + h + 1, :], -jnp.inf)
        m_t = jnp.maximum(bm, jnp.max(logd, axis=-1, keepdims=True))
        s = (_dot_nt(qh, kh) * jnp.exp(logd - m_t)).astype(BF16)
        sv = _dot(s, jnp.concatenate([vh, ones], axis=1))
        gdec = jnp.exp(bm - m_t)

        c_old = cs_ref[h]
        n_old = ns_ref[h, 0:1, :]
        num = gdec * _dot_nt(qh, c_old.astype(BF16)) + sv[:, 0:HP]
        den = gdec * jnp.sum(qh.astype(F32) * n_old, axis=-1, keepdims=True) + sv[:, HP:2 * HP]
        hh = num / jnp.maximum(jnp.abs(den), jnp.exp(-m_t))
        yb_ref[:, hs] = so_ref[:, hs] * _rms(hh, nw_ref[:, hs], B_HEAD_DIM)

        m_new, b_last, decay = m_t[L - 1:L, :], bc[L - 1:L, :], gdec[L - 1:L, :]
        w = jnp.exp(ig + b_last - bc - m_new)
        c_new = decay * c_old + _dot_tn((vh.astype(F32) * w).astype(BF16), kh)
        n_new = decay * n_old + jnp.sum(kh.astype(F32) * w, axis=0, keepdims=True)
        cs_ref[h] = c_new
        ns_ref[h, 0:1, :] = n_new
        mp_ref[h:h + 1, :] = m_new
        mout_ref[L - 1:L, h:h + 1] = m_new[:, 0:1]
        cout_ref[0, h] = c_new[:B_HEAD_DIM, :B_HEAD_DIM]
        nout_ref[0, h:h + 1, :] = n_new[:, :B_HEAD_DIM]


def _mlstm_tile(q_ref, k_ref, v_ref, so_ref, gcol_ref, grow_ref, nw_ref,
                yb_ref, cout_ref, nout_ref, mout_ref,
                cs_ref, ns_ref, mp_ref, col_ref, sv_ref, qc_ref, nr_ref, *, ns, ls):
    L = CHUNK
    ls_shift = ls.bit_length() - 1

    col_ref[...] = jnp.zeros(col_ref.shape, F32)
    row = lax.broadcasted_iota(jnp.int32, (L, L), 0)
    col = lax.broadcasted_iota(jnp.int32, (L, L), 1)
    causal = (col <= row) & ((col >> ls_shift) == (row >> ls_shift))
    cum = jnp.where(causal, 1.0, 0.0).astype(BF16)
    rid = lax.broadcasted_iota(jnp.int32, (L, 1), 0)

    gcol = gcol_ref[...]
    grow = grow_ref[...]
    bcol = sum(_dot(cum, t) for t in _split3(gcol))
    brow = sum(_dot_nt(t, cum) for t in _split3(grow))
    mp = mp_ref[...]

    qs, ks, vs = [], [], []
    for h in range(B_HEADS):
        hs = slice(h * HP, (h + 1) * HP)
        qh = q_ref[:, hs].astype(BF16)
        kh = k_ref[:, hs].astype(BF16)
        vh = v_ref[:, hs].astype(BF16)
        qs.append(qh), ks.append(kh), vs.append(vh)
        ig_r = grow[h:h + 1, :]
        b_r = brow[B_HEADS + h:B_HEADS + h + 1, :]
        b_c = bcol[:, B_HEADS + h:B_HEADS + h + 1]
        bm = b_c + mp[:, h:h + 1]
        logd = jnp.where(causal, ig_r + b_c - b_r, -jnp.inf)
        m_t = jnp.maximum(bm, jnp.max(logd, axis=-1, keepdims=True))
        d = jnp.exp(logd - m_t)
        s = _dot_nt(qh, kh) * d
        sv_ref[h] = _dot(s.astype(BF16), vh)
        col_ref[:, _COL_M + h:_COL_M + h + 1] = m_t
        col_ref[:, _COL_G + h:_COL_G + h + 1] = jnp.exp(bm - m_t)
        col_ref[:, _COL_RS + h:_COL_RS + h + 1] = jnp.sum(s, axis=-1, keepdims=True)

    qc_ref[...] = jnp.zeros(qc_ref.shape, F32)
    nr_ref[...] = jnp.zeros(nr_ref.shape, F32)

    def seq_update(g):
        last = g * ls + (ls - 1)
        in_seq = (rid >> ls_shift) == g
        is_last = rid == last
        for h in range(B_HEADS):
            c_old = cs_ref[g * B_HEADS + h]
            n_old = ns_ref[g * B_HEADS + h, 0:1, :]
            qm = jnp.where(in_seq, qs[h], jnp.zeros_like(qs[h])) if ns > 1 else qs[h]
            qc_ref[h] += _dot_nt(qm, c_old.astype(BF16))
            nr_ref[h] += jnp.where(in_seq, n_old, 0.0)
            m_t = col_ref[:, _COL_M + h:_COL_M + h + 1]
            gdec = col_ref[:, _COL_G + h:_COL_G + h + 1]
            b_c = bcol[:, B_HEADS + h:B_HEADS + h + 1]
            ig_c = gcol[:, h:h + 1]
            pick = lambda a: jnp.sum(jnp.where(is_last, a, 0.0), axis=0, keepdims=True)
            m_new, b_last, decay = pick(m_t), pick(b_c), pick(gdec)
            w_c = jnp.where(in_seq, jnp.exp(ig_c + b_last - b_c - m_new), 0.0)
            vw = (vs[h].astype(F32) * w_c).astype(BF16)
            cs_ref[g * B_HEADS + h] = decay * c_old + _dot_tn(vw, ks[h])
            ns_ref[g * B_HEADS + h, 0:1, :] = (decay * n_old
                                               + jnp.sum(ks[h].astype(F32) * w_c, axis=0, keepdims=True))
            if ns == 1:
                mp_ref[:, h:h + 1] = jnp.broadcast_to(m_new, (L, 1))

    if ns == 1:
        seq_update(0)
    else:
        lax.fori_loop(0, ns, lambda g, c: (seq_update(g), c)[1], 0)

    for h in range(B_HEADS):
        hs = slice(h * HP, (h + 1) * HP)
        m_t = col_ref[:, _COL_M + h:_COL_M + h + 1]
        gdec = col_ref[:, _COL_G + h:_COL_G + h + 1]
        rs = col_ref[:, _COL_RS + h:_COL_RS + h + 1]
        num = gdec * qc_ref[h] + sv_ref[h]
        qn = jnp.sum(qs[h].astype(F32) * nr_ref[h], axis=-1, keepdims=True)
        den = gdec * qn + rs
        hh = num / jnp.maximum(jnp.abs(den), jnp.exp(-m_t))
        yb_ref[:, hs] = so_ref[:, hs] * _rms(hh, nw_ref[:, hs], B_HEAD_DIM)

    mout_ref[...] = col_ref[...]
    for g in range(ns):
        for h in range(B_HEADS):
            cout_ref[g, h] = cs_ref[g * B_HEADS + h, :B_HEAD_DIM, :B_HEAD_DIM]
            nout_ref[g, h:h + 1, :] = ns_ref[g * B_HEADS + h, 0:1, :B_HEAD_DIM]


def _mlstm(q, k, v, so, gcol, grow, m0, c0, n0, nw, *, ns, n_tiles, n_chunks, g_tiles):
    m = q.shape[0]
    nb = c0.shape[0]
    ls = CHUNK // ns
    g = g_tiles
    t3 = lambda a: a.reshape(n_tiles, n_chunks * CHUNK, a.shape[-1])
    rows = lambda w: pl.BlockSpec((g, CHUNK, w), lambda t, c: (t, c, 0))
    cspec = pl.BlockSpec((g, ns, B_HEADS, B_HEAD_DIM, B_HEAD_DIM), lambda t, c: (t, 0, 0, 0, 0))
    nspec = pl.BlockSpec((g, ns, B_HEADS, B_HEAD_DIM), lambda t, c: (t, 0, 0, 0))
    mspec = pl.BlockSpec((g, CHUNK, LANE), lambda t, c: (t, 0, 0))
    yb, c_new, n_new, m_rows = pl.pallas_call(
        functools.partial(_mlstm_kernel, ns=ns, ls=ls, g_tiles=g),
        grid=(n_tiles // g, n_chunks),
        in_specs=[rows(BW_P), rows(BW_P), rows(BW_P), rows(BW_P), rows(LANE),
                  pl.BlockSpec((g, None, 16, CHUNK), lambda t, c: (t, c, 0, 0)),
                  mspec, cspec, nspec, _wspec((1, BW_P))],
        out_specs=(rows(BW_P), cspec, nspec, mspec),
        out_shape=(jax.ShapeDtypeStruct((n_tiles, n_chunks * CHUNK, BW_P), F32),
                   jax.ShapeDtypeStruct((n_tiles, ns, B_HEADS, B_HEAD_DIM, B_HEAD_DIM), F32),
                   jax.ShapeDtypeStruct((n_tiles, ns, B_HEADS, B_HEAD_DIM), F32),
                   jax.ShapeDtypeStruct((n_tiles, CHUNK, LANE), F32)),
        scratch_shapes=[pltpu.VMEM((g, ns * B_HEADS, HP, HP), F32), pltpu.VMEM((g, ns * B_HEADS, 8, HP), F32),
                        pltpu.VMEM((g, CHUNK, LANE), F32), pltpu.VMEM((g, CHUNK, LANE), F32),
                        pltpu.VMEM((g, B_HEADS, CHUNK, HP), F32), pltpu.VMEM((g, B_HEADS, CHUNK, HP), F32),
                        pltpu.VMEM((g, B_HEADS, CHUNK, HP), F32)],
        compiler_params=_params(2),
        name="mlstm",
    )(t3(q), t3(k), t3(v), t3(so), t3(gcol), grow.reshape(n_tiles, n_chunks, 16, CHUNK), m0,
      c0.reshape(n_tiles, ns, B_HEADS, B_HEAD_DIM, B_HEAD_DIM), n0.reshape(n_tiles, ns, B_HEADS, B_HEAD_DIM), nw)
    return (yb.reshape(m, BW_P), c_new.reshape(nb, B_HEADS, B_HEAD_DIM, B_HEAD_DIM),
            n_new.reshape(nb, B_HEADS, B_HEAD_DIM), m_rows)


def _stack_heads(qc_ref, dtype):
    return jnp.concatenate([qc_ref[:, QC_SLOT * h:QC_SLOT * h + LATENT] for h in range(C_HEADS)],
                           axis=0).astype(dtype)


def _unstack_project(o, t, wuv_ref):
    wide = jnp.concatenate([o[h * t:(h + 1) * t] for h in range(C_HEADS)], axis=1)
    return _dot(wide.astype(BF16), wuv_ref[...])


def _softmax_step(s, pv, m_ref, acc_ref):
    reps = s.shape[1] // LANE
    m_prev = m_ref[...]
    m_new = jnp.maximum(m_prev, jnp.max(s, axis=-1, keepdims=True))
    p = jnp.exp(s - jnp.concatenate([m_new] * reps, axis=1))
    corr = jnp.exp(m_prev - m_new)
    acc_ref[...] = jnp.concatenate([corr, corr], axis=1) * acc_ref[...] + pv(p.astype(BF16))
    m_ref[...] = m_new


def _softmax_result(acc_ref):
    acc = acc_ref[...]
    return acc[:, :KV_LORA] / acc[:, KV_LORA:]


def _mla_prompt_kernel(qc_ref, lat_ref, wuv_ref, yc_ref, kb_ref, vb_ref, m_ref, acc_ref, *, tq, tk, row_groups):
    qi = pl.program_id(1)

    @pl.when(qi == 0)
    def _cast_keys():
        lat = lat_ref[...]
        kb_ref[...] = lat.astype(BF16)
        vb_ref[:, 0:KV_LORA] = lat[:, 0:KV_LORA].astype(BF16)
        vb_ref[:, KV_LORA:2 * KV_LORA] = jnp.ones((lat.shape[0], KV_LORA), BF16)

    qst = _stack_heads(qc_ref, BF16)
    m_ref[...] = jnp.full(m_ref.shape, -jnp.inf, F32)
    acc_ref[...] = jnp.zeros(acc_ref.shape, F32)
    rows = C_HEADS * tq

    gr = rows // row_groups

    def block(start, width, masked):
        kj = kb_ref[pl.ds(start, width), :]
        vj = vb_ref[pl.ds(start, width), :]
        for g in range(row_groups):
            rs = slice(g * gr, (g + 1) * gr)
            s = _dot_nt(qst[rs], kj)
            if masked:
                r = g * gr + lax.broadcasted_iota(jnp.int32, (gr, width), 0)
                qpos = qi * tq + (r & (tq - 1))
                kpos = start + lax.broadcasted_iota(jnp.int32, (gr, width), 1)
                s = jnp.where(kpos <= qpos, s, -jnp.inf)
            _softmax_step(s, lambda p: _dot(p, vj), m_ref.at[rs], acc_ref.at[rs])

    n_full = (qi * tq) // tk
    lax.fori_loop(0, n_full, lambda j, c: (block(pl.multiple_of(j * tk, tk), tk, False), c)[1], 0)
    block(pl.multiple_of(n_full * tk, tk), tk, True)
    yc_ref[...] = _unstack_project(_softmax_result(acc_ref), tq, wuv_ref)


def _mla_prompt(qc, lat, wuv, *, nb, seq, tq, tk):
    m = qc.shape[0]
    n_q = seq // tq
    rows = C_HEADS * tq
    assert tk in (tq, 2 * tq)
    return pl.pallas_call(
        functools.partial(_mla_prompt_kernel, tq=tq, tk=tk, row_groups=6),
        grid=(nb, n_q),
        in_specs=[pl.BlockSpec((tq, C_HEADS * QC_SLOT), lambda b, i: (b * n_q + i, 0)),
                  pl.BlockSpec((seq, LATENT), lambda b, i: (b, 0)),
                  _wspec((C_HEADS * KV_LORA, C_WIDTH))],
        out_specs=pl.BlockSpec((tq, C_WIDTH), lambda b, i: (b * n_q + i, 0)),
        out_shape=jax.ShapeDtypeStruct((m, C_WIDTH), F32),
        scratch_shapes=[pltpu.VMEM((seq, LATENT), BF16), pltpu.VMEM((seq, 2 * KV_LORA), BF16),
                        pltpu.VMEM((rows, LANE), F32), pltpu.VMEM((rows, 2 * KV_LORA), F32)],
        compiler_params=_params(2),
        name="mla_prompt",
    )(qc, lat, wuv)


def _mla_sample_kernel(pt_ref, qc_ref, latn_ref, wuv_ref, cache_ref, yc_ref, buf_ref, sem_ref, kt_ref, kn_ref,
                       *, layer, n_pages, t_new, seqs):
    b = pl.program_id(0)
    nb = pl.num_programs(0)
    slot = b & 1
    rows = C_HEADS * t_new

    def page_copy(step, r, page, to_slot):
        return pltpu.make_async_copy(cache_ref.at[layer, pt_ref[step * seqs + r, page]],
                                     buf_ref.at[to_slot, r, page], sem_ref.at[to_slot])

    def request(step, to_slot):
        for r in range(seqs):
            for page in range(n_pages):
                page_copy(step, r, page, to_slot).start()

    @pl.when(b == 0)
    def _first():
        request(0, 0)

    @pl.when(b + 1 < nb)
    def _prefetch_next():
        request(b + 1, 1 - slot)

    for r in range(seqs):
        for page in range(n_pages):
            page_copy(b, r, page, slot).wait()

    tpos = lax.broadcasted_iota(jnp.int32, (rows, PAGE_SIZE), 0) & (t_new - 1)
    kidx = lax.broadcasted_iota(jnp.int32, (rows, PAGE_SIZE), 1)
    for r in range(seqs):
        rs = slice(r * t_new, (r + 1) * t_new)
        qst = jnp.concatenate([qc_ref[rs, QC_SLOT * h:QC_SLOT * h + LATENT] for h in range(C_HEADS)],
                              axis=0).astype(BF16)
        for page in range(n_pages):
            kt_ref[r, :, page * PAGE_SIZE:(page + 1) * PAGE_SIZE] = buf_ref[slot, r, page].astype(BF16)
        s_past = _dot(qst, kt_ref[r])

        kn_ref[r] = jnp.zeros(kn_ref.shape[1:], F32)
        kn_ref[r, 0:t_new, :] = latn_ref[rs, :]
        kn = kn_ref[r].astype(BF16)
        s_new = jnp.where(kidx <= tpos, _dot_nt(qst, kn), -jnp.inf)

        m = jnp.maximum(jnp.max(s_past, axis=-1, keepdims=True), jnp.max(s_new, axis=-1, keepdims=True))
        p_past = jnp.exp(s_past - m).astype(BF16)
        p_new = jnp.exp(s_new - m).astype(BF16)
        denom = (jnp.sum(p_past.astype(F32), axis=-1, keepdims=True)
                 + jnp.sum(p_new.astype(F32), axis=-1, keepdims=True))
        o = _dot_nt(p_past, kt_ref[r, 0:KV_LORA, :]) + _dot(p_new, kn[:, 0:KV_LORA])
        yc_ref[rs, :] = _unstack_project(o / denom, t_new, wuv_ref)


def _mla_sample(page_table, qc, lat_new, wuv, cache_t, *, layer, t_new, seqs):
    m = qc.shape[0]
    nb, n_pages = page_table.shape
    tr = seqs * t_new
    grid_spec = pltpu.PrefetchScalarGridSpec(
        num_scalar_prefetch=1,
        grid=(nb // seqs,),
        in_specs=[pl.BlockSpec((tr, C_HEADS * QC_SLOT), lambda b, pt: (b, 0)),
                  pl.BlockSpec((tr, LATENT), lambda b, pt: (b, 0)),
                  pl.BlockSpec((C_HEADS * KV_LORA, C_WIDTH), lambda b, pt: (0, 0)),
                  pl.BlockSpec(memory_space=pl.ANY)],
        out_specs=pl.BlockSpec((tr, C_WIDTH), lambda b, pt: (b, 0)),
        scratch_shapes=[pltpu.VMEM((2, seqs, n_pages, LATENT, PAGE_SIZE), F32), pltpu.SemaphoreType.DMA((2,)),
                        pltpu.VMEM((seqs, LATENT, n_pages * PAGE_SIZE), BF16),
                        pltpu.VMEM((seqs, PAGE_SIZE, LATENT), F32)])
    return pl.pallas_call(
        functools.partial(_mla_sample_kernel, layer=layer, n_pages=n_pages, t_new=t_new, seqs=seqs),
        grid_spec=grid_spec,
        out_shape=jax.ShapeDtypeStruct((m, C_WIDTH), F32),
        compiler_params=_params(1),
        name="mla_sample",
    )(page_table, qc, lat_new, wuv, cache_t)


def _outproj_kernel(x_ref, ya_ref, yb_ref, yc_ref, wo_ref, nxw_ref, wq_ref, xo_ref, q_ref):
    y = (_dot(ya_ref[...].astype(BF16), wo_ref[0:A_WIDTH, :])
         + _dot(yb_ref[...].astype(BF16), wo_ref[A_WIDTH:A_WIDTH + BW_P, :])
         + _dot(yc_ref[...].astype(BF16), wo_ref[A_WIDTH + BW_P:N_MIX_P, :]))
    xn = x_ref[...] + y
    xo_ref[...] = xn
    hq = _rms(xn, nxw_ref[...], D_MODEL).astype(BF16)
    q_ref[...] = (_dot(hq, wq_ref[...]) * XA_SCALE).astype(q_ref.dtype)


def _outproj(x, ya, yb, yc, wo, nxw, wq, *, tm, q_dtype):
    m = x.shape[0]
    row = lambda w: pl.BlockSpec((tm, w), lambda i: (i, 0))
    return pl.pallas_call(
        _outproj_kernel,
        grid=(m // tm,),
        in_specs=[row(D_MODEL), row(A_WIDTH), row(BW_P), row(C_WIDTH), _wspec((N_MIX_P, D_MODEL)),
                  _wspec((1, D_MODEL)), _wspec((D_MODEL, D_MODEL))],
        out_specs=(row(D_MODEL), row(D_MODEL)),
        out_shape=(jax.ShapeDtypeStruct((m, D_MODEL), F32), jax.ShapeDtypeStruct((m, D_MODEL), q_dtype)),
        compiler_params=_params(1),
        name="outproj",
    )(x, ya, yb, yc, wo, nxw, wq)


def _memkv_kernel(mem_ref, nw_ref, wk_ref, wv_ref, k_ref, v_ref, kb_ref, vb_ref):
    mn = _rms(mem_ref[...], nw_ref[...], D_MODEL).astype(BF16)
    k = _dot(mn, wk_ref[...])
    v = _dot(mn, wv_ref[...])
    kb_ref[...] = k.astype(BF16)
    vb_ref[...] = v.astype(BF16)
    for h in range(X_HEADS):
        hs = slice(h * X_HEAD_DIM, (h + 1) * X_HEAD_DIM)
        k_ref[:, h, :] = k[:, hs]
        v_ref[:, h, :] = v[:, hs]


def _memkv(mem, nw, wk, wv, *, tm):
    m = mem.shape[0]
    row = pl.BlockSpec((tm, D_MODEL), lambda i: (i, 0))
    out = pl.BlockSpec((tm, X_HEADS, X_HEAD_DIM), lambda i: (i, 0, 0))
    return pl.pallas_call(
        _memkv_kernel,
        grid=(m // tm,),
        in_specs=[row, _wspec((1, D_MODEL)), _wspec((D_MODEL, D_MODEL)), _wspec((D_MODEL, D_MODEL))],
        out_specs=(out, out, row, row),
        out_shape=(jax.ShapeDtypeStruct((m, X_HEADS, X_HEAD_DIM), F32),) * 2
                  + (jax.ShapeDtypeStruct((m, D_MODEL), BF16),) * 2,
        compiler_params=_params(1),
        name="memkv",
    )(mem, nw, wk, wv)


def _softmax_rows(s):
    p = jnp.exp(s - jnp.max(s, axis=-1, keepdims=True))
    return p / jnp.sum(p, axis=-1, keepdims=True)


def _xattn_prompt_kernel(x_ref, q_ref, k_ref, v_ref, wo_ref, xo_ref, o_ref):
    for h in range(X_HEADS):
        hs = slice(h * X_HEAD_DIM, (h + 1) * X_HEAD_DIM)
        p = _softmax_rows(_dot_nt(q_ref[:, hs], k_ref[:, hs]))
        o_ref[:, hs] = _dot(p.astype(BF16), v_ref[:, hs])
    xo_ref[...] = x_ref[...] + _dot(o_ref[...].astype(BF16), wo_ref[...])


def _xattn_prompt(x, q, k, v, wo, *, tm, tiles_per_seq):
    m = x.shape[0]
    row = pl.BlockSpec((tm, D_MODEL), lambda i: (i, 0))
    kv = pl.BlockSpec((MEM_LEN, D_MODEL), lambda i: (i // tiles_per_seq, 0))
    return pl.pallas_call(
        _xattn_prompt_kernel,
        grid=(m // tm,),
        in_specs=[row, row, kv, kv, _wspec((D_MODEL, D_MODEL))],
        out_specs=row,
        out_shape=jax.ShapeDtypeStruct((m, D_MODEL), F32),
        scratch_shapes=[pltpu.VMEM((tm, D_MODEL), F32)],
        compiler_params=_params(1),
        name="xattn_prompt",
    )(x, q, k, v, wo)


def _xattn_sample_kernel(x_ref, q_ref, k_ref, v_ref, wo_ref, xo_ref, o_ref, *, n_seq, t_rows):
    rows = X_HEADS * t_rows
    nk = MEM_LEN * X_HEADS
    r = lax.broadcasted_iota(jnp.int32, (rows, nk), 0)
    c = lax.broadcasted_iota(jnp.int32, (rows, nk), 1)
    keep = (c & (X_HEADS - 1)) == (r >> (t_rows.bit_length() - 1))
    for g in range(n_seq):
        rs = slice(g * t_rows, (g + 1) * t_rows)
        qst = jnp.concatenate([q_ref[rs, h * X_HEAD_DIM:(h + 1) * X_HEAD_DIM] for h in range(X_HEADS)], axis=0)
        k2 = k_ref[g].reshape(nk, X_HEAD_DIM).astype(BF16)
        v2 = v_ref[g].reshape(nk, X_HEAD_DIM).astype(BF16)
        p = _softmax_rows(jnp.where(keep, _dot_nt(qst.astype(BF16), k2), -jnp.inf))
        o = _dot(p.astype(BF16), v2)
        for h in range(X_HEADS):
            o_ref[rs, h * X_HEAD_DIM:(h + 1) * X_HEAD_DIM] = o[h * t_rows:(h + 1) * t_rows]
    xo_ref[...] = x_ref[...] + _dot(o_ref[...].astype(BF16), wo_ref[...])


def _xattn_sample(x, q, k, v, wo, *, layer, n_seq, t_rows):
    m = x.shape[0]
    r = n_seq * t_rows
    row = pl.BlockSpec((r, D_MODEL), lambda i: (i, 0))
    kv = pl.BlockSpec((None, n_seq, MEM_LEN, X_HEADS, X_HEAD_DIM), lambda i: (layer, i, 0, 0, 0))
    return pl.pallas_call(
        functools.partial(_xattn_sample_kernel, n_seq=n_seq, t_rows=t_rows),
        grid=(m // r,),
        in_specs=[row, row, kv, kv, _wspec((D_MODEL, D_MODEL))],
        out_specs=row,
        out_shape=jax.ShapeDtypeStruct((m, D_MODEL), F32),
        scratch_shapes=[pltpu.VMEM((r, D_MODEL), F32)],
        compiler_params=_params(1),
        name="xattn_sample",
    )(x, q, k, v, wo)


def _ffn_kernel(x_ref, nw_ref, wg_ref, wu_ref, wd_ref, cw_ref, cb_ref, *rest,
                seq_rows, tiles_per_seq, final_norm):
    if seq_rows:
        p_ref, fnw_ref, *rest = rest
    else:
        p0_ref, p1_ref, fnw_ref, *rest = rest
    if final_norm:
        xo_ref, gtail_ref, y_ref, carry_ref, act_ref = rest
    else:
        xo_ref, gtail_ref, carry_ref, act_ref = rest
    tm = x_ref.shape[0]
    x = x_ref[...]
    xn = _rms(x, nw_ref[...], D_MODEL).astype(BF16)
    rid = lax.broadcasted_iota(jnp.int32, (tm, 1), 0)
    if seq_rows:
        pos = rid & (seq_rows - 1)
    else:
        pos = rid

        @pl.when(pl.program_id(0) % tiles_per_seq == 0)
        def _seq_start():
            carry_ref[0:1, :] = p0_ref[...]
            carry_ref[1:2, :] = p1_ref[...]

    for c in range(N_FF_CHUNKS):
        cs = slice(c * FF_CHUNK, (c + 1) * FF_CHUNK)
        g = _dot(xn, wg_ref[:, cs])
        u = _dot(xn, wu_ref[:, cs])
        if seq_rows:
            n_seq = tm // seq_rows
            to_rows = lambda a: jnp.broadcast_to(a, (n_seq, seq_rows, FF_CHUNK)).reshape(tm, FF_CHUNK)
            prev0, prev1 = to_rows(p_ref[:, 0:1, cs]), to_rows(p_ref[:, 1:2, cs])
            gtail_ref[:, :, cs] = g.reshape(n_seq, seq_rows, FF_CHUNK)[:, seq_rows - 2:seq_rows, :]
        else:
            prev0, prev1 = carry_ref[0:1, cs], carry_ref[1:2, cs]
            gtail_ref[:, cs] = g[tm - 2:tm, :]
            carry_ref[0:2, cs] = g[tm - 2:tm, :]
        g1 = jnp.where(pos == 0, prev1, pltpu.roll(g, 1, 0))
        g2 = jnp.where(pos == 0, prev0, jnp.where(pos == 1, prev1, pltpu.roll(g, 2, 0)))
        y = cb_ref[:, cs] + g2 * cw_ref[0:1, cs] + g1 * cw_ref[1:2, cs] + g * cw_ref[2:3, cs]
        act_ref[:, cs] = (_gelu(y) * u).astype(BF16)
    xo = x + _dot(act_ref[...], wd_ref[...])
    xo_ref[...] = xo
    if final_norm:
        y_ref[...] = _rms(xo, fnw_ref[...], D_MODEL)


def _ffn(x, nw, wg, wu, wd, cw, cb, state, fnw, *, tm, seq_rows, tiles_per_seq, final_norm, layer=0):
    m = x.shape[0]
    n_tiles = m // tm
    row = pl.BlockSpec((tm, D_MODEL), lambda i: (i, 0))
    if seq_rows:
        n_seq = tm // seq_rows
        pspecs = [pl.BlockSpec((None, n_seq, CONV_W - 1, D_FF), lambda i: (layer, i, 0, 0))]
        gspec = pl.BlockSpec((n_seq, CONV_W - 1, D_FF), lambda i: (i, 0, 0))
        gshape = jax.ShapeDtypeStruct((m // seq_rows, CONV_W - 1, D_FF), F32)
        states = (state,)
    else:
        pspecs = [pl.BlockSpec((None, 1, D_FF), lambda i: (i // tiles_per_seq, 0, 0))] * 2
        gspec = pl.BlockSpec((None, 2, D_FF), lambda i: (i, 0, 0))
        gshape = jax.ShapeDtypeStruct((n_tiles, 2, D_FF), F32)
        states = tuple(state)
    return pl.pallas_call(
        functools.partial(_ffn_kernel, seq_rows=seq_rows, tiles_per_seq=tiles_per_seq, final_norm=final_norm),
        grid=(n_tiles,),
        in_specs=[row, _wspec((1, D_MODEL)), _wspec((D_MODEL, D_FF)), _wspec((D_MODEL, D_FF)),
                  _wspec((D_FF, D_MODEL)), _wspec((CONV_W, D_FF)), _wspec((1, D_FF))]
                 + pspecs + [_wspec((1, D_MODEL))],
        out_specs=(row, gspec) + ((row,) if final_norm else ()),
        out_shape=(jax.ShapeDtypeStruct((m, D_MODEL), F32), gshape)
                  + ((jax.ShapeDtypeStruct((m, D_MODEL), F32),) if final_norm else ()),
        scratch_shapes=[pltpu.VMEM((8, D_FF), F32), pltpu.VMEM((tm, D_FF), BF16)],
        compiler_params=_params(1),
        name="ffn",
    )(x, nw, wg, wu, wd, cw, cb, *states, fnw)


def _pad_heads_idx(start):
    idx = np.full((BW_P,), -1, np.int64)
    for h in range(B_HEADS):
        idx[h * HP:h * HP + B_HEAD_DIM] = start + h * B_HEAD_DIM + np.arange(B_HEAD_DIM)
    return idx


def _take_cols(w, idx, axis=-1):
    axis = axis % w.ndim
    pieces, i, n = [], 0, len(idx)
    while i < n:
        j = i + 1
        if idx[i] < 0:
            while j < n and idx[j] < 0:
                j += 1
            shape = w.shape[:axis] + (j - i,) + w.shape[axis + 1:]
            pieces.append(jnp.zeros(shape, w.dtype))
        else:
            while j < n and idx[j] == idx[j - 1] + 1:
                j += 1
            pieces.append(lax.slice_in_dim(w, int(idx[i]), int(idx[i]) + (j - i), axis=axis))
        i = j
    return jnp.concatenate(pieces, axis=axis)


def _win_layout():
    half = C_ROPE // 2
    base = {'au': 0, 'av': 256, 'bq': 512, 'bk': 896, 'bv': 1280, 'bo': 1664, 'bi': 2048, 'bf': 2052,
            'cq': 2056, 'ckv': 2248, 'ckr': 2376}
    idx = np.full((N_IN,), -1, np.int64)
    idx[S_AU:S_AU + 256] = base['au'] + np.arange(256)
    idx[S_AV:S_AV + 256] = base['av'] + np.arange(256)
    for s, name in ((S_BQ, 'bq'), (S_BK, 'bk'), (S_BV, 'bv'), (S_BO, 'bo')):
        idx[s:s + BW_P] = _pad_heads_idx(base[name])
    idx[S_CQ:S_CQ + Q_LORA] = base['cq'] + np.arange(Q_LORA)
    idx[S_CKV:S_CKV + KV_LORA] = base['ckv'] + np.arange(KV_LORA)
    idx[S_R1:S_R1 + C_ROPE] = base['ckr'] + np.arange(C_ROPE)
    idx[S_R2:S_R2 + half] = base['ckr'] + half + np.arange(half)
    idx[S_R2 + half:S_R2 + C_ROPE] = base['ckr'] + np.arange(half)
    idx[S_G:S_G + B_HEADS] = base['bi'] + np.arange(B_HEADS)
    idx[S_G + B_HEADS:S_G + 2 * B_HEADS] = base['bf'] + np.arange(B_HEADS)
    return idx


def _wuq_layout():
    half = C_ROPE // 2
    per = C_NOPE + C_ROPE
    idx = np.full((N_UQ,), -1, np.int64)
    for h in range(C_HEADS):
        idx[UQ_NOPE + h * C_NOPE:UQ_NOPE + (h + 1) * C_NOPE] = h * per + np.arange(C_NOPE)
        idx[UQ_RA + h * LANE:UQ_RA + h * LANE + C_ROPE] = h * per + C_NOPE + np.arange(C_ROPE)
        idx[UQ_RB + h * LANE:UQ_RB + h * LANE + half] = h * per + C_NOPE + half + np.arange(half)
        idx[UQ_RB + h * LANE + half:UQ_RB + h * LANE + C_ROPE] = h * per + C_NOPE + np.arange(half)
    return idx


def _rope_tables(pos, reps):
    half = C_ROPE // 2
    inv = ROPE_THETA ** (-jnp.arange(half, dtype=F32) / half)
    ang = pos.astype(F32)[:, None] * inv[None, :]
    cos, sin = jnp.cos(ang), jnp.sin(ang)
    z = jnp.zeros((pos.shape[0], LANE - C_ROPE), F32)
    ct = jnp.concatenate([cos, cos, z], axis=1)
    st = jnp.concatenate([-sin, sin, z], axis=1)
    return jnp.tile(ct, (reps, 1)), jnp.tile(st, (reps, 1))


def _pad_lanes(v, idx):
    return _take_cols(v[None, :], idx)


def _layer_weights(l, w):
    hp_idx = _pad_heads_idx(0)
    out = {}
    out['nw_mix'] = w['norm_mix_w'][l][None, :]
    out['w1'] = _take_cols(w['w_in'][l], _win_layout()).astype(BF16)
    out['qnw'] = _pad_lanes(w['mla_qnorm_w'][l], np.concatenate([np.arange(Q_LORA), np.full(CQ_P - Q_LORA, -1)]))
    out['kvnw'] = w['mla_kvnorm_w'][l][None, :]
    wuq = w['mla_w_uq'][l].reshape(Q_LORA, C_HEADS * (C_NOPE + C_ROPE))
    wuq = _take_cols(wuq, _wuq_layout())
    out['wuq'] = jnp.concatenate([wuq, jnp.zeros((CQ_P - Q_LORA, N_UQ), F32)], axis=0).astype(BF16)
    wuk = jnp.transpose(w['mla_w_uk'][l], (1, 2, 0))
    eye = jnp.eye(C_HEADS, dtype=F32)
    out['wuk'] = jnp.einsum('hdc,hg->hdgc', wuk, eye).reshape(C_HEADS * C_NOPE, C_HEADS * KV_LORA).astype(BF16)
    wuv = jnp.transpose(w['mla_w_uv'][l], (1, 0, 2))
    out['wuv'] = jnp.einsum('hce,hg->hcge', wuv, eye).reshape(C_HEADS * KV_LORA, C_WIDTH).astype(BF16)
    gb = jnp.concatenate([w['mlstm_bi'][l], w['mlstm_bf'][l]])
    out['gb'] = _pad_lanes(gb, np.concatenate([np.arange(2 * B_HEADS), np.full(LANE - 2 * B_HEADS, -1)]))
    out['mnw'] = _pad_lanes(w['mlstm_norm_w'][l], hp_idx)
    wo = w['w_out'][l]
    rows = np.concatenate([np.arange(A_WIDTH), np.where(hp_idx < 0, -1, hp_idx + A_WIDTH),
                           A_WIDTH + B_WIDTH + np.arange(C_WIDTH)])
    out['wo'] = _take_cols(wo, rows, axis=0).astype(BF16)
    out['nw_x'] = w['norm_x_w'][l][None, :]
    out['nw_mem'] = w['norm_mem_w'][l][None, :]
    out['xwq'] = w['xa_wq'][l].reshape(D_MODEL, D_MODEL).astype(BF16)
    out['xwk'] = w['xa_wk'][l].reshape(D_MODEL, D_MODEL).astype(BF16)
    out['xwv'] = w['xa_wv'][l].reshape(D_MODEL, D_MODEL).astype(BF16)
    out['xwo'] = w['xa_wo'][l].reshape(D_MODEL, D_MODEL).astype(BF16)
    out['nw_ffn'] = w['norm_ffn_w'][l][None, :]
    out['wg'] = w['ffn_wg'][l].astype(BF16)
    out['wu'] = w['ffn_wu'][l].astype(BF16)
    out['wd'] = w['ffn_wd'][l].astype(BF16)
    out['cw'] = w['ffn_conv_w'][l]
    out['cb'] = w['ffn_conv_b'][l][None, :]
    return out


def _chunk_mixers(ws, cb, t_rows):
    reps = CHUNK // t_rows
    wt = ws[:, :t_rows, :t_rows]
    if reps > 1:
        wt = jnp.einsum('ab,hts->hatbs', jnp.eye(reps, dtype=F32), wt).reshape(A_HEADS, CHUNK, CHUNK)
    bias = jnp.tile(jnp.repeat(jnp.transpose(cb[:, :t_rows]), A_HEAD_DIM, axis=1), (reps, 1))
    return wt, bias


def _mixer_block(x, lw, *, mlstm_state, pos_tables, chunk_w, tm, act_dtype, ns, n_tiles, n_chunks, mla_fn):
    m = x.shape[0]
    cos_t, sin_t = pos_tables
    gu, gv, q, k, v, so, qc, lat, gate = _inproj(x, lw['nw_mix'], lw['w1'], lw['qnw'], lw['kvnw'], lw['wuq'],
                                                 lw['wuk'], lw['gb'], cos_t, sin_t, tm=tm, act_dtype=act_dtype)
    ya = _chunkmlp(gu, gv, chunk_w[0], chunk_w[1], tm=tm)
    grow = jnp.transpose(gate[:, :16].reshape(m // CHUNK, CHUNK, 16), (0, 2, 1))
    c0, n0, m0 = mlstm_state
    yb, c_new, n_new, m_rows = _mlstm(q, k, v, so, gate, grow, m0, c0, n0, lw['mnw'], ns=ns, n_tiles=n_tiles,
                                      n_chunks=n_chunks, g_tiles=2 if (ns == 1 and n_tiles % 2 == 0) else 1)
    yc = mla_fn(qc, lat)
    return (ya, yb, yc), (c_new, n_new, m_rows), lat, gv


def kernel(x_prompt, x_sample, mem_prompt, cache_mla, page_table, cache_mem_k, cache_mem_v, state_mlstm_C, state_mlstm_n, state_mlstm_m, state_ffn_conv, norm_mix_w, w_in, chunk_ws, chunk_b, mlstm_bi, mlstm_bf, mlstm_norm_w, mla_qnorm_w, mla_kvnorm_w, mla_w_uq, mla_w_uk, mla_w_uv, w_out, norm_x_w, norm_mem_w, xa_wq, xa_wk, xa_wv, xa_wo, norm_ffn_w, ffn_wg, ffn_wu, ffn_conv_w, ffn_conv_b, ffn_wd, norm_final_w):
    w = dict(norm_mix_w=norm_mix_w, w_in=w_in, mlstm_bi=mlstm_bi, mlstm_bf=mlstm_bf, mlstm_norm_w=mlstm_norm_w,
             mla_qnorm_w=mla_qnorm_w, mla_kvnorm_w=mla_kvnorm_w, mla_w_uq=mla_w_uq, mla_w_uk=mla_w_uk,
             mla_w_uv=mla_w_uv, w_out=w_out, norm_x_w=norm_x_w, norm_mem_w=norm_mem_w, xa_wq=xa_wq, xa_wk=xa_wk,
             xa_wv=xa_wv, xa_wo=xa_wo, norm_ffn_w=norm_ffn_w, ffn_wg=ffn_wg, ffn_wu=ffn_wu,
             ffn_conv_w=ffn_conv_w, ffn_conv_b=ffn_conv_b, ffn_wd=ffn_wd)
    depth = w_in.shape[0]
    nbp, S, _ = x_prompt.shape
    nbs, T, _ = x_sample.shape
    n_pages = page_table.shape[1]
    past_len = n_pages * PAGE_SIZE
    mp, ms = nbp * S, nbs * T
    assert S % CHUNK == 0 and CHUNK % T == 0 and T % 8 == 0 and ms % CHUNK == 0

    tm_p = min(512, S)
    tm_s = min(512, ms)
    tq = min(256, S)
    tk = min(512, S)
    ns_s = CHUNK // T
    xg = 4 if nbs % 4 == 0 else 1
    tm_ffn_p = min(512, S)
    tm_ffn_s = min(256, ms)

    pos_p = _rope_tables(jnp.arange(S), 1)
    pos_s = _rope_tables(past_len + jnp.arange(T), tm_s // T)
    fnw = norm_final_w[None, :]

    xp = x_prompt.reshape(mp, D_MODEL)
    xs = x_sample.reshape(ms, D_MODEL)
    mem = mem_prompt.reshape(nbp * MEM_LEN, D_MODEL)

    zero_c = jnp.zeros((nbp, B_HEADS, B_HEAD_DIM, B_HEAD_DIM), F32)
    zero_n = jnp.zeros((nbp, B_HEADS, B_HEAD_DIM), F32)
    zero_m = jnp.zeros((nbp, CHUNK, LANE), F32)
    zero_conv = jnp.zeros((nbp, 1, D_FF), F32)
    cache_t = jnp.swapaxes(cache_mla, 2, 3)

    outs = {k: [] for k in ('lat_p', 'lat_s', 'mk', 'mv', 'cp', 'np', 'mp', 'cs', 'ns', 'ms', 'conv_p', 'conv_s',
                            'chunkv')}
    yp = ys = None
    for l in range(depth):
        lw = _layer_weights(l, w)
        last = l == depth - 1

        mla_p = functools.partial(_mla_prompt, wuv=lw['wuv'], nb=nbp, seq=S, tq=tq, tk=tk)
        ymix, st, lat, _ = _mixer_block(
            xp, lw, mlstm_state=(zero_c, zero_n, zero_m), pos_tables=pos_p,
            chunk_w=_chunk_mixers(chunk_ws[l], chunk_b[l], CHUNK), tm=tm_p, act_dtype=BF16, ns=1, n_tiles=nbp,
            n_chunks=S // CHUNK, mla_fn=lambda qc, lt: mla_p(qc, lt))
        xp, qx_p = _outproj(xp, *ymix, lw['wo'], lw['nw_x'], lw['xwq'], tm=tm_p, q_dtype=BF16)
        outs['lat_p'].append(lat.reshape(nbp, S, LATENT))
        outs['cp'].append(st[0])
        outs['np'].append(st[1])
        outs['mp'].append(st[2][:, CHUNK - 1, :B_HEADS])

        m0_rows = jnp.concatenate([jnp.repeat(state_mlstm_m[l], T, axis=0),
                                   jnp.zeros((ms, LANE - B_HEADS), F32)], axis=1).reshape(ms // CHUNK, CHUNK, LANE)
        mla_s = functools.partial(_mla_sample, page_table, wuv=lw['wuv'], cache_t=cache_t, layer=l, t_new=T,
                                  seqs=1)
        ymix, st, lat, gv = _mixer_block(
            xs, lw, mlstm_state=(state_mlstm_C[l], state_mlstm_n[l], m0_rows), pos_tables=pos_s,
            chunk_w=_chunk_mixers(chunk_ws[l], chunk_b[l], T), tm=tm_s, act_dtype=F32, ns=ns_s,
            n_tiles=ms // CHUNK, n_chunks=1, mla_fn=lambda qc, lt: mla_s(qc, lt))
        xs, qx_s = _outproj(xs, *ymix, lw['wo'], lw['nw_x'], lw['xwq'], tm=tm_s, q_dtype=F32)
        outs['lat_s'].append(lat.reshape(nbs, T, LATENT))
        outs['cs'].append(st[0])
        outs['ns'].append(st[1])
        outs['ms'].append(st[2].reshape(nbs, T, LANE)[:, T - 1, :B_HEADS])
        outs['chunkv'].append(gv.reshape(nbs, T, A_WIDTH))

        mk, mv, mk_b, mv_b = _memkv(mem, lw['nw_mem'], lw['xwk'], lw['xwv'], tm=min(512, nbp * MEM_LEN))
        outs['mk'].append(mk.reshape(nbp, MEM_LEN, X_HEADS, X_HEAD_DIM))
        outs['mv'].append(mv.reshape(nbp, MEM_LEN, X_HEADS, X_HEAD_DIM))
        xp = _xattn_prompt(xp, qx_p, mk_b, mv_b, lw['xwo'], tm=tm_p, tiles_per_seq=S // tm_p)
        xs = _xattn_sample(xs, qx_s, cache_mem_k, cache_mem_v, lw['xwo'], layer=l, n_seq=xg, t_rows=T)

        res_p = _ffn(xp, lw['nw_ffn'], lw['wg'], lw['wu'], lw['wd'], lw['cw'], lw['cb'], (zero_conv, zero_conv),
                     fnw, tm=tm_ffn_p, seq_rows=0, tiles_per_seq=S // tm_ffn_p, final_norm=last)
        res_s = _ffn(xs, lw['nw_ffn'], lw['wg'], lw['wu'], lw['wd'], lw['cw'], lw['cb'], state_ffn_conv, fnw,
                     tm=tm_ffn_s, seq_rows=T, tiles_per_seq=1, final_norm=last, layer=l)
        xp, xs = res_p[0], res_s[0]
        outs['conv_p'].append(res_p[1][S // tm_ffn_p - 1::S // tm_ffn_p])
        outs['conv_s'].append(res_s[1])
        if last:
            yp, ys = res_p[2], res_s[2]

    st = lambda k: jnp.stack(outs[k])
    return (yp.reshape(nbp, S, D_MODEL), ys.reshape(nbs, T, D_MODEL),
            st('lat_p'), st('lat_s'), st('mk'), st('mv'),
            st('cp'), st('np'), st('mp'), st('cs'), st('ns'), st('ms'),
            st('conv_p'), st('conv_s'), st('chunkv'))
```

```python
import functools
import math

import numpy as np
import jax
import jax.numpy as jnp
from jax import lax
from jax.experimental import pallas as pl
from jax.experimental.pallas import tpu as pltpu

F32 = jnp.float32
BF16 = jnp.bfloat16

D_MODEL = 1024
PAGE_SIZE = 128
A_HEADS = 4
A_WIDTH = 256
A_HEAD_DIM = 64
CHUNK = 128
B_HEADS = 4
B_WIDTH = 384
B_HEAD_DIM = 96
MLSTM_KSCALE = B_HEAD_DIM ** -0.5
C_WIDTH = 384
C_HEADS = 6
C_V_DIM = 64
C_NOPE = 64
C_ROPE = 32
Q_LORA = 192
KV_LORA = 128
LATENT = KV_LORA + C_ROPE
MLA_SCALE = (C_NOPE + C_ROPE) ** -0.5
ROPE_THETA = 10000.0
MEM_LEN = 256
X_HEADS = 4
X_HEAD_DIM = 256
XA_SCALE = X_HEAD_DIM ** -0.5
D_FF = 2816
CONV_W = 3
EPS = 1e-6

LANE = 128
HP = 128
BW_P = B_HEADS * HP
FF_CHUNK = 256
N_FF_CHUNKS = D_FF // FF_CHUNK
VMEM_LIMIT = 48 * 1024 * 1024

S_AU, S_AV, S_BQ, S_BK, S_BV, S_BO = 0, 256, 512, 1024, 1536, 2048
S_CQ, S_CKV, S_R1, S_R2, S_G = 2560, 2816, 2944, 3072, 3200
N_IN = 3328
CQ_P = 256
UQ_NOPE, UQ_RA, UQ_RB, N_UQ = 0, 384, 1152, 1920
QC_SLOT = 256
N_MIX_P = A_WIDTH + BW_P + C_WIDTH


def _rms(x, w, n):
    ms = jnp.sum(x * x, axis=-1, keepdims=True) * (1.0 / n)
    return x * lax.rsqrt(ms + EPS) * w


def _gelu(x):
    c = math.sqrt(2.0 / math.pi)
    return x * (0.5 * (1.0 + jnp.tanh(c * (x + 0.044715 * (x * x * x)))))


def _sigmoid(x):
    return 1.0 / (1.0 + jnp.exp(-x))


def _log_sigmoid(x):
    return jnp.minimum(x, 0.0) - jnp.log1p(jnp.exp(-jnp.abs(x)))


def _dot(a, b):
    return jnp.dot(a, b, preferred_element_type=F32)


def _dot_nt(a, b):
    return lax.dot_general(a, b, (((1,), (1,)), ((), ())), preferred_element_type=F32)


def _dot_tn(a, b):
    return lax.dot_general(a, b, (((0,), (0,)), ((), ())), preferred_element_type=F32)


def _split3(a):
    a1 = a.astype(BF16)
    r1 = a - a1.astype(F32)
    a2 = r1.astype(BF16)
    a3 = (r1 - a2.astype(F32)).astype(BF16)
    return a1, a2, a3


def _wspec(shape):
    nd = len(shape)
    return pl.BlockSpec(shape, lambda *_: (0,) * nd, pipeline_mode=pl.Buffered(1))


def _params(n_axes):
    return pltpu.CompilerParams(dimension_semantics=("arbitrary",) * n_axes,
                                vmem_limit_bytes=VMEM_LIMIT)


def _inproj_kernel(x_ref, nw_ref, w1_ref, qnw_ref, kvnw_ref, wuq_ref, wuk_ref, gb_ref, cos_ref, sin_ref,
                   gu_ref, gv_ref, q_ref, k_ref, v_ref, so_ref, qc_ref, lat_ref, gate_ref):
    xn = _rms(x_ref[...], nw_ref[...], D_MODEL).astype(BF16)

    def seg(a, b):
        return _dot(xn, w1_ref[:, a:b])

    gu_ref[...] = _gelu(seg(S_AU, S_AV))
    gv_ref[...] = _gelu(seg(S_AV, S_BQ))
    q_ref[...] = seg(S_BQ, S_BK).astype(q_ref.dtype)
    k_ref[...] = (seg(S_BK, S_BV) * MLSTM_KSCALE).astype(k_ref.dtype)
    v_ref[...] = seg(S_BV, S_BO).astype(v_ref.dtype)
    so_ref[...] = _sigmoid(seg(S_BO, S_CQ))

    c_q = _rms(seg(S_CQ, S_CKV), qnw_ref[...], Q_LORA).astype(BF16)
    q2 = _dot(c_q, wuq_ref[...])
    qlat = _dot(q2[:, UQ_NOPE:UQ_RA].astype(BF16), wuk_ref[...])
    cos = cos_ref[...]
    sin = sin_ref[...]
    for h in range(C_HEADS):
        ra = q2[:, UQ_RA + LANE * h:UQ_RA + LANE * (h + 1)]
        rb = q2[:, UQ_RB + LANE * h:UQ_RB + LANE * (h + 1)]
        qr = ra * cos + rb * sin
        qc_ref[:, QC_SLOT * h:QC_SLOT * h + LANE] = (qlat[:, LANE * h:LANE * (h + 1)] * MLA_SCALE).astype(qc_ref.dtype)
        qc_ref[:, QC_SLOT * h + LANE:QC_SLOT * (h + 1)] = (qr * MLA_SCALE).astype(qc_ref.dtype)

    lat_ref[:, 0:KV_LORA] = _rms(seg(S_CKV, S_R1), kvnw_ref[...], KV_LORA)
    kr = seg(S_R1, S_R2) * cos + seg(S_R2, S_G) * sin
    lat_ref[:, KV_LORA:LATENT] = kr[:, :C_ROPE]

    gt = seg(S_G, N_IN) + gb_ref[...]
    lane = lax.broadcasted_iota(jnp.int32, gt.shape, 1)
    gate_ref[...] = jnp.where(lane < B_HEADS, gt, jnp.where(lane < 2 * B_HEADS, _log_sigmoid(gt), 0.0))


def _inproj(x, nw, w1, qnw, kvnw, wuq, wuk, gb, cos_t, sin_t, *, tm, act_dtype):
    m = x.shape[0]
    n_pos_blocks = cos_t.shape[0] // tm
    row = lambda w: pl.BlockSpec((tm, w), lambda i: (i, 0))
    pos = pl.BlockSpec((tm, LANE), lambda i: (i % n_pos_blocks, 0))
    out_shapes = (
        jax.ShapeDtypeStruct((m, A_WIDTH), F32), jax.ShapeDtypeStruct((m, A_WIDTH), F32),
        jax.ShapeDtypeStruct((m, BW_P), act_dtype), jax.ShapeDtypeStruct((m, BW_P), act_dtype),
        jax.ShapeDtypeStruct((m, BW_P), act_dtype), jax.ShapeDtypeStruct((m, BW_P), F32),
        jax.ShapeDtypeStruct((m, C_HEADS * QC_SLOT), act_dtype), jax.ShapeDtypeStruct((m, LATENT), F32),
        jax.ShapeDtypeStruct((m, LANE), F32))
    return pl.pallas_call(
        _inproj_kernel,
        grid=(m // tm,),
        in_specs=[row(D_MODEL), _wspec((1, D_MODEL)), _wspec((D_MODEL, N_IN)), _wspec((1, CQ_P)),
                  _wspec((1, KV_LORA)), _wspec((CQ_P, N_UQ)), _wspec((C_HEADS * C_NOPE, C_HEADS * LANE)),
                  _wspec((1, LANE)), pos, pos],
        out_specs=(row(A_WIDTH), row(A_WIDTH), row(BW_P), row(BW_P), row(BW_P), row(BW_P),
                   row(C_HEADS * QC_SLOT), row(LATENT), row(LANE)),
        out_shape=out_shapes,
        compiler_params=_params(1),
        name="inproj",
    )(x, nw, w1, qnw, kvnw, wuq, wuk, gb, cos_t, sin_t)


def _chunkmlp_kernel(gu_ref, gv_ref, ws_ref, bias_ref, ya_ref, *, n_chunks):
    r = lax.broadcasted_iota(jnp.int32, (CHUNK, CHUNK), 0)
    c = lax.broadcasted_iota(jnp.int32, (CHUNK, CHUNK), 1)
    wsm = [jnp.where(r >= c, ws_ref[h], 0.0).astype(BF16) for h in range(A_HEADS)]
    head_of_lane = lax.broadcasted_iota(jnp.int32, (CHUNK, A_WIDTH), 1) >> (A_HEAD_DIM.bit_length() - 1)
    for ci in range(n_chunks):
        rows = slice(ci * CHUNK, (ci + 1) * CHUNK)
        gv = gv_ref[rows, :]
        z = bias_ref[...]
        for h in range(A_HEADS):
            z = z + _dot(wsm[h], jnp.where(head_of_lane == h, gv, 0.0).astype(BF16))
        ya_ref[rows, :] = gu_ref[rows, :] * z


def _chunkmlp(gu, gv, ws, bias, *, tm):
    m = gu.shape[0]
    row = pl.BlockSpec((tm, A_WIDTH), lambda i: (i, 0))
    return pl.pallas_call(
        functools.partial(_chunkmlp_kernel, n_chunks=tm // CHUNK),
        grid=(m // tm,),
        in_specs=[row, row, _wspec((A_HEADS, CHUNK, CHUNK)), _wspec((CHUNK, A_WIDTH))],
        out_specs=row,
        out_shape=jax.ShapeDtypeStruct((m, A_WIDTH), F32),
        compiler_params=_params(1),
        name="chunkmlp",
    )(gu, gv, ws, bias)


_COL_M, _COL_G, _COL_RS = 0, 8, 16


def _mlstm_kernel(q_ref, k_ref, v_ref, so_ref, gcol_ref, grow_ref, m0_ref, c0_ref, n0_ref, nw_ref,
                  yb_ref, cout_ref, nout_ref, mout_ref,
                  cs_ref, ns_ref, mp_ref, col_ref, sv_ref, qc_ref, nr_ref, *, ns, ls, g_tiles):
    @pl.when(pl.program_id(1) == 0)
    def _init():
        cs_ref[...] = jnp.zeros(cs_ref.shape, F32)
        ns_ref[...] = jnp.zeros(ns_ref.shape, F32)
        for s in range(g_tiles):
            for g in range(ns):
                for h in range(B_HEADS):
                    cs_ref[s, g * B_HEADS + h, :B_HEAD_DIM, :B_HEAD_DIM] = c0_ref[s, g, h]
                    ns_ref[s, g * B_HEADS + h, 0:1, :B_HEAD_DIM] = n0_ref[s, g, h:h + 1, :]
        if ns == 1:
            for s in range(g_tiles):
                for h in range(B_HEADS):
                    mp_ref[s, h:h + 1, :] = jnp.broadcast_to(m0_ref[s, 0:1, h:h + 1], (1, LANE))
        else:
            mp_ref[...] = m0_ref[...]

    for s in range(g_tiles):
        at = lambda r: r.at[s]
        if ns == 1:
            _mlstm_chunk(at(q_ref), at(k_ref), at(v_ref), at(so_ref), at(gcol_ref), at(grow_ref), nw_ref,
                         at(yb_ref), at(cout_ref), at(nout_ref), at(mout_ref), at(cs_ref), at(ns_ref), at(mp_ref))
        else:
            _mlstm_tile(at(q_ref), at(k_ref), at(v_ref), at(so_ref), at(gcol_ref), at(grow_ref), nw_ref,
                        at(yb_ref), at(cout_ref), at(nout_ref), at(mout_ref),
                        at(cs_ref), at(ns_ref), at(mp_ref), at(col_ref), at(sv_ref), at(qc_ref), at(nr_ref),
                        ns=ns, ls=ls)


def _mlstm_chunk(q_ref, k_ref, v_ref, so_ref, gcol_ref, grow_ref, nw_ref,
                 yb_ref, cout_ref, nout_ref, mout_ref, cs_ref, ns_ref, mp_ref):
    L = CHUNK
    row = lax.broadcasted_iota(jnp.int32, (L, L), 0)
    col = lax.broadcasted_iota(jnp.int32, (L, L), 1)
    causal = col <= row
    cum = jnp.where(causal, 1.0, 0.0).astype(BF16)
    gcol = gcol_ref[...]
    grow = grow_ref[...]
    brow = sum(_dot_nt(t, cum) for t in _split3(grow))
    ones = jnp.ones((L, HP), BF16)
    mout_ref[...] = jnp.zeros(mout_ref.shape, F32)
    for h in range(B_HEADS):
        hs = slice(h * HP, (h + 1) * HP)
        qh = q_ref[:, hs].astype(BF16)
        kh = k_ref[:, hs].astype(BF16)
        vh = v_ref[:, hs].astype(BF16)
        ig = jnp.broadcast_to(gcol[:, h:h + 1], (L, HP))
        lf = jnp.broadcast_to(gcol[:, B_HEADS + h:B_HEADS + h + 1], (L, HP))
        bc = sum(_dot(cum, t) for t in _split3(lf))
        bm = bc + mp_ref[h:h + 1, :]
        logd = jnp.where(causal, grow[h:h + 1, :] + bc - brow[B_HEADS + h:B_HEADS + h + 1, :], -jnp.inf)
        m_t = jnp.maximum(bm, jnp.max(logd, axis=-1, keepdims=True))
        s = (_dot_nt(qh, kh) * jnp.exp(logd - m_t)).astype(BF16)
        sv = _dot(s, jnp.concatenate([vh, ones], axis=1))
        gdec = jnp.exp(bm - m_t)

        c_old = cs_ref[h]
        n_old = ns_ref[h, 0:1, :]
        num = gdec * _dot_nt(qh, c_old.astype(BF16)) + sv[:, 0:HP]
        den = gdec * jnp.sum(qh.astype(F32) * n_old, axis=-1, keepdims=True) + sv[:, HP:2 * HP]
        hh = num / jnp.maximum(jnp.abs(den), jnp.exp(-m_t))
        yb_ref[:, hs] = so_ref[:, hs] * _rms(hh, nw_ref[:, hs], B_HEAD_DIM)

        m_new, b_last, decay = m_t[L - 1:L, :], bc[L - 1:L, :], gdec[L - 1:L, :]
        w = jnp.exp(ig + b_last - bc - m_new)
        c_new = decay * c_old + _dot_tn((vh.astype(F32) * w).astype(BF16), kh)
        n_new = decay * n_old + jnp.sum(kh.astype(F32) * w, axis=0, keepdims=True)
        cs_ref[h] = c_new
        ns_ref[h, 0:1, :] = n_new
        mp_ref[h:h + 1, :] = m_new
        mout_ref[L - 1:L, h:h + 1] = m_new[:, 0:1]
        cout_ref[0, h] = c_new[:B_HEAD_DIM, :B_HEAD_DIM]
        nout_ref[0, h:h + 1, :] = n_new[:, :B_HEAD_DIM]


def _mlstm_tile(q_ref, k_ref, v_ref, so_ref, gcol_ref, grow_ref, nw_ref,
                yb_ref, cout_ref, nout_ref, mout_ref,
                cs_ref, ns_ref, mp_ref, col_ref, sv_ref, qc_ref, nr_ref, *, ns, ls):
    L = CHUNK
    ls_shift = ls.bit_length() - 1

    col_ref[...] = jnp.zeros(col_ref.shape, F32)
    row = lax.broadcasted_iota(jnp.int32, (L, L), 0)
    col = lax.broadcasted_iota(jnp.int32, (L, L), 1)
    causal = (col <= row) & ((col >> ls_shift) == (row >> ls_shift))
    cum = jnp.where(causal, 1.0, 0.0).astype(BF16)
    rid = lax.broadcasted_iota(jnp.int32, (L, 1), 0)

    gcol = gcol_ref[...]
    grow = grow_ref[...]
    bcol = sum(_dot(cum, t) for t in _split3(gcol))
    brow = sum(_dot_nt(t, cum) for t in _split3(grow))
    mp = mp_ref[...]

    qs, ks, vs = [], [], []
    for h in range(B_HEADS):
        hs = slice(h * HP, (h + 1) * HP)
        qh = q_ref[:, hs].astype(BF16)
        kh = k_ref[:, hs].astype(BF16)
        vh = v_ref[:, hs].astype(BF16)
        qs.append(qh), ks.append(kh), vs.append(vh)
        ig_r = grow[h:h + 1, :]
        b_r = brow[B_HEADS + h:B_HEADS + h + 1, :]
        b_c = bcol[:, B_HEADS + h:B_HEADS + h + 1]
        bm = b_c + mp[:, h:h + 1]
        logd = jnp.where(causal, ig_r + b_c - b_r, -jnp.inf)
        m_t = jnp.maximum(bm, jnp.max(logd, axis=-1, keepdims=True))
        d = jnp.exp(logd - m_t)
        s = _dot_nt(qh, kh) * d
        sv_ref[h] = _dot(s.astype(BF16), vh)
        col_ref[:, _COL_M + h:_COL_M + h + 1] = m_t
        col_ref[:, _COL_G + h:_COL_G + h + 1] = jnp.exp(bm - m_t)
        col_ref[:, _COL_RS + h:_COL_RS + h + 1] = jnp.sum(s, axis=-1, keepdims=True)

    qc_ref[...] = jnp.zeros(qc_ref.shape, F32)
    nr_ref[...] = jnp.zeros(nr_ref.shape, F32)

    def seq_update(g):
        last = g * ls + (ls - 1)
        in_seq = (rid >> ls_shift) == g
        is_last = rid == last
        for h in range(B_HEADS):
            c_old = cs_ref[g * B_HEADS + h]
            n_old = ns_ref[g * B_HEADS + h, 0:1, :]
            qm = jnp.where(in_seq, qs[h], jnp.zeros_like(qs[h])) if ns > 1 else qs[h]
            qc_ref[h] += _dot_nt(qm, c_old.astype(BF16))
            nr_ref[h] += jnp.where(in_seq, n_old, 0.0)
            m_t = col_ref[:, _COL_M + h:_COL_M + h + 1]
            gdec = col_ref[:, _COL_G + h:_COL_G + h + 1]
            b_c = bcol[:, B_HEADS + h:B_HEADS + h + 1]
            ig_c = gcol[:, h:h + 1]
            pick = lambda a: jnp.sum(jnp.where(is_last, a, 0.0), axis=0, keepdims=True)
            m_new, b_last, decay = pick(m_t), pick(b_c), pick(gdec)
            w_c = jnp.where(in_seq, jnp.exp(ig_c + b_last - b_c - m_new), 0.0)
            vw = (vs[h].astype(F32) * w_c).astype(BF16)
            cs_ref[g * B_HEADS + h] = decay * c_old + _dot_tn(vw, ks[h])
            ns_ref[g * B_HEADS + h, 0:1, :] = (decay * n_old
                                               + jnp.sum(ks[h].astype(F32) * w_c, axis=0, keepdims=True))
            if ns == 1:
                mp_ref[:, h:h + 1] = jnp.broadcast_to(m_new, (L, 1))

    if ns == 1:
        seq_update(0)
    else:
        lax.fori_loop(0, ns, lambda g, c: (seq_update(g), c)[1], 0)

    for h in range(B_HEADS):
        hs = slice(h * HP, (h + 1) * HP)
        m_t = col_ref[:, _COL_M + h:_COL_M + h + 1]
        gdec = col_ref[:, _COL_G + h:_COL_G + h + 1]
        rs = col_ref[:, _COL_RS + h:_COL_RS + h + 1]
        num = gdec * qc_ref[h] + sv_ref[h]
        qn = jnp.sum(qs[h].astype(F32) * nr_ref[h], axis=-1, keepdims=True)
        den = gdec * qn + rs
        hh = num / jnp.maximum(jnp.abs(den), jnp.exp(-m_t))
        yb_ref[:, hs] = so_ref[:, hs] * _rms(hh, nw_ref[:, hs], B_HEAD_DIM)

    mout_ref[...] = col_ref[...]
    for g in range(ns):
        for h in range(B_HEADS):
            cout_ref[g, h] = cs_ref[g * B_HEADS + h, :B_HEAD_DIM, :B_HEAD_DIM]
            nout_ref[g, h:h + 1, :] = ns_ref[g * B_HEADS + h, 0:1, :B_HEAD_DIM]


def _mlstm(q, k, v, so, gcol, grow, m0, c0, n0, nw, *, ns, n_tiles, n_chunks, g_tiles):
    m = q.shape[0]
    nb = c0.shape[0]
    ls = CHUNK // ns
    g = g_tiles
    t3 = lambda a: a.reshape(n_tiles, n_chunks * CHUNK, a.shape[-1])
    rows = lambda w: pl.BlockSpec((g, CHUNK, w), lambda t, c: (t, c, 0))
    cspec = pl.BlockSpec((g, ns, B_HEADS, B_HEAD_DIM, B_HEAD_DIM), lambda t, c: (t, 0, 0, 0, 0))
    nspec = pl.BlockSpec((g, ns, B_HEADS, B_HEAD_DIM), lambda t, c: (t, 0, 0, 0))
    mspec = pl.BlockSpec((g, CHUNK, LANE), lambda t, c: (t, 0, 0))
    yb, c_new, n_new, m_rows = pl.pallas_call(
        functools.partial(_mlstm_kernel, ns=ns, ls=ls, g_tiles=g),
        grid=(n_tiles // g, n_chunks),
        in_specs=[rows(BW_P), rows(BW_P), rows(BW_P), rows(BW_P), rows(LANE),
                  pl.BlockSpec((g, None, 16, CHUNK), lambda t, c: (t, c, 0, 0)),
                  mspec, cspec, nspec, _wspec((1, BW_P))],
        out_specs=(rows(BW_P), cspec, nspec, mspec),
        out_shape=(jax.ShapeDtypeStruct((n_tiles, n_chunks * CHUNK, BW_P), F32),
                   jax.ShapeDtypeStruct((n_tiles, ns, B_HEADS, B_HEAD_DIM, B_HEAD_DIM), F32),
                   jax.ShapeDtypeStruct((n_tiles, ns, B_HEADS, B_HEAD_DIM), F32),
                   jax.ShapeDtypeStruct((n_tiles, CHUNK, LANE), F32)),
        scratch_shapes=[pltpu.VMEM((g, ns * B_HEADS, HP, HP), F32), pltpu.VMEM((g, ns * B_HEADS, 8, HP), F32),
                        pltpu.VMEM((g, CHUNK, LANE), F32), pltpu.VMEM((g, CHUNK, LANE), F32),
                        pltpu.VMEM((g, B_HEADS, CHUNK, HP), F32), pltpu.VMEM((g, B_HEADS, CHUNK, HP), F32),
                        pltpu.VMEM((g, B_HEADS, CHUNK, HP), F32)],
        compiler_params=_params(2),
        name="mlstm",
    )(t3(q), t3(k), t3(v), t3(so), t3(gcol), grow.reshape(n_tiles, n_chunks, 16, CHUNK), m0,
      c0.reshape(n_tiles, ns, B_HEADS, B_HEAD_DIM, B_HEAD_DIM), n0.reshape(n_tiles, ns, B_HEADS, B_HEAD_DIM), nw)
    return (yb.reshape(m, BW_P), c_new.reshape(nb, B_HEADS, B_HEAD_DIM, B_HEAD_DIM),
            n_new.reshape(nb, B_HEADS, B_HEAD_DIM), m_rows)


def _stack_heads(qc_ref, dtype):
    return jnp.concatenate([qc_ref[:, QC_SLOT * h:QC_SLOT * h + LATENT] for h in range(C_HEADS)],
                           axis=0).astype(dtype)


def _unstack_project(o, t, wuv_ref):
    wide = jnp.concatenate([o[h * t:(h + 1) * t] for h in range(C_HEADS)], axis=1)
    return _dot(wide.astype(BF16), wuv_ref[...])


def _softmax_step(s, pv, m_ref, acc_ref):
    reps = s.shape[1] // LANE
    m_prev = m_ref[...]
    m_new = jnp.maximum(m_prev, jnp.max(s, axis=-1, keepdims=True))
    p = jnp.exp(s - jnp.concatenate([m_new] * reps, axis=1))
    corr = jnp.exp(m_prev - m_new)
    acc_ref[...] = jnp.concatenate([corr, corr], axis=1) * acc_ref[...] + pv(p.astype(BF16))
    m_ref[...] = m_new


def _softmax_result(acc_ref):
    acc = acc_ref[...]
    return acc[:, :KV_LORA] / acc[:, KV_LORA:]


def _mla_prompt_kernel(qc_ref, lat_ref, wuv_ref, yc_ref, kb_ref, vb_ref, m_ref, acc_ref, *, tq, tk, row_groups):
    qi = pl.program_id(1)

    @pl.when(qi == 0)
    def _cast_keys():
        lat = lat_ref[...]
        kb_ref[...] = lat.astype(BF16)
        vb_ref[:, 0:KV_LORA] = lat[:, 0:KV_LORA].astype(BF16)
        vb_ref[:, KV_LORA:2 * KV_LORA] = jnp.ones((lat.shape[0], KV_LORA), BF16)

    qst = _stack_heads(qc_ref, BF16)
    m_ref[...] = jnp.full(m_ref.shape, -jnp.inf, F32)
    acc_ref[...] = jnp.zeros(acc_ref.shape, F32)
    rows = C_HEADS * tq

    gr = rows // row_groups

    def block(start, width, masked):
        kj = kb_ref[pl.ds(start, width), :]
        vj = vb_ref[pl.ds(start, width), :]
        for g in range(row_groups):
            rs = slice(g * gr, (g + 1) * gr)
            s = _dot_nt(qst[rs], kj)
            if masked:
                r = g * gr + lax.broadcasted_iota(jnp.int32, (gr, width), 0)
                qpos = qi * tq + (r & (tq - 1))
                kpos = start + lax.broadcasted_iota(jnp.int32, (gr, width), 1)
                s = jnp.where(kpos <= qpos, s, -jnp.inf)
            _softmax_step(s, lambda p: _dot(p, vj), m_ref.at[rs], acc_ref.at[rs])

    n_full = (qi * tq) // tk
    lax.fori_loop(0, n_full, lambda j, c: (block(pl.multiple_of(j * tk, tk), tk, False), c)[1], 0)
    block(pl.multiple_of(n_full * tk, tk), tk, True)
    yc_ref[...] = _unstack_project(_softmax_result(acc_ref), tq, wuv_ref)


def _mla_prompt(qc, lat, wuv, *, nb, seq, tq, tk):
    m = qc.shape[0]
    n_q = seq // tq
    rows = C_HEADS * tq
    assert tk in (tq, 2 * tq)
    return pl.pallas_call(
        functools.partial(_mla_prompt_kernel, tq=tq, tk=tk, row_groups=6),
        grid=(nb, n_q),
        in_specs=[pl.BlockSpec((tq, C_HEADS * QC_SLOT), lambda b, i: (b * n_q + i, 0)),
                  pl.BlockSpec((seq, LATENT), lambda b, i: (b, 0)),
                  _wspec((C_HEADS * KV_LORA, C_WIDTH))],
        out_specs=pl.BlockSpec((tq, C_WIDTH), lambda b, i: (b * n_q + i, 0)),
        out_shape=jax.ShapeDtypeStruct((m, C_WIDTH), F32),
        scratch_shapes=[pltpu.VMEM((seq, LATENT), BF16), pltpu.VMEM((seq, 2 * KV_LORA), BF16),
                        pltpu.VMEM((rows, LANE), F32), pltpu.VMEM((rows, 2 * KV_LORA), F32)],
        compiler_params=_params(2),
        name="mla_prompt",
    )(qc, lat, wuv)


def _mla_sample_kernel(pt_ref, qc_ref, latn_ref, wuv_ref, cache_ref, yc_ref, buf_ref, sem_ref, kt_ref, kn_ref,
                       *, layer, n_pages, t_new, seqs):
    b = pl.program_id(0)
    nb = pl.num_programs(0)
    slot = b & 1
    rows = C_HEADS * t_new

    def page_copy(step, r, page, to_slot):
        return pltpu.make_async_copy(cache_ref.at[layer, pt_ref[step * seqs + r, page]],
                                     buf_ref.at[to_slot, r, page], sem_ref.at[to_slot])

    def request(step, to_slot):
        for r in range(seqs):
            for page in range(n_pages):
                page_copy(step, r, page, to_slot).start()

    @pl.when(b == 0)
    def _first():
        request(0, 0)

    @pl.when(b + 1 < nb)
    def _prefetch_next():
        request(b + 1, 1 - slot)

    for r in range(seqs):
        for page in range(n_pages):
            page_copy(b, r, page, slot).wait()

    tpos = lax.broadcasted_iota(jnp.int32, (rows, PAGE_SIZE), 0) & (t_new - 1)
    kidx = lax.broadcasted_iota(jnp.int32, (rows, PAGE_SIZE), 1)
    for r in range(seqs):
        rs = slice(r * t_new, (r + 1) * t_new)
        qst = jnp.concatenate([qc_ref[rs, QC_SLOT * h:QC_SLOT * h + LATENT] for h in range(C_HEADS)],
                              axis=0).astype(BF16)
        for page in range(n_pages):
            kt_ref[r, :, page * PAGE_SIZE:(page + 1) * PAGE_SIZE] = buf_ref[slot, r, page].astype(BF16)
        s_past = _dot(qst, kt_ref[r])

        kn_ref[r] = jnp.zeros(kn_ref.shape[1:], F32)
        kn_ref[r, 0:t_new, :] = latn_ref[rs, :]
        kn = kn_ref[r].astype(BF16)
        s_new = jnp.where(kidx <= tpos, _dot_nt(qst, kn), -jnp.inf)

        m = jnp.maximum(jnp.max(s_past, axis=-1, keepdims=True), jnp.max(s_new, axis=-1, keepdims=True))
        p_past = jnp.exp(s_past - m).astype(BF16)
        p_new = jnp.exp(s_new - m).astype(BF16)
        denom = (jnp.sum(p_past.astype(F32), axis=-1, keepdims=True)
                 + jnp.sum(p_new.astype(F32), axis=-1, keepdims=True))
        o = _dot_nt(p_past, kt_ref[r, 0:KV_LORA, :]) + _dot(p_new, kn[:, 0:KV_LORA])
        yc_ref[rs, :] = _unstack_project(o / denom, t_new, wuv_ref)


def _mla_sample(page_table, qc, lat_new, wuv, cache_t, *, layer, t_new, seqs):
    m = qc.shape[0]
    nb, n_pages = page_table.shape
    tr = seqs * t_new
    grid_spec = pltpu.PrefetchScalarGridSpec(
        num_scalar_prefetch=1,
        grid=(nb // seqs,),
        in_specs=[pl.BlockSpec((tr, C_HEADS * QC_SLOT), lambda b, pt: (b, 0)),
                  pl.BlockSpec((tr, LATENT), lambda b, pt: (b, 0)),
                  pl.BlockSpec((C_HEADS * KV_LORA, C_WIDTH), lambda b, pt: (0, 0)),
                  pl.BlockSpec(memory_space=pl.ANY)],
        out_specs=pl.BlockSpec((tr, C_WIDTH), lambda b, pt: (b, 0)),
        scratch_shapes=[pltpu.VMEM((2, seqs, n_pages, LATENT, PAGE_SIZE), F32), pltpu.SemaphoreType.DMA((2,)),
                        pltpu.VMEM((seqs, LATENT, n_pages * PAGE_SIZE), BF16),
                        pltpu.VMEM((seqs, PAGE_SIZE, LATENT), F32)])
    return pl.pallas_call(
        functools.partial(_mla_sample_kernel, layer=layer, n_pages=n_pages, t_new=t_new, seqs=seqs),
        grid_spec=grid_spec,
        out_shape=jax.ShapeDtypeStruct((m, C_WIDTH), F32),
        compiler_params=_params(1),
        name="mla_sample",
    )(page_table, qc, lat_new, wuv, cache_t)


def _outproj_kernel(x_ref, ya_ref, yb_ref, yc_ref, wo_ref, nxw_ref, wq_ref, xo_ref, q_ref):
    y = (_dot(ya_ref[...].astype(BF16), wo_ref[0:A_WIDTH, :])
         + _dot(yb_ref[...].astype(BF16), wo_ref[A_WIDTH:A_WIDTH + BW_P, :])
         + _dot(yc_ref[...].astype(BF16), wo_ref[A_WIDTH + BW_P:N_MIX_P, :]))
    xn = x_ref[...] + y
    xo_ref[...] = xn
    hq = _rms(xn, nxw_ref[...], D_MODEL).astype(BF16)
    q_ref[...] = (_dot(hq, wq_ref[...]) * XA_SCALE).astype(q_ref.dtype)


def _outproj(x, ya, yb, yc, wo, nxw, wq, *, tm, q_dtype):
    m = x.shape[0]
    row = lambda w: pl.BlockSpec((tm, w), lambda i: (i, 0))
    return pl.pallas_call(
        _outproj_kernel,
        grid=(m // tm,),
        in_specs=[row(D_MODEL), row(A_WIDTH), row(BW_P), row(C_WIDTH), _wspec((N_MIX_P, D_MODEL)),
                  _wspec((1, D_MODEL)), _wspec((D_MODEL, D_MODEL))],
        out_specs=(row(D_MODEL), row(D_MODEL)),
        out_shape=(jax.ShapeDtypeStruct((m, D_MODEL), F32), jax.ShapeDtypeStruct((m, D_MODEL), q_dtype)),
        compiler_params=_params(1),
        name="outproj",
    )(x, ya, yb, yc, wo, nxw, wq)


def _memkv_kernel(mem_ref, nw_ref, wk_ref, wv_ref, k_ref, v_ref, kb_ref, vb_ref):
    mn = _rms(mem_ref[...], nw_ref[...], D_MODEL).astype(BF16)
    k = _dot(mn, wk_ref[...])
    v = _dot(mn, wv_ref[...])
    kb_ref[...] = k.astype(BF16)
    vb_ref[...] = v.astype(BF16)
    for h in range(X_HEADS):
        hs = slice(h * X_HEAD_DIM, (h + 1) * X_HEAD_DIM)
        k_ref[:, h, :] = k[:, hs]
        v_ref[:, h, :] = v[:, hs]


def _memkv(mem, nw, wk, wv, *, tm):
    m = mem.shape[0]
    row = pl.BlockSpec((tm, D_MODEL), lambda i: (i, 0))
    out = pl.BlockSpec((tm, X_HEADS, X_HEAD_DIM), lambda i: (i, 0, 0))
    return pl.pallas_call(
        _memkv_kernel,
        grid=(m // tm,),
        in_specs=[row, _wspec((1, D_MODEL)), _wspec((D_MODEL, D_MODEL)), _wspec((D_MODEL, D_MODEL))],
        out_specs=(out, out, row, row),
        out_shape=(jax.ShapeDtypeStruct((m, X_HEADS, X_HEAD_DIM), F32),) * 2
                  + (jax.ShapeDtypeStruct((m, D_MODEL), BF16),) * 2,
        compiler_params=_params(1),
        name="memkv",
    )(mem, nw, wk, wv)


def _softmax_rows(s):
    p = jnp.exp(s - jnp.max(s, axis=-1, keepdims=True))
    return p / jnp.sum(p, axis=-1, keepdims=True)


def _xattn_prompt_kernel(x_ref, q_ref, k_ref, v_ref, wo_ref, xo_ref, o_ref):
    for h in range(X_HEADS):
        hs = slice(h * X_HEAD_DIM, (h + 1) * X_HEAD_DIM)
        p = _softmax_rows(_dot_nt(q_ref[:, hs], k_ref[:, hs]))
        o_ref[:, hs] = _dot(p.astype(BF16), v_ref[:, hs])
    xo_ref[...] = x_ref[...] + _dot(o_ref[...].astype(BF16), wo_ref[...])


def _xattn_prompt(x, q, k, v, wo, *, tm, tiles_per_seq):
    m = x.shape[0]
    row = pl.BlockSpec((tm, D_MODEL), lambda i: (i, 0))
    kv = pl.BlockSpec((MEM_LEN, D_MODEL), lambda i: (i // tiles_per_seq, 0))
    return pl.pallas_call(
        _xattn_prompt_kernel,
        grid=(m // tm,),
        in_specs=[row, row, kv, kv, _wspec((D_MODEL, D_MODEL))],
        out_specs=row,
        out_shape=jax.ShapeDtypeStruct((m, D_MODEL), F32),
        scratch_shapes=[pltpu.VMEM((tm, D_MODEL), F32)],
        compiler_params=_params(1),
        name="xattn_prompt",
    )(x, q, k, v, wo)


def _xattn_sample_kernel(x_ref, q_ref, k_ref, v_ref, wo_ref, xo_ref, o_ref, *, n_seq, t_rows):
    rows = X_HEADS * t_rows
    nk = MEM_LEN * X_HEADS
    r = lax.broadcasted_iota(jnp.int32, (rows, nk), 0)
    c = lax.broadcasted_iota(jnp.int32, (rows, nk), 1)
    keep = (c & (X_HEADS - 1)) == (r >> (t_rows.bit_length() - 1))
    for g in range(n_seq):
        rs = slice(g * t_rows, (g + 1) * t_rows)
        qst = jnp.concatenate([q_ref[rs, h * X_HEAD_DIM:(h + 1) * X_HEAD_DIM] for h in range(X_HEADS)], axis=0)
        k2 = k_ref[g].reshape(nk, X_HEAD_DIM).astype(BF16)
        v2 = v_ref[g].reshape(nk, X_HEAD_DIM).astype(BF16)
        p = _softmax_rows(jnp.where(keep, _dot_nt(qst.astype(BF16), k2), -jnp.inf))
        o = _dot(p.astype(BF16), v2)
        for h in range(X_HEADS):
            o_ref[rs, h * X_HEAD_DIM:(h + 1) * X_HEAD_DIM] = o[h * t_rows:(h + 1) * t_rows]
    xo_ref[...] = x_ref[...] + _dot(o_ref[...].astype(BF16), wo_ref[...])


def _xattn_sample(x, q, k, v, wo, *, layer, n_seq, t_rows):
    m = x.shape[0]
    r = n_seq * t_rows
    row = pl.BlockSpec((r, D_MODEL), lambda i: (i, 0))
    kv = pl.BlockSpec((None, n_seq, MEM_LEN, X_HEADS, X_HEAD_DIM), lambda i: (layer, i, 0, 0, 0))
    return pl.pallas_call(
        functools.partial(_xattn_sample_kernel, n_seq=n_seq, t_rows=t_rows),
        grid=(m // r,),
        in_specs=[row, row, kv, kv, _wspec((D_MODEL, D_MODEL))],
        out_specs=row,
        out_shape=jax.ShapeDtypeStruct((m, D_MODEL), F32),
        scratch_shapes=[pltpu.VMEM((r, D_MODEL), F32)],
        compiler_params=_params(1),
        name="xattn_sample",
    )(x, q, k, v, wo)


def _ffn_kernel(x_ref, nw_ref, wg_ref, wu_ref, wd_ref, cw_ref, cb_ref, *rest,
                seq_rows, tiles_per_seq, final_norm):
    if seq_rows:
        p_ref, fnw_ref, *rest = rest
    else:
        p0_ref, p1_ref, fnw_ref, *rest = rest
    if final_norm:
        xo_ref, gtail_ref, y_ref, carry_ref, act_ref = rest
    else:
        xo_ref, gtail_ref, carry_ref, act_ref = rest
    tm = x_ref.shape[0]
    x = x_ref[...]
    xn = _rms(x, nw_ref[...], D_MODEL).astype(BF16)
    rid = lax.broadcasted_iota(jnp.int32, (tm, 1), 0)
    if seq_rows:
        pos = rid & (seq_rows - 1)
    else:
        pos = rid

        @pl.when(pl.program_id(0) % tiles_per_seq == 0)
        def _seq_start():
            carry_ref[0:1, :] = p0_ref[...]
            carry_ref[1:2, :] = p1_ref[...]

    for c in range(N_FF_CHUNKS):
        cs = slice(c * FF_CHUNK, (c + 1) * FF_CHUNK)
        g = _dot(xn, wg_ref[:, cs])
        u = _dot(xn, wu_ref[:, cs])
        if seq_rows:
            n_seq = tm // seq_rows
            to_rows = lambda a: jnp.broadcast_to(a, (n_seq, seq_rows, FF_CHUNK)).reshape(tm, FF_CHUNK)
            prev0, prev1 = to_rows(p_ref[:, 0:1, cs]), to_rows(p_ref[:, 1:2, cs])
            gtail_ref[:, :, cs] = g.reshape(n_seq, seq_rows, FF_CHUNK)[:, seq_rows - 2:seq_rows, :]
        else:
            prev0, prev1 = carry_ref[0:1, cs], carry_ref[1:2, cs]
            gtail_ref[:, cs] = g[tm - 2:tm, :]
            carry_ref[0:2, cs] = g[tm - 2:tm, :]
        g1 = jnp.where(pos == 0, prev1, pltpu.roll(g, 1, 0))
        g2 = jnp.where(pos == 0, prev0, jnp.where(pos == 1, prev1, pltpu.roll(g, 2, 0)))
        y = cb_ref[:, cs] + g2 * cw_ref[0:1, cs] + g1 * cw_ref[1:2, cs] + g * cw_ref[2:3, cs]
        act_ref[:, cs] = (_gelu(y) * u).astype(BF16)
    xo = x + _dot(act_ref[...], wd_ref[...])
    xo_ref[...] = xo
    if final_norm:
        y_ref[...] = _rms(xo, fnw_ref[...], D_MODEL)


def _ffn(x, nw, wg, wu, wd, cw, cb, state, fnw, *, tm, seq_rows, tiles_per_seq, final_norm, layer=0):
    m = x.shape[0]
    n_tiles = m // tm
    row = pl.BlockSpec((tm, D_MODEL), lambda i: (i, 0))
    if seq_rows:
        n_seq = tm // seq_rows
        pspecs = [pl.BlockSpec((None, n_seq, CONV_W - 1, D_FF), lambda i: (layer, i, 0, 0))]
        gspec = pl.BlockSpec((n_seq, CONV_W - 1, D_FF), lambda i: (i, 0, 0))
        gshape = jax.ShapeDtypeStruct((m // seq_rows, CONV_W - 1, D_FF), F32)
        states = (state,)
    else:
        pspecs = [pl.BlockSpec((None, 1, D_FF), lambda i: (i // tiles_per_seq, 0, 0))] * 2
        gspec = pl.BlockSpec((None, 2, D_FF), lambda i: (i, 0, 0))
        gshape = jax.ShapeDtypeStruct((n_tiles, 2, D_FF), F32)
        states = tuple(state)
    return pl.pallas_call(
        functools.partial(_ffn_kernel, seq_rows=seq_rows, tiles_per_seq=tiles_per_seq, final_norm=final_norm),
        grid=(n_tiles,),
        in_specs=[row, _wspec((1, D_MODEL)), _wspec((D_MODEL, D_FF)), _wspec((D_MODEL, D_FF)),
                  _wspec((D_FF, D_MODEL)), _wspec((CONV_W, D_FF)), _wspec((1, D_FF))]
                 + pspecs + [_wspec((1, D_MODEL))],
        out_specs=(row, gspec) + ((row,) if final_norm else ()),
        out_shape=(jax.ShapeDtypeStruct((m, D_MODEL), F32), gshape)
                  + ((jax.ShapeDtypeStruct((m, D_MODEL), F32),) if final_norm else ()),
        scratch_shapes=[pltpu.VMEM((8, D_FF), F32), pltpu.VMEM((tm, D_FF), BF16)],
        compiler_params=_params(1),
        name="ffn",
    )(x, nw, wg, wu, wd, cw, cb, *states, fnw)


def _pad_heads_idx(start):
    idx = np.full((BW_P,), -1, np.int64)
    for h in range(B_HEADS):
        idx[h * HP:h * HP + B_HEAD_DIM] = start + h * B_HEAD_DIM + np.arange(B_HEAD_DIM)
    return idx


def _take_cols(w, idx, axis=-1):
    axis = axis % w.ndim
    pieces, i, n = [], 0, len(idx)
    while i < n:
        j = i + 1
        if idx[i] < 0:
            while j < n and idx[j] < 0:
                j += 1
            shape = w.shape[:axis] + (j - i,) + w.shape[axis + 1:]
            pieces.append(jnp.zeros(shape, w.dtype))
        else:
            while j < n and idx[j] == idx[j - 1] + 1:
                j += 1
            pieces.append(lax.slice_in_dim(w, int(idx[i]), int(idx[i]) + (j - i), axis=axis))
        i = j
    return jnp.concatenate(pieces, axis=axis)


def _win_layout():
    half = C_ROPE // 2
    base = {'au': 0, 'av': 256, 'bq': 512, 'bk': 896, 'bv': 1280, 'bo': 1664, 'bi': 2048, 'bf': 2052,
            'cq': 2056, 'ckv': 2248, 'ckr': 2376}
    idx = np.full((N_IN,), -1, np.int64)
    idx[S_AU:S_AU + 256] = base['au'] + np.arange(256)
    idx[S_AV:S_AV + 256] = base['av'] + np.arange(256)
    for s, name in ((S_BQ, 'bq'), (S_BK, 'bk'), (S_BV, 'bv'), (S_BO, 'bo')):
        idx[s:s + BW_P] = _pad_heads_idx(base[name])
    idx[S_CQ:S_CQ + Q_LORA] = base['cq'] + np.arange(Q_LORA)
    idx[S_CKV:S_CKV + KV_LORA] = base['ckv'] + np.arange(KV_LORA)
    idx[S_R1:S_R1 + C_ROPE] = base['ckr'] + np.arange(C_ROPE)
    idx[S_R2:S_R2 + half] = base['ckr'] + half + np.arange(half)
    idx[S_R2 + half:S_R2 + C_ROPE] = base['ckr'] + np.arange(half)
    idx[S_G:S_G + B_HEADS] = base['bi'] + np.arange(B_HEADS)
    idx[S_G + B_HEADS:S_G + 2 * B_HEADS] = base['bf'] + np.arange(B_HEADS)
    return idx


def _wuq_layout():
    half = C_ROPE // 2
    per = C_NOPE + C_ROPE
    idx = np.full((N_UQ,), -1, np.int64)
    for h in range(C_HEADS):
        idx[UQ_NOPE + h * C_NOPE:UQ_NOPE + (h + 1) * C_NOPE] = h * per + np.arange(C_NOPE)
        idx[UQ_RA + h * LANE:UQ_RA + h * LANE + C_ROPE] = h * per + C_NOPE + np.arange(C_ROPE)
        idx[UQ_RB + h * LANE:UQ_RB + h * LANE + half] = h * per + C_NOPE + half + np.arange(half)
        idx[UQ_RB + h * LANE + half:UQ_RB + h * LANE + C_ROPE] = h * per + C_NOPE + np.arange(half)
    return idx


def _rope_tables(pos, reps):
    half = C_ROPE // 2
    inv = ROPE_THETA ** (-jnp.arange(half, dtype=F32) / half)
    ang = pos.astype(F32)[:, None] * inv[None, :]
    cos, sin = jnp.cos(ang), jnp.sin(ang)
    z = jnp.zeros((pos.shape[0], LANE - C_ROPE), F32)
    ct = jnp.concatenate([cos, cos, z], axis=1)
    st = jnp.concatenate([-sin, sin, z], axis=1)
    return jnp.tile(ct, (reps, 1)), jnp.tile(st, (reps, 1))


def _pad_lanes(v, idx):
    return _take_cols(v[None, :], idx)


def _layer_weights(l, w):
    hp_idx = _pad_heads_idx(0)
    out = {}
    out['nw_mix'] = w['norm_mix_w'][l][None, :]
    out['w1'] = _take_cols(w['w_in'][l], _win_layout()).astype(BF16)
    out['qnw'] = _pad_lanes(w['mla_qnorm_w'][l], np.concatenate([np.arange(Q_LORA), np.full(CQ_P - Q_LORA, -1)]))
    out['kvnw'] = w['mla_kvnorm_w'][l][None, :]
    wuq = w['mla_w_uq'][l].reshape(Q_LORA, C_HEADS * (C_NOPE + C_ROPE))
    wuq = _take_cols(wuq, _wuq_layout())
    out['wuq'] = jnp.concatenate([wuq, jnp.zeros((CQ_P - Q_LORA, N_UQ), F32)], axis=0).astype(BF16)
    wuk = jnp.transpose(w['mla_w_uk'][l], (1, 2, 0))
    eye = jnp.eye(C_HEADS, dtype=F32)
    out['wuk'] = jnp.einsum('hdc,hg->hdgc', wuk, eye).reshape(C_HEADS * C_NOPE, C_HEADS * KV_LORA).astype(BF16)
    wuv = jnp.transpose(w['mla_w_uv'][l], (1, 0, 2))
    out['wuv'] = jnp.einsum('hce,hg->hcge', wuv, eye).reshape(C_HEADS * KV_LORA, C_WIDTH).astype(BF16)
    gb = jnp.concatenate([w['mlstm_bi'][l], w['mlstm_bf'][l]])
    out['gb'] = _pad_lanes(gb, np.concatenate([np.arange(2 * B_HEADS), np.full(LANE - 2 * B_HEADS, -1)]))
    out['mnw'] = _pad_lanes(w['mlstm_norm_w'][l], hp_idx)
    wo = w['w_out'][l]
    rows = np.concatenate([np.arange(A_WIDTH), np.where(hp_idx < 0, -1, hp_idx + A_WIDTH),
                           A_WIDTH + B_WIDTH + np.arange(C_WIDTH)])
    out['wo'] = _take_cols(wo, rows, axis=0).astype(BF16)
    out['nw_x'] = w['norm_x_w'][l][None, :]
    out['nw_mem'] = w['norm_mem_w'][l][None, :]
    out['xwq'] = w['xa_wq'][l].reshape(D_MODEL, D_MODEL).astype(BF16)
    out['xwk'] = w['xa_wk'][l].reshape(D_MODEL, D_MODEL).astype(BF16)
    out['xwv'] = w['xa_wv'][l].reshape(D_MODEL, D_MODEL).astype(BF16)
    out['xwo'] = w['xa_wo'][l].reshape(D_MODEL, D_MODEL).astype(BF16)
    out['nw_ffn'] = w['norm_ffn_w'][l][None, :]
    out['wg'] = w['ffn_wg'][l].astype(BF16)
    out['wu'] = w['ffn_wu'][l].astype(BF16)
    out['wd'] = w['ffn_wd'][l].astype(BF16)
    out['cw'] = w['ffn_conv_w'][l]
    out['cb'] = w['ffn_conv_b'][l][None, :]
    return out


def _chunk_mixers(ws, cb, t_rows):
    reps = CHUNK // t_rows
    wt = ws[:, :t_rows, :t_rows]
    if reps > 1:
        wt = jnp.einsum('ab,hts->hatbs', jnp.eye(reps, dtype=F32), wt).reshape(A_HEADS, CHUNK, CHUNK)
    bias = jnp.tile(jnp.repeat(jnp.transpose(cb[:, :t_rows]), A_HEAD_DIM, axis=1), (reps, 1))
    return wt, bias


def _mixer_block(x, lw, *, mlstm_state, pos_tables, chunk_w, tm, act_dtype, ns, n_tiles, n_chunks, mla_fn):
    m = x.shape[0]
    cos_t, sin_t = pos_tables
    gu, gv, q, k, v, so, qc, lat, gate = _inproj(x, lw['nw_mix'], lw['w1'], lw['qnw'], lw['kvnw'], lw['wuq'],
                                                 lw['wuk'], lw['gb'], cos_t, sin_t, tm=tm, act_dtype=act_dtype)
    ya = _chunkmlp(gu, gv, chunk_w[0], chunk_w[1], tm=tm)
    grow = jnp.transpose(gate[:, :16].reshape(m // CHUNK, CHUNK, 16), (0, 2, 1))
    c0, n0, m0 = mlstm_state
    yb, c_new, n_new, m_rows = _mlstm(q, k, v, so, gate, grow, m0, c0, n0, lw['mnw'], ns=ns, n_tiles=n_tiles,
                                      n_chunks=n_chunks, g_tiles=2 if (ns == 1 and n_tiles % 2 == 0) else 1)
    yc = mla_fn(qc, lat)
    return (ya, yb, yc), (c_new, n_new, m_rows), lat, gv


def kernel(x_prompt, x_sample, mem_prompt, cache_mla, page_table, cache_mem_k, cache_mem_v, state_mlstm_C, state_mlstm_n, state_mlstm_m, state_ffn_conv, norm_mix_w, w_in, chunk_ws, chunk_b, mlstm_bi, mlstm_bf, mlstm_norm_w, mla_qnorm_w, mla_kvnorm_w, mla_w_uq, mla_w_uk, mla_w_uv, w_out, norm_x_w, norm_mem_w, xa_wq, xa_wk, xa_wv, xa_wo, norm_ffn_w, ffn_wg, ffn_wu, ffn_conv_w, ffn_conv_b, ffn_wd, norm_final_w):
    w = dict(norm_mix_w=norm_mix_w, w_in=w_in, mlstm_bi=mlstm_bi, mlstm_bf=mlstm_bf, mlstm_norm_w=mlstm_norm_w,
             mla_qnorm_w=mla_qnorm_w, mla_kvnorm_w=mla_kvnorm_w, mla_w_uq=mla_w_uq, mla_w_uk=mla_w_uk,
             mla_w_uv=mla_w_uv, w_out=w_out, norm_x_w=norm_x_w, norm_mem_w=norm_mem_w, xa_wq=xa_wq, xa_wk=xa_wk,
             xa_wv=xa_wv, xa_wo=xa_wo, norm_ffn_w=norm_ffn_w, ffn_wg=ffn_wg, ffn_wu=ffn_wu,
             ffn_conv_w=ffn_conv_w, ffn_conv_b=ffn_conv_b, ffn_wd=ffn_wd)
    depth = w_in.shape[0]
    nbp, S, _ = x_prompt.shape
    nbs, T, _ = x_sample.shape
    n_pages = page_table.shape[1]
    past_len = n_pages * PAGE_SIZE
    mp, ms = nbp * S, nbs * T
    assert S % CHUNK == 0 and CHUNK % T == 0 and T % 8 == 0 and ms % CHUNK == 0

    tm_p = min(512, S)
    tm_s = min(512, ms)
    tq = min(512, S)
    tk = min(512, S)
    ns_s = CHUNK // T
    xg = 4 if nbs % 4 == 0 else 1
    tm_ffn_p = min(512, S)
    tm_ffn_s = min(256, ms)

    pos_p = _rope_tables(jnp.arange(S), 1)
    pos_s = _rope_tables(past_len + jnp.arange(T), tm_s // T)
    fnw = norm_final_w[None, :]

    xp = x_prompt.reshape(mp, D_MODEL)
    xs = x_sample.reshape(ms, D_MODEL)
    mem = mem_prompt.reshape(nbp * MEM_LEN, D_MODEL)

    zero_c = jnp.zeros((nbp, B_HEADS, B_HEAD_DIM, B_HEAD_DIM), F32)
    zero_n = jnp.zeros((nbp, B_HEADS, B_HEAD_DIM), F32)
    zero_m = jnp.zeros((nbp, CHUNK, LANE), F32)
    zero_conv = jnp.zeros((nbp, 1, D_FF), F32)
    cache_t = jnp.swapaxes(cache_mla, 2, 3)

    outs = {k: [] for k in ('lat_p', 'lat_s', 'mk', 'mv', 'cp', 'np', 'mp', 'cs', 'ns', 'ms', 'conv_p', 'conv_s',
                            'chunkv')}
    yp = ys = None
    for l in range(depth):
        lw = _layer_weights(l, w)
        last = l == depth - 1

        mla_p = functools.partial(_mla_prompt, wuv=lw['wuv'], nb=nbp, seq=S, tq=tq, tk=tk)
        ymix, st, lat, _ = _mixer_block(
            xp, lw, mlstm_state=(zero_c, zero_n, zero_m), pos_tables=pos_p,
            chunk_w=_chunk_mixers(chunk_ws[l], chunk_b[l], CHUNK), tm=tm_p, act_dtype=BF16, ns=1, n_tiles=nbp,
            n_chunks=S // CHUNK, mla_fn=lambda qc, lt: mla_p(qc, lt))
        xp, qx_p = _outproj(xp, *ymix, lw['wo'], lw['nw_x'], lw['xwq'], tm=tm_p, q_dtype=BF16)
        outs['lat_p'].append(lat.reshape(nbp, S, LATENT))
        outs['cp'].append(st[0])
        outs['np'].append(st[1])
        outs['mp'].append(st[2][:, CHUNK - 1, :B_HEADS])

        m0_rows = jnp.concatenate([jnp.repeat(state_mlstm_m[l], T, axis=0),
                                   jnp.zeros((ms, LANE - B_HEADS), F32)], axis=1).reshape(ms // CHUNK, CHUNK, LANE)
        mla_s = functools.partial(_mla_sample, page_table, wuv=lw['wuv'], cache_t=cache_t, layer=l, t_new=T,
                                  seqs=1)
        ymix, st, lat, gv = _mixer_block(
            xs, lw, mlstm_state=(state_mlstm_C[l], state_mlstm_n[l], m0_rows), pos_tables=pos_s,
            chunk_w=_chunk_mixers(chunk_ws[l], chunk_b[l], T), tm=tm_s, act_dtype=F32, ns=ns_s,
            n_tiles=ms // CHUNK, n_chunks=1, mla_fn=lambda qc, lt: mla_s(qc, lt))
        xs, qx_s = _outproj(xs, *ymix, lw['wo'], lw['nw_x'], lw['xwq'], tm=tm_s, q_dtype=F32)
        outs['lat_s'].append(lat.reshape(nbs, T, LATENT))
        outs['cs'].append(st[0])
        outs['ns'].append(st[1])
        outs['ms'].append(st[2].reshape(nbs, T, LANE)[:, T - 1, :B_HEADS])
        outs['chunkv'].append(gv.reshape(nbs, T, A_WIDTH))

        mk, mv, mk_b, mv_b = _memkv(mem, lw['nw_mem'], lw['xwk'], lw['xwv'], tm=min(512, nbp * MEM_LEN))
        outs['mk'].append(mk.reshape(nbp, MEM_LEN, X_HEADS, X_HEAD_DIM))
        outs['mv'].append(mv.reshape(nbp, MEM_LEN, X_HEADS, X_HEAD_DIM))
        xp = _xattn_prompt(xp, qx_p, mk_b, mv_b, lw['xwo'], tm=tm_p, tiles_per_seq=S // tm_p)
        xs = _xattn_sample(xs, qx_s, cache_mem_k, cache_mem_v, lw['xwo'], layer=l, n_seq=xg, t_rows=T)

        res_p = _ffn(xp, lw['nw_ffn'], lw['wg'], lw['wu'], lw['wd'], lw['cw'], lw['cb'], (zero_conv, zero_conv),
                     fnw, tm=tm_ffn_p, seq_rows=0, tiles_per_seq=S // tm_ffn_p, final_norm=last)
        res_s = _ffn(xs, lw['nw_ffn'], lw['wg'], lw['wu'], lw['wd'], lw['cw'], lw['cb'], state_ffn_conv, fnw,
                     tm=tm_ffn_s, seq_rows=T, tiles_per_seq=1, final_norm=last, layer=l)
        xp, xs = res_p[0], res_s[0]
        outs['conv_p'].append(res_p[1][S // tm_ffn_p - 1::S // tm_ffn_p])
        outs['conv_s'].append(res_s[1])
        if last:
            yp, ys = res_p[2], res_s[2]

    st = lambda k: jnp.stack(outs[k])
    return (yp.reshape(nbp, S, D_MODEL), ys.reshape(nbs, T, D_MODEL),
            st('lat_p'), st('lat_s'), st('mk'), st('mv'),
            st('cp'), st('np'), st('mp'), st('cs'), st('ns'), st('ms'),
            st('conv_p'), st('conv_s'), st('chunkv'))
```

```python
import functools
import math

import numpy as np
import jax
import jax.numpy as jnp
from jax import lax
from jax.experimental import pallas as pl
from jax.experimental.pallas import tpu as pltpu

F32 = jnp.float32
BF16 = jnp.bfloat16

D_MODEL = 1024
PAGE_SIZE = 128
A_HEADS = 4
A_WIDTH = 256
A_HEAD_DIM = 64
CHUNK = 128
B_HEADS = 4
B_WIDTH = 384
B_HEAD_DIM = 96
MLSTM_KSCALE = B_HEAD_DIM ** -0.5
C_WIDTH = 384
C_HEADS = 6
C_V_DIM = 64
C_NOPE = 64
C_ROPE = 32
Q_LORA = 192
KV_LORA = 128
LATENT = KV_LORA + C_ROPE
MLA_SCALE = (C_NOPE + C_ROPE) ** -0.5
ROPE_THETA = 10000.0
MEM_LEN = 256
X_HEADS = 4
X_HEAD_DIM = 256
XA_SCALE = X_HEAD_DIM ** -0.5
D_FF = 2816
CONV_W = 3
EPS = 1e-6

LANE = 128
HP = 128
BW_P = B_HEADS * HP
FF_CHUNK = 256
N_FF_CHUNKS = D_FF // FF_CHUNK
VMEM_LIMIT = 48 * 1024 * 1024

S_AU, S_AV, S_BQ, S_BK, S_BV, S_BO = 0, 256, 512, 1024, 1536, 2048
S_CQ, S_CKV, S_R1, S_R2, S_G = 2560, 2816, 2944, 3072, 3200
N_IN = 3328
CQ_P = 256
UQ_NOPE, UQ_RA, UQ_RB, N_UQ = 0, 384, 1152, 1920
QC_SLOT = 256
N_MIX_P = A_WIDTH + BW_P + C_WIDTH


def _rms(x, w, n):
    ms = jnp.sum(x * x, axis=-1, keepdims=True) * (1.0 / n)
    return x * lax.rsqrt(ms + EPS) * w


def _gelu(x):
    c = math.sqrt(2.0 / math.pi)
    return x * (0.5 * (1.0 + jnp.tanh(c * (x + 0.044715 * (x * x * x)))))


def _sigmoid(x):
    return 1.0 / (1.0 + jnp.exp(-x))


def _log_sigmoid(x):
    return jnp.minimum(x, 0.0) - jnp.log1p(jnp.exp(-jnp.abs(x)))


def _dot(a, b):
    return jnp.dot(a, b, preferred_element_type=F32)


def _dot_nt(a, b):
    return lax.dot_general(a, b, (((1,), (1,)), ((), ())), preferred_element_type=F32)


def _dot_tn(a, b):
    return lax.dot_general(a, b, (((0,), (0,)), ((), ())), preferred_element_type=F32)


def _split3(a):
    a1 = a.astype(BF16)
    r1 = a - a1.astype(F32)
    a2 = r1.astype(BF16)
    a3 = (r1 - a2.astype(F32)).astype(BF16)
    return a1, a2, a3


def _wspec(shape):
    nd = len(shape)
    return pl.BlockSpec(shape, lambda *_: (0,) * nd, pipeline_mode=pl.Buffered(1))


def _params(n_axes):
    return pltpu.CompilerParams(dimension_semantics=("arbitrary",) * n_axes,
                                vmem_limit_bytes=VMEM_LIMIT)


def _inproj_kernel(x_ref, nw_ref, w1_ref, qnw_ref, kvnw_ref, wuq_ref, wuk_ref, gb_ref, cos_ref, sin_ref,
                   gu_ref, gv_ref, q_ref, k_ref, v_ref, so_ref, qc_ref, lat_ref, gate_ref):
    xn = _rms(x_ref[...], nw_ref[...], D_MODEL).astype(BF16)

    def seg(a, b):
        return _dot(xn, w1_ref[:, a:b])

    gu_ref[...] = _gelu(seg(S_AU, S_AV))
    gv_ref[...] = _gelu(seg(S_AV, S_BQ))
    q_ref[...] = seg(S_BQ, S_BK).astype(q_ref.dtype)
    k_ref[...] = (seg(S_BK, S_BV) * MLSTM_KSCALE).astype(k_ref.dtype)
    v_ref[...] = seg(S_BV, S_BO).astype(v_ref.dtype)
    so_ref[...] = _sigmoid(seg(S_BO, S_CQ))

    c_q = _rms(seg(S_CQ, S_CKV), qnw_ref[...], Q_LORA).astype(BF16)
    q2 = _dot(c_q, wuq_ref[...])
    qlat = _dot(q2[:, UQ_NOPE:UQ_RA].astype(BF16), wuk_ref[...])
    cos = cos_ref[...]
    sin = sin_ref[...]
    for h in range(C_HEADS):
        ra = q2[:, UQ_RA + LANE * h:UQ_RA + LANE * (h + 1)]
        rb = q2[:, UQ_RB + LANE * h:UQ_RB + LANE * (h + 1)]
        qr = ra * cos + rb * sin
        qc_ref[:, QC_SLOT * h:QC_SLOT * h + LANE] = (qlat[:, LANE * h:LANE * (h + 1)] * MLA_SCALE).astype(qc_ref.dtype)
        qc_ref[:, QC_SLOT * h + LANE:QC_SLOT * (h + 1)] = (qr * MLA_SCALE).astype(qc_ref.dtype)

    lat_ref[:, 0:KV_LORA] = _rms(seg(S_CKV, S_R1), kvnw_ref[...], KV_LORA)
    kr = seg(S_R1, S_R2) * cos + seg(S_R2, S_G) * sin
    lat_ref[:, KV_LORA:LATENT] = kr[:, :C_ROPE]

    gt = seg(S_G, N_IN) + gb_ref[...]
    lane = lax.broadcasted_iota(jnp.int32, gt.shape, 1)
    gate_ref[...] = jnp.where(lane < B_HEADS, gt, jnp.where(lane < 2 * B_HEADS, _log_sigmoid(gt), 0.0))


def _inproj(x, nw, w1, qnw, kvnw, wuq, wuk, gb, cos_t, sin_t, *, tm, act_dtype):
    m = x.shape[0]
    n_pos_blocks = cos_t.shape[0] // tm
    row = lambda w: pl.BlockSpec((tm, w), lambda i: (i, 0))
    pos = pl.BlockSpec((tm, LANE), lambda i: (i % n_pos_blocks, 0))
    out_shapes = (
        jax.ShapeDtypeStruct((m, A_WIDTH), F32), jax.ShapeDtypeStruct((m, A_WIDTH), F32),
        jax.ShapeDtypeStruct((m, BW_P), act_dtype), jax.ShapeDtypeStruct((m, BW_P), act_dtype),
        jax.ShapeDtypeStruct((m, BW_P), act_dtype), jax.ShapeDtypeStruct((m, BW_P), F32),
        jax.ShapeDtypeStruct((m, C_HEADS * QC_SLOT), act_dtype), jax.ShapeDtypeStruct((m, LATENT), F32),
        jax.ShapeDtypeStruct((m, LANE), F32))
    return pl.pallas_call(
        _inproj_kernel,
        grid=(m // tm,),
        in_specs=[row(D_MODEL), _wspec((1, D_MODEL)), _wspec((D_MODEL, N_IN)), _wspec((1, CQ_P)),
                  _wspec((1, KV_LORA)), _wspec((CQ_P, N_UQ)), _wspec((C_HEADS * C_NOPE, C_HEADS * LANE)),
                  _wspec((1, LANE)), pos, pos],
        out_specs=(row(A_WIDTH), row(A_WIDTH), row(BW_P), row(BW_P), row(BW_P), row(BW_P),
                   row(C_HEADS * QC_SLOT), row(LATENT), row(LANE)),
        out_shape=out_shapes,
        compiler_params=_params(1),
        name="inproj",
    )(x, nw, w1, qnw, kvnw, wuq, wuk, gb, cos_t, sin_t)


def _chunkmlp_kernel(gu_ref, gv_ref, ws_ref, bias_ref, ya_ref, *, n_chunks):
    r = lax.broadcasted_iota(jnp.int32, (CHUNK, CHUNK), 0)
    c = lax.broadcasted_iota(jnp.int32, (CHUNK, CHUNK), 1)
    wsm = [jnp.where(r >= c, ws_ref[h], 0.0).astype(BF16) for h in range(A_HEADS)]
    head_of_lane = lax.broadcasted_iota(jnp.int32, (CHUNK, A_WIDTH), 1) >> (A_HEAD_DIM.bit_length() - 1)
    for ci in range(n_chunks):
        rows = slice(ci * CHUNK, (ci + 1) * CHUNK)
        gv = gv_ref[rows, :]
        z = bias_ref[...]
        for h in range(A_HEADS):
            z = z + _dot(wsm[h], jnp.where(head_of_lane == h, gv, 0.0).astype(BF16))
        ya_ref[rows, :] = gu_ref[rows, :] * z


def _chunkmlp(gu, gv, ws, bias, *, tm):
    m = gu.shape[0]
    row = pl.BlockSpec((tm, A_WIDTH), lambda i: (i, 0))
    return pl.pallas_call(
        functools.partial(_chunkmlp_kernel, n_chunks=tm // CHUNK),
        grid=(m // tm,),
        in_specs=[row, row, _wspec((A_HEADS, CHUNK, CHUNK)), _wspec((CHUNK, A_WIDTH))],
        out_specs=row,
        out_shape=jax.ShapeDtypeStruct((m, A_WIDTH), F32),
        compiler_params=_params(1),
        name="chunkmlp",
    )(gu, gv, ws, bias)


_COL_M, _COL_G, _COL_RS = 0, 8, 16


def _mlstm_kernel(q_ref, k_ref, v_ref, so_ref, gcol_ref, grow_ref, m0_ref, c0_ref, n0_ref, nw_ref,
                  yb_ref, cout_ref, nout_ref, mout_ref,
                  cs_ref, ns_ref, mp_ref, col_ref, sv_ref, qc_ref, nr_ref, *, ns, ls, g_tiles):
    @pl.when(pl.program_id(1) == 0)
    def _init():
        cs_ref[...] = jnp.zeros(cs_ref.shape, F32)
        ns_ref[...] = jnp.zeros(ns_ref.shape, F32)
        for s in range(g_tiles):
            for g in range(ns):
                for h in range(B_HEADS):
                    cs_ref[s, g * B_HEADS + h, :B_HEAD_DIM, :B_HEAD_DIM] = c0_ref[s, g, h]
                    ns_ref[s, g * B_HEADS + h, 0:1, :B_HEAD_DIM] = n0_ref[s, g, h:h + 1, :]
        if ns == 1:
            for s in range(g_tiles):
                for h in range(B_HEADS):
                    mp_ref[s, h:h + 1, :] = jnp.broadcast_to(m0_ref[s, 0:1, h:h + 1], (1, LANE))
        else:
            mp_ref[...] = m0_ref[...]

    for s in range(g_tiles):
        at = lambda r: r.at[s]
        if ns == 1:
            _mlstm_chunk(at(q_ref), at(k_ref), at(v_ref), at(so_ref), at(gcol_ref), at(grow_ref), nw_ref,
                         at(yb_ref), at(cout_ref), at(nout_ref), at(mout_ref), at(cs_ref), at(ns_ref), at(mp_ref))
        else:
            _mlstm_tile(at(q_ref), at(k_ref), at(v_ref), at(so_ref), at(gcol_ref), at(grow_ref), nw_ref,
                        at(yb_ref), at(cout_ref), at(nout_ref), at(mout_ref),
                        at(cs_ref), at(ns_ref), at(mp_ref), at(col_ref), at(sv_ref), at(qc_ref), at(nr_ref),
                        ns=ns, ls=ls)


def _mlstm_chunk(q_ref, k_ref, v_ref, so_ref, gcol_ref, grow_ref, nw_ref,
                 yb_ref, cout_ref, nout_ref, mout_ref, cs_ref, ns_ref, mp_ref):
    L = CHUNK
    row = lax.broadcasted_iota(jnp.int32, (L, L), 0)
    col = lax.broadcasted_iota(jnp.int32, (L, L), 1)
    causal = col <= row
    cum = jnp.where(causal, 1.0, 0.0).astype(BF16)
    gcol = gcol_ref[...]
    grow = grow_ref[...]
    brow = sum(_dot_nt(t, cum) for t in _split3(grow))
    ones = jnp.ones((L, HP), BF16)
    mout_ref[...] = jnp.zeros(mout_ref.shape, F32)
    for h in range(B_HEADS):
        hs = slice(h * HP, (h + 1) * HP)
        qh = q_ref[:, hs].astype(BF16)
        kh = k_ref[:, hs].astype(BF16)
        vh = v_ref[:, hs].astype(BF16)
        ig = jnp.broadcast_to(gcol[:, h:h + 1], (L, HP))
        lf = jnp.broadcast_to(gcol[:, B_HEADS + h:B_HEADS + h + 1], (L, HP))
        bc = sum(_dot(cum, t) for t in _split3(lf))
        bm = bc + mp_ref[h:h + 1, :]
        logd = jnp.where(causal, grow[h:h + 1, :] + bc - brow[B_HEADS + h:B_HEADS + h + 1, :], -jnp.inf)
        m_t = jnp.maximum(bm, jnp.max(logd, axis=-1, keepdims=True))
        s = (_dot_nt(qh, kh) * jnp.exp(logd - m_t)).astype(BF16)
        sv = _dot(s, jnp.concatenate([vh, ones], axis=1))
        gdec = jnp.exp(bm - m_t)

        c_old = cs_ref[h]
        n_old = ns_ref[h, 0:1, :]
        num = gdec * _dot_nt(qh, c_old.astype(BF16)) + sv[:, 0:HP]
        den = gdec * jnp.sum(qh.astype(F32) * n_old, axis=-1, keepdims=True) + sv[:, HP:2 * HP]
        hh = num / jnp.maximum(jnp.abs(den), jnp.exp(-m_t))
        yb_ref[:, hs] = so_ref[:, hs] * _rms(hh, nw_ref[:, hs], B_HEAD_DIM)

        m_new, b_last, decay = m_t[L - 1:L, :], bc[L - 1:L, :], gdec[L - 1:L, :]
        w = jnp.exp(ig + b_last - bc - m_new)
        c_new = decay * c_old + _dot_tn((vh.astype(F32) * w).astype(BF16), kh)
        n_new = decay * n_old + jnp.sum(kh.astype(F32) * w, axis=0, keepdims=True)
        cs_ref[h] = c_new
        ns_ref[h, 0:1, :] = n_new
        mp_ref[h:h + 1, :] = m_new
        mout_ref[L - 1:L, h:h + 1] = m_new[:, 0:1]
        cout_ref[0, h] = c_new[:B_HEAD_DIM, :B_HEAD_DIM]
        nout_ref[0, h:h + 1, :] = n_new[:, :B_HEAD_DIM]


def _mlstm_tile(q_ref, k_ref, v_ref, so_ref, gcol_ref, grow_ref, nw_ref,
                yb_ref, cout_ref, nout_ref, mout_ref,
                cs_ref, ns_ref, mp_ref, col_ref, sv_ref, qc_ref, nr_ref, *, ns, ls):
    L = CHUNK
    ls_shift = ls.bit_length() - 1

    col_ref[...] = jnp.zeros(col_ref.shape, F32)
    row = lax.broadcasted_iota(jnp.int32, (L, L), 0)
    col = lax.broadcasted_iota(jnp.int32, (L, L), 1)
    causal = (col <= row) & ((col >> ls_shift) == (row >> ls_shift))
    cum = jnp.where(causal, 1.0, 0.0).astype(BF16)
    rid = lax.broadcasted_iota(jnp.int32, (L, 1), 0)

    gcol = gcol_ref[...]
    grow = grow_ref[...]
    bcol = sum(_dot(cum, t) for t in _split3(gcol))
    brow = sum(_dot_nt(t, cum) for t in _split3(grow))
    mp = mp_ref[...]

    qs, ks, vs = [], [], []
    for h in range(B_HEADS):
        hs = slice(h * HP, (h + 1) * HP)
        qh = q_ref[:, hs].astype(BF16)
        kh = k_ref[:, hs].astype(BF16)
        vh = v_ref[:, hs].astype(BF16)
        qs.append(qh), ks.append(kh), vs.append(vh)
        ig_r = grow[h:h + 1, :]
        b_r = brow[B_HEADS + h:B_HEADS + h + 1, :]
        b_c = bcol[:, B_HEADS + h:B_HEADS + h + 1]
        bm = b_c + mp[:, h:h + 1]
        logd = jnp.where(causal, ig_r + b_c - b_r, -jnp.inf)
        m_t = jnp.maximum(bm, jnp.max(logd, axis=-1, keepdims=True))
        d = jnp.exp(logd - m_t)
        s = _dot_nt(qh, kh) * d
        sv_ref[h] = _dot(s.astype(BF16), vh)
        col_ref[:, _COL_M + h:_COL_M + h + 1] = m_t
        col_ref[:, _COL_G + h:_COL_G + h + 1] = jnp.exp(bm - m_t)
        col_ref[:, _COL_RS + h:_COL_RS + h + 1] = jnp.sum(s, axis=-1, keepdims=True)

    qc_ref[...] = jnp.zeros(qc_ref.shape, F32)
    nr_ref[...] = jnp.zeros(nr_ref.shape, F32)

    def seq_update(g):
        last = g * ls + (ls - 1)
        in_seq = (rid >> ls_shift) == g
        is_last = rid == last
        for h in range(B_HEADS):
            c_old = cs_ref[g * B_HEADS + h]
            n_old = ns_ref[g * B_HEADS + h, 0:1, :]
            qm = jnp.where(in_seq, qs[h], jnp.zeros_like(qs[h])) if ns > 1 else qs[h]
            qc_ref[h] += _dot_nt(qm, c_old.astype(BF16))
            nr_ref[h] += jnp.where(in_seq, n_old, 0.0)
            m_t = col_ref[:, _COL_M + h:_COL_M + h + 1]
            gdec = col_ref[:, _COL_G + h:_COL_G + h + 1]
            b_c = bcol[:, B_HEADS + h:B_HEADS + h + 1]
            ig_c = gcol[:, h:h + 1]
            pick = lambda a: jnp.sum(jnp.where(is_last, a, 0.0), axis=0, keepdims=True)
            m_new, b_last, decay = pick(m_t), pick(b_c), pick(gdec)
            w_c = jnp.where(in_seq, jnp.exp(ig_c + b_last - b_c - m_new), 0.0)
            vw = (vs[h].astype(F32) * w_c).astype(BF16)
            cs_ref[g * B_HEADS + h] = decay * c_old + _dot_tn(vw, ks[h])
            ns_ref[g * B_HEADS + h, 0:1, :] = (decay * n_old
                                               + jnp.sum(ks[h].astype(F32) * w_c, axis=0, keepdims=True))
            if ns == 1:
                mp_ref[:, h:h + 1] = jnp.broadcast_to(m_new, (L, 1))

    if ns == 1:
        seq_update(0)
    else:
        lax.fori_loop(0, ns, lambda g, c: (seq_update(g), c)[1], 0)

    for h in range(B_HEADS):
        hs = slice(h * HP, (h + 1) * HP)
        m_t = col_ref[:, _COL_M + h:_COL_M + h + 1]
        gdec = col_ref[:, _COL_G + h:_COL_G + h + 1]
        rs = col_ref[:, _COL_RS + h:_COL_RS + h + 1]
        num = gdec * qc_ref[h] + sv_ref[h]
        qn = jnp.sum(qs[h].astype(F32) * nr_ref[h], axis=-1, keepdims=True)
        den = gdec * qn + rs
        hh = num / jnp.maximum(jnp.abs(den), jnp.exp(-m_t))
        yb_ref[:, hs] = so_ref[:, hs] * _rms(hh, nw_ref[:, hs], B_HEAD_DIM)

    mout_ref[...] = col_ref[...]
    for g in range(ns):
        for h in range(B_HEADS):
            cout_ref[g, h] = cs_ref[g * B_HEADS + h, :B_HEAD_DIM, :B_HEAD_DIM]
            nout_ref[g, h:h + 1, :] = ns_ref[g * B_HEADS + h, 0:1, :B_HEAD_DIM]


def _mlstm(q, k, v, so, gcol, grow, m0, c0, n0, nw, *, ns, n_tiles, n_chunks, g_tiles):
    m = q.shape[0]
    nb = c0.shape[0]
    ls = CHUNK // ns
    g = g_tiles
    t3 = lambda a: a.reshape(n_tiles, n_chunks * CHUNK, a.shape[-1])
    rows = lambda w: pl.BlockSpec((g, CHUNK, w), lambda t, c: (t, c, 0))
    cspec = pl.BlockSpec((g, ns, B_HEADS, B_HEAD_DIM, B_HEAD_DIM), lambda t, c: (t, 0, 0, 0, 0))
    nspec = pl.BlockSpec((g, ns, B_HEADS, B_HEAD_DIM), lambda t, c: (t, 0, 0, 0))
    mspec = pl.BlockSpec((g, CHUNK, LANE), lambda t, c: (t, 0, 0))
    yb, c_new, n_new, m_rows = pl.pallas_call(
        functools.partial(_mlstm_kernel, ns=ns, ls=ls, g_tiles=g),
        grid=(n_tiles // g, n_chunks),
        in_specs=[rows(BW_P), rows(BW_P), rows(BW_P), rows(BW_P), rows(LANE),
                  pl.BlockSpec((g, None, 16, CHUNK), lambda t, c: (t, c, 0, 0)),
                  mspec, cspec, nspec, _wspec((1, BW_P))],
        out_specs=(rows(BW_P), cspec, nspec, mspec),
        out_shape=(jax.ShapeDtypeStruct((n_tiles, n_chunks * CHUNK, BW_P), F32),
                   jax.ShapeDtypeStruct((n_tiles, ns, B_HEADS, B_HEAD_DIM, B_HEAD_DIM), F32),
                   jax.ShapeDtypeStruct((n_tiles, ns, B_HEADS, B_HEAD_DIM), F32),
                   jax.ShapeDtypeStruct((n_tiles, CHUNK, LANE), F32)),
        scratch_shapes=[pltpu.VMEM((g, ns * B_HEADS, HP, HP), F32), pltpu.VMEM((g, ns * B_HEADS, 8, HP), F32),
                        pltpu.VMEM((g, CHUNK, LANE), F32), pltpu.VMEM((g, CHUNK, LANE), F32),
                        pltpu.VMEM((g, B_HEADS, CHUNK, HP), F32), pltpu.VMEM((g, B_HEADS, CHUNK, HP), F32),
                        pltpu.VMEM((g, B_HEADS, CHUNK, HP), F32)],
        compiler_params=_params(2),
        name="mlstm",
    )(t3(q), t3(k), t3(v), t3(so), t3(gcol), grow.reshape(n_tiles, n_chunks, 16, CHUNK), m0,
      c0.reshape(n_tiles, ns, B_HEADS, B_HEAD_DIM, B_HEAD_DIM), n0.reshape(n_tiles, ns, B_HEADS, B_HEAD_DIM), nw)
    return (yb.reshape(m, BW_P), c_new.reshape(nb, B_HEADS, B_HEAD_DIM, B_HEAD_DIM),
            n_new.reshape(nb, B_HEADS, B_HEAD_DIM), m_rows)


def _stack_heads(qc_ref, dtype):
    return jnp.concatenate([qc_ref[:, QC_SLOT * h:QC_SLOT * h + LATENT] for h in range(C_HEADS)],
                           axis=0).astype(dtype)


def _unstack_project(o, t, wuv_ref):
    wide = jnp.concatenate([o[h * t:(h + 1) * t] for h in range(C_HEADS)], axis=1)
    return _dot(wide.astype(BF16), wuv_ref[...])


def _softmax_step(s, pv, m_ref, acc_ref):
    reps = s.shape[1] // LANE
    m_prev = m_ref[...]
    m_new = jnp.maximum(m_prev, jnp.max(s, axis=-1, keepdims=True))
    p = jnp.exp(s - jnp.concatenate([m_new] * reps, axis=1))
    corr = jnp.exp(m_prev - m_new)
    acc_ref[...] = jnp.concatenate([corr, corr], axis=1) * acc_ref[...] + pv(p.astype(BF16))
    m_ref[...] = m_new


def _softmax_result(acc_ref):
    acc = acc_ref[...]
    return acc[:, :KV_LORA] / acc[:, KV_LORA:]


def _mla_prompt_kernel(qc_ref, lat_ref, wuv_ref, yc_ref, kb_ref, vb_ref, m_ref, acc_ref, *, tq, tk, row_groups):
    qi = pl.program_id(1)

    @pl.when(qi == 0)
    def _cast_keys():
        lat = lat_ref[...]
        kb_ref[...] = lat.astype(BF16)
        vb_ref[:, 0:KV_LORA] = lat[:, 0:KV_LORA].astype(BF16)
        vb_ref[:, KV_LORA:2 * KV_LORA] = jnp.ones((lat.shape[0], KV_LORA), BF16)

    qst = _stack_heads(qc_ref, BF16)
    m_ref[...] = jnp.full(m_ref.shape, -jnp.inf, F32)
    acc_ref[...] = jnp.zeros(acc_ref.shape, F32)
    rows = C_HEADS * tq

    gr = rows // row_groups

    def block(start, width, masked):
        kj = kb_ref[pl.ds(start, width), :]
        vj = vb_ref[pl.ds(start, width), :]
        for g in range(row_groups):
            rs = slice(g * gr, (g + 1) * gr)
            s = _dot_nt(qst[rs], kj)
            if masked:
                r = g * gr + lax.broadcasted_iota(jnp.int32, (gr, width), 0)
                qpos = qi * tq + (r & (tq - 1))
                kpos = start + lax.broadcasted_iota(jnp.int32, (gr, width), 1)
                s = jnp.where(kpos <= qpos, s, -jnp.inf)
            _softmax_step(s, lambda p: _dot(p, vj), m_ref.at[rs], acc_ref.at[rs])

    n_full = (qi * tq) // tk
    lax.fori_loop(0, n_full, lambda j, c: (block(pl.multiple_of(j * tk, tk), tk, False), c)[1], 0)
    block(pl.multiple_of(n_full * tk, tk), tk, True)
    yc_ref[...] = _unstack_project(_softmax_result(acc_ref), tq, wuv_ref)


def _mla_prompt(qc, lat, wuv, *, nb, seq, tq, tk):
    m = qc.shape[0]
    n_q = seq // tq
    rows = C_HEADS * tq
    assert tk in (tq, 2 * tq)
    return pl.pallas_call(
        functools.partial(_mla_prompt_kernel, tq=tq, tk=tk, row_groups=6),
        grid=(nb, n_q),
        in_specs=[pl.BlockSpec((tq, C_HEADS * QC_SLOT), lambda b, i: (b * n_q + i, 0)),
                  pl.BlockSpec((seq, LATENT), lambda b, i: (b, 0)),
                  _wspec((C_HEADS * KV_LORA, C_WIDTH))],
        out_specs=pl.BlockSpec((tq, C_WIDTH), lambda b, i: (b * n_q + i, 0)),
        out_shape=jax.ShapeDtypeStruct((m, C_WIDTH), F32),
        scratch_shapes=[pltpu.VMEM((seq, LATENT), BF16), pltpu.VMEM((seq, 2 * KV_LORA), BF16),
                        pltpu.VMEM((rows, LANE), F32), pltpu.VMEM((rows, 2 * KV_LORA), F32)],
        compiler_params=_params(2),
        name="mla_prompt",
    )(qc, lat, wuv)


def _mla_sample_kernel(pt_ref, qc_ref, latn_ref, wuv_ref, cache_ref, yc_ref, buf_ref, sem_ref, kt_ref, kn_ref,
                       *, layer, n_pages, t_new, seqs):
    b = pl.program_id(0)
    nb = pl.num_programs(0)
    slot = b & 1
    rows = C_HEADS * t_new

    def page_copy(step, r, page, to_slot):
        return pltpu.make_async_copy(cache_ref.at[layer, pt_ref[step * seqs + r, page]],
                                     buf_ref.at[to_slot, r, page], sem_ref.at[to_slot])

    def request(step, to_slot):
        for r in range(seqs):
            for page in range(n_pages):
                page_copy(step, r, page, to_slot).start(priority=page % 2)

    @pl.when(b == 0)
    def _first():
        request(0, 0)

    @pl.when(b + 1 < nb)
    def _prefetch_next():
        request(b + 1, 1 - slot)

    for r in range(seqs):
        for page in range(n_pages):
            page_copy(b, r, page, slot).wait()

    tpos = lax.broadcasted_iota(jnp.int32, (rows, PAGE_SIZE), 0) & (t_new - 1)
    kidx = lax.broadcasted_iota(jnp.int32, (rows, PAGE_SIZE), 1)
    for r in range(seqs):
        rs = slice(r * t_new, (r + 1) * t_new)
        qst = jnp.concatenate([qc_ref[rs, QC_SLOT * h:QC_SLOT * h + LATENT] for h in range(C_HEADS)],
                              axis=0).astype(BF16)
        for page in range(n_pages):
            kt_ref[r, :, page * PAGE_SIZE:(page + 1) * PAGE_SIZE] = buf_ref[slot, r, page].astype(BF16)
        s_past = _dot(qst, kt_ref[r])

        kn_ref[r] = jnp.zeros(kn_ref.shape[1:], F32)
        kn_ref[r, 0:t_new, :] = latn_ref[rs, :]
        kn = kn_ref[r].astype(BF16)
        s_new = jnp.where(kidx <= tpos, _dot_nt(qst, kn), -jnp.inf)

        m = jnp.maximum(jnp.max(s_past, axis=-1, keepdims=True), jnp.max(s_new, axis=-1, keepdims=True))
        p_past = jnp.exp(s_past - m).astype(BF16)
        p_new = jnp.exp(s_new - m).astype(BF16)
        denom = (jnp.sum(p_past.astype(F32), axis=-1, keepdims=True)
                 + jnp.sum(p_new.astype(F32), axis=-1, keepdims=True))
        o = _dot_nt(p_past, kt_ref[r, 0:KV_LORA, :]) + _dot(p_new, kn[:, 0:KV_LORA])
        yc_ref[rs, :] = _unstack_project(o / denom, t_new, wuv_ref)


def _mla_sample(page_table, qc, lat_new, wuv, cache_t, *, layer, t_new, seqs):
    m = qc.shape[0]
    nb, n_pages = page_table.shape
    tr = seqs * t_new
    grid_spec = pltpu.PrefetchScalarGridSpec(
        num_scalar_prefetch=1,
        grid=(nb // seqs,),
        in_specs=[pl.BlockSpec((tr, C_HEADS * QC_SLOT), lambda b, pt: (b, 0)),
                  pl.BlockSpec((tr, LATENT), lambda b, pt: (b, 0)),
                  pl.BlockSpec((C_HEADS * KV_LORA, C_WIDTH), lambda b, pt: (0, 0)),
                  pl.BlockSpec(memory_space=pl.ANY)],
        out_specs=pl.BlockSpec((tr, C_WIDTH), lambda b, pt: (b, 0)),
        scratch_shapes=[pltpu.VMEM((2, seqs, n_pages, LATENT, PAGE_SIZE), F32), pltpu.SemaphoreType.DMA((2,)),
                        pltpu.VMEM((seqs, LATENT, n_pages * PAGE_SIZE), BF16),
                        pltpu.VMEM((seqs, PAGE_SIZE, LATENT), F32)])
    return pl.pallas_call(
        functools.partial(_mla_sample_kernel, layer=layer, n_pages=n_pages, t_new=t_new, seqs=seqs),
        grid_spec=grid_spec,
        out_shape=jax.ShapeDtypeStruct((m, C_WIDTH), F32),
        compiler_params=_params(1),
        name="mla_sample",
    )(page_table, qc, lat_new, wuv, cache_t)


def _outproj_kernel(x_ref, ya_ref, yb_ref, yc_ref, wo_ref, nxw_ref, wq_ref, xo_ref, q_ref):
    y = (_dot(ya_ref[...].astype(BF16), wo_ref[0:A_WIDTH, :])
         + _dot(yb_ref[...].astype(BF16), wo_ref[A_WIDTH:A_WIDTH + BW_P, :])
         + _dot(yc_ref[...].astype(BF16), wo_ref[A_WIDTH + BW_P:N_MIX_P, :]))
    xn = x_ref[...] + y
    xo_ref[...] = xn
    hq = _rms(xn, nxw_ref[...], D_MODEL).astype(BF16)
    q_ref[...] = (_dot(hq, wq_ref[...]) * XA_SCALE).astype(q_ref.dtype)


def _outproj(x, ya, yb, yc, wo, nxw, wq, *, tm, q_dtype):
    m = x.shape[0]
    row = lambda w: pl.BlockSpec((tm, w), lambda i: (i, 0))
    return pl.pallas_call(
        _outproj_kernel,
        grid=(m // tm,),
        in_specs=[row(D_MODEL), row(A_WIDTH), row(BW_P), row(C_WIDTH), _wspec((N_MIX_P, D_MODEL)),
                  _wspec((1, D_MODEL)), _wspec((D_MODEL, D_MODEL))],
        out_specs=(row(D_MODEL), row(D_MODEL)),
        out_shape=(jax.ShapeDtypeStruct((m, D_MODEL), F32), jax.ShapeDtypeStruct((m, D_MODEL), q_dtype)),
        compiler_params=_params(1),
        name="outproj",
    )(x, ya, yb, yc, wo, nxw, wq)


def _memkv_kernel(mem_ref, nw_ref, wk_ref, wv_ref, k_ref, v_ref, kb_ref, vb_ref):
    mn = _rms(mem_ref[...], nw_ref[...], D_MODEL).astype(BF16)
    k = _dot(mn, wk_ref[...])
    v = _dot(mn, wv_ref[...])
    kb_ref[...] = k.astype(BF16)
    vb_ref[...] = v.astype(BF16)
    for h in range(X_HEADS):
        hs = slice(h * X_HEAD_DIM, (h + 1) * X_HEAD_DIM)
        k_ref[:, h, :] = k[:, hs]
        v_ref[:, h, :] = v[:, hs]


def _memkv(mem, nw, wk, wv, *, tm):
    m = mem.shape[0]
    row = pl.BlockSpec((tm, D_MODEL), lambda i: (i, 0))
    out = pl.BlockSpec((tm, X_HEADS, X_HEAD_DIM), lambda i: (i, 0, 0))
    return pl.pallas_call(
        _memkv_kernel,
        grid=(m // tm,),
        in_specs=[row, _wspec((1, D_MODEL)), _wspec((D_MODEL, D_MODEL)), _wspec((D_MODEL, D_MODEL))],
        out_specs=(out, out, row, row),
        out_shape=(jax.ShapeDtypeStruct((m, X_HEADS, X_HEAD_DIM), F32),) * 2
                  + (jax.ShapeDtypeStruct((m, D_MODEL), BF16),) * 2,
        compiler_params=_params(1),
        name="memkv",
    )(mem, nw, wk, wv)


def _softmax_rows(s):
    p = jnp.exp(s - jnp.max(s, axis=-1, keepdims=True))
    return p / jnp.sum(p, axis=-1, keepdims=True)


def _xattn_prompt_kernel(x_ref, q_ref, k_ref, v_ref, wo_ref, xo_ref, o_ref):
    for h in range(X_HEADS):
        hs = slice(h * X_HEAD_DIM, (h + 1) * X_HEAD_DIM)
        p = _softmax_rows(_dot_nt(q_ref[:, hs], k_ref[:, hs]))
        o_ref[:, hs] = _dot(p.astype(BF16), v_ref[:, hs])
    xo_ref[...] = x_ref[...] + _dot(o_ref[...].astype(BF16), wo_ref[...])


def _xattn_prompt(x, q, k, v, wo, *, tm, tiles_per_seq):
    m = x.shape[0]
    row = pl.BlockSpec((tm, D_MODEL), lambda i: (i, 0))
    kv = pl.BlockSpec((MEM_LEN, D_MODEL), lambda i: (i // tiles_per_seq, 0))
    return pl.pallas_call(
        _xattn_prompt_kernel,
        grid=(m // tm,),
        in_specs=[row, row, kv, kv, _wspec((D_MODEL, D_MODEL))],
        out_specs=row,
        out_shape=jax.ShapeDtypeStruct((m, D_MODEL), F32),
        scratch_shapes=[pltpu.VMEM((tm, D_MODEL), F32)],
        compiler_params=_params(1),
        name="xattn_prompt",
    )(x, q, k, v, wo)


def _xattn_sample_kernel(x_ref, q_ref, k_ref, v_ref, wo_ref, xo_ref, o_ref, *, n_seq, t_rows):
    rows = X_HEADS * t_rows
    nk = MEM_LEN * X_HEADS
    r = lax.broadcasted_iota(jnp.int32, (rows, nk), 0)
    c = lax.broadcasted_iota(jnp.int32, (rows, nk), 1)
    keep = (c & (X_HEADS - 1)) == (r >> (t_rows.bit_length() - 1))
    for g in range(n_seq):
        rs = slice(g * t_rows, (g + 1) * t_rows)
        qst = jnp.concatenate([q_ref[rs, h * X_HEAD_DIM:(h + 1) * X_HEAD_DIM] for h in range(X_HEADS)], axis=0)
        k2 = k_ref[g].reshape(nk, X_HEAD_DIM).astype(BF16)
        v2 = v_ref[g].reshape(nk, X_HEAD_DIM).astype(BF16)
        p = _softmax_rows(jnp.where(keep, _dot_nt(qst.astype(BF16), k2), -jnp.inf))
        o = _dot(p.astype(BF16), v2)
        for h in range(X_HEADS):
            o_ref[rs, h * X_HEAD_DIM:(h + 1) * X_HEAD_DIM] = o[h * t_rows:(h + 1) * t_rows]
    xo_ref[...] = x_ref[...] + _dot(o_ref[...].astype(BF16), wo_ref[...])


def _xattn_sample(x, q, k, v, wo, *, layer, n_seq, t_rows):
    m = x.shape[0]
    r = n_seq * t_rows
    row = pl.BlockSpec((r, D_MODEL), lambda i: (i, 0))
    kv = pl.BlockSpec((None, n_seq, MEM_LEN, X_HEADS, X_HEAD_DIM), lambda i: (layer, i, 0, 0, 0))
    return pl.pallas_call(
        functools.partial(_xattn_sample_kernel, n_seq=n_seq, t_rows=t_rows),
        grid=(m // r,),
        in_specs=[row, row, kv, kv, _wspec((D_MODEL, D_MODEL))],
        out_specs=row,
        out_shape=jax.ShapeDtypeStruct((m, D_MODEL), F32),
        scratch_shapes=[pltpu.VMEM((r, D_MODEL), F32)],
        compiler_params=_params(1),
        name="xattn_sample",
    )(x, q, k, v, wo)


def _ffn_kernel(x_ref, nw_ref, wg_ref, wu_ref, wd_ref, cw_ref, cb_ref, *rest,
                seq_rows, tiles_per_seq, final_norm):
    if seq_rows:
        p_ref, fnw_ref, *rest = rest
    else:
        p0_ref, p1_ref, fnw_ref, *rest = rest
    if final_norm:
        xo_ref, gtail_ref, y_ref, carry_ref, act_ref = rest
    else:
        xo_ref, gtail_ref, carry_ref, act_ref = rest
    tm = x_ref.shape[0]
    x = x_ref[...]
    xn = _rms(x, nw_ref[...], D_MODEL).astype(BF16)
    rid = lax.broadcasted_iota(jnp.int32, (tm, 1), 0)
    if seq_rows:
        pos = rid & (seq_rows - 1)
    else:
        pos = rid

        @pl.when(pl.program_id(0) % tiles_per_seq == 0)
        def _seq_start():
            carry_ref[0:1, :] = p0_ref[...]
            carry_ref[1:2, :] = p1_ref[...]

    for c in range(N_FF_CHUNKS):
        cs = slice(c * FF_CHUNK, (c + 1) * FF_CHUNK)
        g = _dot(xn, wg_ref[:, cs])
        u = _dot(xn, wu_ref[:, cs])
        if seq_rows:
            n_seq = tm // seq_rows
            to_rows = lambda a: jnp.broadcast_to(a, (n_seq, seq_rows, FF_CHUNK)).reshape(tm, FF_CHUNK)
            prev0, prev1 = to_rows(p_ref[:, 0:1, cs]), to_rows(p_ref[:, 1:2, cs])
            gtail_ref[:, :, cs] = g.reshape(n_seq, seq_rows, FF_CHUNK)[:, seq_rows - 2:seq_rows, :]
        else:
            prev0, prev1 = carry_ref[0:1, cs], carry_ref[1:2, cs]
            gtail_ref[:, cs] = g[tm - 2:tm, :]
            carry_ref[0:2, cs] = g[tm - 2:tm, :]
        g1 = jnp.where(pos == 0, prev1, pltpu.roll(g, 1, 0))
        g2 = jnp.where(pos == 0, prev0, jnp.where(pos == 1, prev1, pltpu.roll(g, 2, 0)))
        y = cb_ref[:, cs] + g2 * cw_ref[0:1, cs] + g1 * cw_ref[1:2, cs] + g * cw_ref[2:3, cs]
        act_ref[:, cs] = (_gelu(y) * u).astype(BF16)
    xo = x + _dot(act_ref[...], wd_ref[...])
    xo_ref[...] = xo
    if final_norm:
        y_ref[...] = _rms(xo, fnw_ref[...], D_MODEL)


def _ffn(x, nw, wg, wu, wd, cw, cb, state, fnw, *, tm, seq_rows, tiles_per_seq, final_norm, layer=0):
    m = x.shape[0]
    n_tiles = m // tm
    row = pl.BlockSpec((tm, D_MODEL), lambda i: (i, 0))
    if seq_rows:
        n_seq = tm // seq_rows
        pspecs = [pl.BlockSpec((None, n_seq, CONV_W - 1, D_FF), lambda i: (layer, i, 0, 0))]
        gspec = pl.BlockSpec((n_seq, CONV_W - 1, D_FF), lambda i: (i, 0, 0))
        gshape = jax.ShapeDtypeStruct((m // seq_rows, CONV_W - 1, D_FF), F32)
        states = (state,)
    else:
        pspecs = [pl.BlockSpec((None, 1, D_FF), lambda i: (i // tiles_per_seq, 0, 0))] * 2
        gspec = pl.BlockSpec((None, 2, D_FF), lambda i: (i, 0, 0))
        gshape = jax.ShapeDtypeStruct((n_tiles, 2, D_FF), F32)
        states = tuple(state)
    return pl.pallas_call(
        functools.partial(_ffn_kernel, seq_rows=seq_rows, tiles_per_seq=tiles_per_seq, final_norm=final_norm),
        grid=(n_tiles,),
        in_specs=[row, _wspec((1, D_MODEL)), _wspec((D_MODEL, D_FF)), _wspec((D_MODEL, D_FF)),
                  _wspec((D_FF, D_MODEL)), _wspec((CONV_W, D_FF)), _wspec((1, D_FF))]
                 + pspecs + [_wspec((1, D_MODEL))],
        out_specs=(row, gspec) + ((row,) if final_norm else ()),
        out_shape=(jax.ShapeDtypeStruct((m, D_MODEL), F32), gshape)
                  + ((jax.ShapeDtypeStruct((m, D_MODEL), F32),) if final_norm else ()),
        scratch_shapes=[pltpu.VMEM((8, D_FF), F32), pltpu.VMEM((tm, D_FF), BF16)],
        compiler_params=_params(1),
        name="ffn",
    )(x, nw, wg, wu, wd, cw, cb, *states, fnw)


def _pad_heads_idx(start):
    idx = np.full((BW_P,), -1, np.int64)
    for h in range(B_HEADS):
        idx[h * HP:h * HP + B_HEAD_DIM] = start + h * B_HEAD_DIM + np.arange(B_HEAD_DIM)
    return idx


def _take_cols(w, idx, axis=-1):
    axis = axis % w.ndim
    pieces, i, n = [], 0, len(idx)
    while i < n:
        j = i + 1
        if idx[i] < 0:
            while j < n and idx[j] < 0:
                j += 1
            shape = w.shape[:axis] + (j - i,) + w.shape[axis + 1:]
            pieces.append(jnp.zeros(shape, w.dtype))
        else:
            while j < n and idx[j] == idx[j - 1] + 1:
                j += 1
            pieces.append(lax.slice_in_dim(w, int(idx[i]), int(idx[i]) + (j - i), axis=axis))
        i = j
    return jnp.concatenate(pieces, axis=axis)


def _win_layout():
    half = C_ROPE // 2
    base = {'au': 0, 'av': 256, 'bq': 512, 'bk': 896, 'bv': 1280, 'bo': 1664, 'bi': 2048, 'bf': 2052,
            'cq': 2056, 'ckv': 2248, 'ckr': 2376}
    idx = np.full((N_IN,), -1, np.int64)
    idx[S_AU:S_AU + 256] = base['au'] + np.arange(256)
    idx[S_AV:S_AV + 256] = base['av'] + np.arange(256)
    for s, name in ((S_BQ, 'bq'), (S_BK, 'bk'), (S_BV, 'bv'), (S_BO, 'bo')):
        idx[s:s + BW_P] = _pad_heads_idx(base[name])
    idx[S_CQ:S_CQ + Q_LORA] = base['cq'] + np.arange(Q_LORA)
    idx[S_CKV:S_CKV + KV_LORA] = base['ckv'] + np.arange(KV_LORA)
    idx[S_R1:S_R1 + C_ROPE] = base['ckr'] + np.arange(C_ROPE)
    idx[S_R2:S_R2 + half] = base['ckr'] + half + np.arange(half)
    idx[S_R2 + half:S_R2 + C_ROPE] = base['ckr'] + np.arange(half)
    idx[S_G:S_G + B_HEADS] = base['bi'] + np.arange(B_HEADS)
    idx[S_G + B_HEADS:S_G + 2 * B_HEADS] = base['bf'] + np.arange(B_HEADS)
    return idx


def _wuq_layout():
    half = C_ROPE // 2
    per = C_NOPE + C_ROPE
    idx = np.full((N_UQ,), -1, np.int64)
    for h in range(C_HEADS):
        idx[UQ_NOPE + h * C_NOPE:UQ_NOPE + (h + 1) * C_NOPE] = h * per + np.arange(C_NOPE)
        idx[UQ_RA + h * LANE:UQ_RA + h * LANE + C_ROPE] = h * per + C_NOPE + np.arange(C_ROPE)
        idx[UQ_RB + h * LANE:UQ_RB + h * LANE + half] = h * per + C_NOPE + half + np.arange(half)
        idx[UQ_RB + h * LANE + half:UQ_RB + h * LANE + C_ROPE] = h * per + C_NOPE + np.arange(half)
    return idx


def _rope_tables(pos, reps):
    half = C_ROPE // 2
    inv = ROPE_THETA ** (-jnp.arange(half, dtype=F32) / half)
    ang = pos.astype(F32)[:, None] * inv[None, :]
    cos, sin = jnp.cos(ang), jnp.sin(ang)
    z = jnp.zeros((pos.shape[0], LANE - C_ROPE), F32)
    ct = jnp.concatenate([cos, cos, z], axis=1)
    st = jnp.concatenate([-sin, sin, z], axis=1)
    return jnp.tile(ct, (reps, 1)), jnp.tile(st, (reps, 1))


def _pad_lanes(v, idx):
    return _take_cols(v[None, :], idx)


def _layer_weights(l, w):
    hp_idx = _pad_heads_idx(0)
    out = {}
    out['nw_mix'] = w['norm_mix_w'][l][None, :]
    out['w1'] = _take_cols(w['w_in'][l], _win_layout()).astype(BF16)
    out['qnw'] = _pad_lanes(w['mla_qnorm_w'][l], np.concatenate([np.arange(Q_LORA), np.full(CQ_P - Q_LORA, -1)]))
    out['kvnw'] = w['mla_kvnorm_w'][l][None, :]
    wuq = w['mla_w_uq'][l].reshape(Q_LORA, C_HEADS * (C_NOPE + C_ROPE))
    wuq = _take_cols(wuq, _wuq_layout())
    out['wuq'] = jnp.concatenate([wuq, jnp.zeros((CQ_P - Q_LORA, N_UQ), F32)], axis=0).astype(BF16)
    wuk = jnp.transpose(w['mla_w_uk'][l], (1, 2, 0))
    eye = jnp.eye(C_HEADS, dtype=F32)
    out['wuk'] = jnp.einsum('hdc,hg->hdgc', wuk, eye).reshape(C_HEADS * C_NOPE, C_HEADS * KV_LORA).astype(BF16)
    wuv = jnp.transpose(w['mla_w_uv'][l], (1, 0, 2))
    out['wuv'] = jnp.einsum('hce,hg->hcge', wuv, eye).reshape(C_HEADS * KV_LORA, C_WIDTH).astype(BF16)
    gb = jnp.concatenate([w['mlstm_bi'][l], w['mlstm_bf'][l]])
    out['gb'] = _pad_lanes(gb, np.concatenate([np.arange(2 * B_HEADS), np.full(LANE - 2 * B_HEADS, -1)]))
    out['mnw'] = _pad_lanes(w['mlstm_norm_w'][l], hp_idx)
    wo = w['w_out'][l]
    rows = np.concatenate([np.arange(A_WIDTH), np.where(hp_idx < 0, -1, hp_idx + A_WIDTH),
                           A_WIDTH + B_WIDTH + np.arange(C_WIDTH)])
    out['wo'] = _take_cols(wo, rows, axis=0).astype(BF16)
    out['nw_x'] = w['norm_x_w'][l][None, :]
    out['nw_mem'] = w['norm_mem_w'][l][None, :]
    out['xwq'] = w['xa_wq'][l].reshape(D_MODEL, D_MODEL).astype(BF16)
    out['xwk'] = w['xa_wk'][l].reshape(D_MODEL, D_MODEL).astype(BF16)
    out['xwv'] = w['xa_wv'][l].reshape(D_MODEL, D_MODEL).astype(BF16)
    out['xwo'] = w['xa_wo'][l].reshape(D_MODEL, D_MODEL).astype(BF16)
    out['nw_ffn'] = w['norm_ffn_w'][l][None, :]
    out['wg'] = w['ffn_wg'][l].astype(BF16)
    out['wu'] = w['ffn_wu'][l].astype(BF16)
    out['wd'] = w['ffn_wd'][l].astype(BF16)
    out['cw'] = w['ffn_conv_w'][l]
    out['cb'] = w['ffn_conv_b'][l][None, :]
    return out


def _chunk_mixers(ws, cb, t_rows):
    reps = CHUNK // t_rows
    wt = ws[:, :t_rows, :t_rows]
    if reps > 1:
        wt = jnp.einsum('ab,hts->hatbs', jnp.eye(reps, dtype=F32), wt).reshape(A_HEADS, CHUNK, CHUNK)
    bias = jnp.tile(jnp.repeat(jnp.transpose(cb[:, :t_rows]), A_HEAD_DIM, axis=1), (reps, 1))
    return wt, bias


def _mixer_block(x, lw, *, mlstm_state, pos_tables, chunk_w, tm, act_dtype, ns, n_tiles, n_chunks, mla_fn):
    m = x.shape[0]
    cos_t, sin_t = pos_tables
    gu, gv, q, k, v, so, qc, lat, gate = _inproj(x, lw['nw_mix'], lw['w1'], lw['qnw'], lw['kvnw'], lw['wuq'],
                                                 lw['wuk'], lw['gb'], cos_t, sin_t, tm=tm, act_dtype=act_dtype)
    ya = _chunkmlp(gu, gv, chunk_w[0], chunk_w[1], tm=tm)
    grow = jnp.transpose(gate[:, :16].reshape(m // CHUNK, CHUNK, 16), (0, 2, 1))
    c0, n0, m0 = mlstm_state
    yb, c_new, n_new, m_rows = _mlstm(q, k, v, so, gate, grow, m0, c0, n0, lw['mnw'], ns=ns, n_tiles=n_tiles,
                                      n_chunks=n_chunks, g_tiles=2 if (ns == 1 and n_tiles % 2 == 0) else 1)
    yc = mla_fn(qc, lat)
    return (ya, yb, yc), (c_new, n_new, m_rows), lat, gv


def kernel(x_prompt, x_sample, mem_prompt, cache_mla, page_table, cache_mem_k, cache_mem_v, state_mlstm_C, state_mlstm_n, state_mlstm_m, state_ffn_conv, norm_mix_w, w_in, chunk_ws, chunk_b, mlstm_bi, mlstm_bf, mlstm_norm_w, mla_qnorm_w, mla_kvnorm_w, mla_w_uq, mla_w_uk, mla_w_uv, w_out, norm_x_w, norm_mem_w, xa_wq, xa_wk, xa_wv, xa_wo, norm_ffn_w, ffn_wg, ffn_wu, ffn_conv_w, ffn_conv_b, ffn_wd, norm_final_w):
    w = dict(norm_mix_w=norm_mix_w, w_in=w_in, mlstm_bi=mlstm_bi, mlstm_bf=mlstm_bf, mlstm_norm_w=mlstm_norm_w,
             mla_qnorm_w=mla_qnorm_w, mla_kvnorm_w=mla_kvnorm_w, mla_w_uq=mla_w_uq, mla_w_uk=mla_w_uk,
             mla_w_uv=mla_w_uv, w_out=w_out, norm_x_w=norm_x_w, norm_mem_w=norm_mem_w, xa_wq=xa_wq, xa_wk=xa_wk,
             xa_wv=xa_wv, xa_wo=xa_wo, norm_ffn_w=norm_ffn_w, ffn_wg=ffn_wg, ffn_wu=ffn_wu,
             ffn_conv_w=ffn_conv_w, ffn_conv_b=ffn_conv_b, ffn_wd=ffn_wd)
    depth = w_in.shape[0]
    nbp, S, _ = x_prompt.shape
    nbs, T, _ = x_sample.shape
    n_pages = page_table.shape[1]
    past_len = n_pages * PAGE_SIZE
    mp, ms = nbp * S, nbs * T
    assert S % CHUNK == 0 and CHUNK % T == 0 and T % 8 == 0 and ms % CHUNK == 0

    tm_p = min(512, S)
    tm_s = min(512, ms)
    tq = min(512, S)
    tk = min(512, S)
    ns_s = CHUNK // T
    xg = 4 if nbs % 4 == 0 else 1
    tm_ffn_p = min(512, S)
    tm_ffn_s = min(256, ms)

    pos_p = _rope_tables(jnp.arange(S), 1)
    pos_s = _rope_tables(past_len + jnp.arange(T), tm_s // T)
    fnw = norm_final_w[None, :]

    xp = x_prompt.reshape(mp, D_MODEL)
    xs = x_sample.reshape(ms, D_MODEL)
    mem = mem_prompt.reshape(nbp * MEM_LEN, D_MODEL)

    zero_c = jnp.zeros((nbp, B_HEADS, B_HEAD_DIM, B_HEAD_DIM), F32)
    zero_n = jnp.zeros((nbp, B_HEADS, B_HEAD_DIM), F32)
    zero_m = jnp.zeros((nbp, CHUNK, LANE), F32)
    zero_conv = jnp.zeros((nbp, 1, D_FF), F32)
    cache_t = jnp.swapaxes(cache_mla, 2, 3)

    outs = {k: [] for k in ('lat_p', 'lat_s', 'mk', 'mv', 'cp', 'np', 'mp', 'cs', 'ns', 'ms', 'conv_p', 'conv_s',
                            'chunkv')}
    yp = ys = None
    for l in range(depth):
        lw = _layer_weights(l, w)
        last = l == depth - 1

        mla_p = functools.partial(_mla_prompt, wuv=lw['wuv'], nb=nbp, seq=S, tq=tq, tk=tk)
        ymix, st, lat, _ = _mixer_block(
            xp, lw, mlstm_state=(zero_c, zero_n, zero_m), pos_tables=pos_p,
            chunk_w=_chunk_mixers(chunk_ws[l], chunk_b[l], CHUNK), tm=tm_p, act_dtype=BF16, ns=1, n_tiles=nbp,
            n_chunks=S // CHUNK, mla_fn=lambda qc, lt: mla_p(qc, lt))
        xp, qx_p = _outproj(xp, *ymix, lw['wo'], lw['nw_x'], lw['xwq'], tm=tm_p, q_dtype=BF16)
        outs['lat_p'].append(lat.reshape(nbp, S, LATENT))
        outs['cp'].append(st[0])
        outs['np'].append(st[1])
        outs['mp'].append(st[2][:, CHUNK - 1, :B_HEADS])

        m0_rows = jnp.concatenate([jnp.repeat(state_mlstm_m[l], T, axis=0),
                                   jnp.zeros((ms, LANE - B_HEADS), F32)], axis=1).reshape(ms // CHUNK, CHUNK, LANE)
        mla_s = functools.partial(_mla_sample, page_table, wuv=lw['wuv'], cache_t=cache_t, layer=l, t_new=T,
                                  seqs=1)
        ymix, st, lat, gv = _mixer_block(
            xs, lw, mlstm_state=(state_mlstm_C[l], state_mlstm_n[l], m0_rows), pos_tables=pos_s,
            chunk_w=_chunk_mixers(chunk_ws[l], chunk_b[l], T), tm=tm_s, act_dtype=F32, ns=ns_s,
            n_tiles=ms // CHUNK, n_chunks=1, mla_fn=lambda qc, lt: mla_s(qc, lt))
        xs, qx_s = _outproj(xs, *ymix, lw['wo'], lw['nw_x'], lw['xwq'], tm=tm_s, q_dtype=F32)
        outs['lat_s'].append(lat.reshape(nbs, T, LATENT))
        outs['cs'].append(st[0])
        outs['ns'].append(st[1])
        outs['ms'].append(st[2].reshape(nbs, T, LANE)[:, T - 1, :B_HEADS])
        outs['chunkv'].append(gv.reshape(nbs, T, A_WIDTH))

        mk, mv, mk_b, mv_b = _memkv(mem, lw['nw_mem'], lw['xwk'], lw['xwv'], tm=min(512, nbp * MEM_LEN))
        outs['mk'].append(mk.reshape(nbp, MEM_LEN, X_HEADS, X_HEAD_DIM))
        outs['mv'].append(mv.reshape(nbp, MEM_LEN, X_HEADS, X_HEAD_DIM))
        xp = _xattn_prompt(xp, qx_p, mk_b, mv_b, lw['xwo'], tm=tm_p, tiles_per_seq=S // tm_p)
        xs = _xattn_sample(xs, qx_s, cache_mem_k, cache_mem_v, lw['xwo'], layer=l, n_seq=xg, t_rows=T)

        res_p = _ffn(xp, lw['nw_ffn'], lw['wg'], lw['wu'], lw['wd'], lw['cw'], lw['cb'], (zero_conv, zero_conv),
                     fnw, tm=tm_ffn_p, seq_rows=0, tiles_per_seq=S // tm_ffn_p, final_norm=last)
        res_s = _ffn(xs, lw['nw_ffn'], lw['wg'], lw['wu'], lw['wd'], lw['cw'], lw['cb'], state_ffn_conv, fnw,
                     tm=tm_ffn_s, seq_rows=T, tiles_per_seq=1, final_norm=last, layer=l)
        xp, xs = res_p[0], res_s[0]
        outs['conv_p'].append(res_p[1][S // tm_ffn_p - 1::S // tm_ffn_p])
        outs['conv_s'].append(res_s[1])
        if last:
            yp, ys = res_p[2], res_s[2]

    st = lambda k: jnp.stack(outs[k])
    return (yp.reshape(nbp, S, D_MODEL), ys.reshape(nbs, T, D_MODEL),
            st('lat_p'), st('lat_s'), st('mk'), st('mv'),
            st('cp'), st('np'), st('mp'), st('cs'), st('ns'), st('ms'),
            st('conv_p'), st('conv_s'), st('chunkv'))
```
